```python
import math
import jax
import jax.numpy as jnp
from jax import lax
import numpy as np

D_MODEL = 2048
BATCH = 1
SEQ = 8192
DEPTH = 2

GRID_W = 64
CTX_LEN = 256
D_MIX = D_MODEL
Q_BLOCK = 128
ROPE_THETA = 10000.0
EPS = 1e-6
GROUP_W = D_MIX // 4

A_HEAD_DIM = 64
A_Q_HEADS = GROUP_W // A_HEAD_DIM
A_KV_HEADS = 2
A_COLS = GROUP_W + 2 * A_KV_HEADS * A_HEAD_DIM

HY_WIDTH = GROUP_W
HY_SHORT = 3
HY_EMB = 33
HY_BANDS = (HY_EMB - 1) // 2
HY_FFN = 64
HY_FAST_DECAY = 0.3
HY_SLOW_DECAY = 1.5
HY_TARGET = 1e-2
HY_COLS = 3 * HY_WIDTH

DN_WIDTH = GROUP_W
DN_HEAD_DIM = 128
DN_HEADS = DN_WIDTH // DN_HEAD_DIM
DN_SHORT = 3
DN_CHUNK = 64
DN_COLS = 4 * DN_WIDTH + 4 * DN_HEADS

DF_HEAD_DIM = 64
DF_HEADS = GROUP_W // (2 * DF_HEAD_DIM)
DF_COLS = 3 * GROUP_W

D_FF = 5632
FFN_CONV = 3

IN_COLS = A_COLS + HY_COLS + DN_COLS + DF_COLS

kernel_name = 'hybrid_parallel_heads_flow_block'


def rmsnorm(x, gain):
    xf = x.astype(jnp.float32)
    y = xf * lax.rsqrt(jnp.mean(xf * xf, axis=-1, keepdims=True) + EPS)
    return (y * gain.astype(jnp.float32)).astype(x.dtype)


def l2norm(x):
    xf = x.astype(jnp.float32)
    return (xf * lax.rsqrt(jnp.sum(xf * xf, axis=-1, keepdims=True) + EPS)).astype(x.dtype)


def modulate(h, shift, scale):
    return h * (1 + scale) + shift


def dwconv_centred(x, w):
    k = w.shape[0]
    return lax.conv_general_dilated(
        x, w[:, None, :].astype(x.dtype), window_strides=(1,),
        padding=[(k // 2, k // 2)], dimension_numbers=('NWC', 'WIO', 'NWC'),
        feature_group_count=x.shape[-1])


def axial_rope(rows, dim):
    row = jnp.repeat(jnp.arange(rows, dtype=jnp.float32), GRID_W)
    col = jnp.tile(jnp.arange(GRID_W, dtype=jnp.float32), rows)
    n_freq = dim // 4
    inv = ROPE_THETA ** (-jnp.arange(n_freq, dtype=jnp.float32) / n_freq)
    ang = jnp.concatenate([row[:, None] * inv, col[:, None] * inv], axis=-1)
    return jnp.cos(ang), jnp.sin(ang)


def apply_rope(x, cos, sin):
    shape = (cos.shape[0],) + (1,) * (x.ndim - 3) + (cos.shape[1],)
    cos = cos.reshape(shape)
    sin = sin.reshape(shape)
    x1, x2 = jnp.split(x.astype(jnp.float32), 2, axis=-1)
    return jnp.concatenate([x1 * cos - x2 * sin, x1 * sin + x2 * cos], axis=-1).astype(x.dtype)


def gqa_blocks(q, k, v):
    b, s, hq, d = q.shape
    hkv = k.shape[2]
    nb = s // Q_BLOCK
    qb = q.reshape(b, nb, Q_BLOCK, hkv, hq // hkv, d).transpose(1, 0, 3, 4, 2, 5)
    scale = d ** -0.5

    def one(qblk):
        sc = jnp.einsum('bhgqd,bkhd->bhgqk', qblk, k).astype(jnp.float32) * scale
        p = jax.nn.softmax(sc, axis=-1).astype(v.dtype)
        return jnp.einsum('bhgqk,bkhd->bhgqd', p, v)

    o = lax.map(one, qb)
    return o.transpose(1, 0, 4, 2, 3, 5).reshape(b, s, hq * d)


def diff_blocks(q, k, v, lam):
    b, s, h, _, d = q.shape
    nb = s // Q_BLOCK
    qb = q.reshape(b, nb, Q_BLOCK, h, 2, d).transpose(1, 0, 3, 4, 2, 5)
    scale = d ** -0.5

    def one(qblk):
        sc = jnp.einsum('bhmqd,bkhmd->bhmqk', qblk, k).astype(jnp.float32) * scale
        p = jax.nn.softmax(sc, axis=-1)
        a = (p[:, :, 0] - lam * p[:, :, 1]).astype(v.dtype)
        return jnp.einsum('bhqk,bkhe->bhqe', a, v)

    o = lax.map(one, qb)
    return o.transpose(1, 0, 3, 2, 4).reshape(b, s, h, v.shape[-1])


def mixer_gqa(p_lat, p_ctx, q_gain, k_gain, rope, update_ctx):
    def heads(p):
        b, l, _ = p.shape
        q, k, v = jnp.split(p, [GROUP_W, GROUP_W + A_KV_HEADS * A_HEAD_DIM], axis=-1)
        q = rmsnorm(q.reshape(b, l, A_Q_HEADS, A_HEAD_DIM), q_gain)
        k = rmsnorm(k.reshape(b, l, A_KV_HEADS, A_HEAD_DIM), k_gain)
        return q, k, v.reshape(b, l, A_KV_HEADS, A_HEAD_DIM)

    q, k, v = heads(p_lat)
    qc, kc, vc = heads(p_ctx)
    cos, sin = rope
    q = apply_rope(q, cos, sin)
    k = apply_rope(k, cos, sin)
    y = gqa_blocks(q, jnp.concatenate([kc, k], axis=1), jnp.concatenate([vc, v], axis=1))
    y_ctx = gqa_blocks(qc, kc, vc) if update_ctx else None
    return y, y_ctx


def mixer_diff(p_lat, p_ctx, lam_vecs, sub_gain, lam_init, rope, update_ctx):
    def heads(p):
        b, l, _ = p.shape
        q, k, v = jnp.split(p, 3, axis=-1)
        return (q.reshape(b, l, DF_HEADS, 2, DF_HEAD_DIM),
                k.reshape(b, l, DF_HEADS, 2, DF_HEAD_DIM),
                v.reshape(b, l, DF_HEADS, 2 * DF_HEAD_DIM))

    lv = lam_vecs.astype(jnp.float32)
    lam = jnp.exp(jnp.sum(lv[0] * lv[1])) - jnp.exp(jnp.sum(lv[2] * lv[3])) + lam_init

    def finish(o):
        b, l = o.shape[:2]
        return (rmsnorm(o, sub_gain) * (1.0 - lam_init)).reshape(b, l, GROUP_W)

    q, k, v = heads(p_lat)
    qc, kc, vc = heads(p_ctx)
    cos, sin = rope
    q = apply_rope(q, cos, sin)
    k = apply_rope(k, cos, sin)
    y = finish(diff_blocks(q, jnp.concatenate([kc, k], axis=1), jnp.concatenate([vc, v], axis=1), lam))
    y_ctx = finish(diff_blocks(qc, kc, vc, lam)) if update_ctx else None
    return y, y_ctx


def hyena_filters(l, w1, b1, w2, b2, w3, b3, w4, freq):
    f32 = jnp.float32
    t = jnp.linspace(0.0, 1.0, l, dtype=f32)[:, None]
    w = 2.0 * math.pi * jnp.arange(l, dtype=f32)[:, None] / l
    f = jnp.linspace(1e-4, HY_BANDS - 1, HY_BANDS, dtype=f32)[None, :]
    z = jnp.concatenate([t, jnp.cos(f * w), -jnp.sin(f * w)], axis=-1)
    fr = freq.astype(f32)
    h = jnp.sin(fr * (z @ w1.astype(f32) + b1.astype(f32)))
    h = jnp.sin(fr * (h @ w2.astype(f32) + b2.astype(f32)))
    h = jnp.sin(fr * (h @ w3.astype(f32) + b3.astype(f32)))
    h = h @ w4.astype(f32)
    min_decay = math.log(HY_TARGET) / HY_SLOW_DECAY
    max_decay = math.log(HY_TARGET) / HY_FAST_DECAY
    deltas = jnp.linspace(min_decay, max_decay, HY_WIDTH, dtype=f32)
    window = jnp.exp(-t * jnp.abs(deltas))
    h = h.reshape(l, 2, HY_WIDTH) * window[:, None, :]
    return h[:, 0], h[:, 1]


def bidir_long_conv(u, h_fwd, h_bwd, d_skip):
    l, ch = u.shape[1], u.shape[2]
    kern = jnp.concatenate([h_fwd, jnp.zeros((1, ch), jnp.float32), h_bwd[:0:-1]], axis=0)
    uf = u.astype(jnp.float32)
    uk = jnp.fft.rfft(uf, n=2 * l, axis=1) * jnp.fft.rfft(kern, n=2 * l, axis=0)[None]
    y = jnp.fft.irfft(uk, n=2 * l, axis=1)[:, :l]
    return (y + uf * d_skip.astype(jnp.float32)).astype(u.dtype)


def mixer_hyena(p_lat, p_ctx, short_w, filt, d_skip, update_ctx):
    def run(p):
        uc = dwconv_centred(p, short_w)
        x0, x1, v = jnp.split(uc, 3, axis=-1)
        h_fwd, h_bwd = hyena_filters(p.shape[1], *filt)
        return x0 * bidir_long_conv(x1 * v, h_fwd, h_bwd, d_skip)

    return run(p_lat), (run(p_ctx) if update_ctx else None)


def gated_delta_chunks(q, k, v, beta, g, state0):
    b, l, h, dk = q.shape
    dv = v.shape[-1]
    n = l // DN_CHUNK
    f32 = jnp.float32

    def to_chunks(t):
        t = t.astype(f32).reshape((b, n, DN_CHUNK) + t.shape[2:])
        return jnp.swapaxes(t, 2, 3)

    q, k, v, beta, g = (to_chunks(t) for t in (q, k, v, beta, g))
    gc = jnp.cumsum(g, axis=-1)
    idx = jnp.arange(DN_CHUNK)
    incl = idx[:, None] >= idx[None, :]
    strict = idx[:, None] > idx[None, :]
    gamma = jnp.exp(jnp.where(incl, gc[..., :, None] - gc[..., None, :], -jnp.inf))
    kb = k * beta[..., None]
    m = jnp.where(strict, jnp.einsum('bnhid,bnhjd->bnhij', kb, k) * gamma, 0.0)
    a = m + jnp.eye(DN_CHUNK, dtype=f32)
    rhs = jnp.concatenate([v * beta[..., None], kb * jnp.exp(gc)[..., None]], axis=-1)
    sol = lax.linalg.triangular_solve(a, rhs, left_side=True, lower=True, unit_diagonal=True)
    u, w = sol[..., :dv], sol[..., dv:]
    qk = jnp.einsum('bnhid,bnhjd->bnhij', q, k) * gamma
    q_dec = q * jnp.exp(gc)[..., None]
    k_dec = k * jnp.exp(gc[..., -1:] - gc)[..., None]
    last = jnp.exp(gc[..., -1])

    def step(s, xs):
        u_c, w_c, qk_c, qd_c, kd_c, l_c = xs
        v_new = u_c - jnp.einsum('bhcd,bhde->bhce', w_c, s)
        o = jnp.einsum('bhcd,bhde->bhce', qd_c, s) + jnp.einsum('bhij,bhje->bhie', qk_c, v_new)
        s = s * l_c[..., None, None] + jnp.einsum('bhcd,bhce->bhde', kd_c, v_new)
        return s, o

    xs = tuple(jnp.moveaxis(t, 1, 0) for t in (u, w, qk, q_dec, k_dec, last))
    s_final, o = lax.scan(step, state0, xs)
    o = o.transpose(1, 0, 3, 2, 4).reshape(b, l, h, dv)
    return o, s_final


def mixer_deltanet(p_lat, p_ctx, short_w, a_log, dt_bias, o_gain, update_ctx):
    w_, h_ = DN_WIDTH, DN_HEADS

    def prep(p):
        b, l, _ = p.shape
        qkv, gate, bf, bb, af, ab = jnp.split(
            p, [3 * w_, 4 * w_, 4 * w_ + h_, 4 * w_ + 2 * h_, 4 * w_ + 3 * h_], axis=-1)
        qkv = jax.nn.silu(dwconv_centred(qkv, short_w))
        q, k, v = (t.reshape(b, l, h_, DN_HEAD_DIM) for t in jnp.split(qkv, 3, axis=-1))
        q = l2norm(q) * (DN_HEAD_DIM ** -0.5)
        k = l2norm(k)
        beta = jax.nn.sigmoid(jnp.stack([bf, bb]).astype(jnp.float32))
        g = -jnp.exp(a_log.astype(jnp.float32))[:, None, None, :] * jax.nn.softplus(
            jnp.stack([af, ab]).astype(jnp.float32) + dt_bias.astype(jnp.float32)[:, None, None, :])
        return q, k, v, gate, beta, g

    q, k, v, gate, beta, g = prep(p_lat)
    qc, kc, vc, gate_c, beta_c, g_c = prep(p_ctx)
    s0 = jnp.zeros((p_lat.shape[0], h_, DN_HEAD_DIM, DN_HEAD_DIM), jnp.float32)
    rev = lambda t: jnp.flip(t, axis=1)
    oc_f, s_f = gated_delta_chunks(qc, kc, vc, beta_c[0], g_c[0], s0)
    o_f, _ = gated_delta_chunks(q, k, v, beta[0], g[0], s_f)
    oc_b, s_b = gated_delta_chunks(rev(qc), rev(kc), rev(vc), rev(beta_c[1]), rev(g_c[1]), s0)
    o_b, _ = gated_delta_chunks(rev(q), rev(k), rev(v), rev(beta[1]), rev(g[1]), s_b)

    def finish(o_fwd, o_bwd_rev, gt):
        b, l = gt.shape[:2]
        o = (o_fwd + rev(o_bwd_rev)).astype(gt.dtype)
        return rmsnorm(o, o_gain).reshape(b, l, w_) * jax.nn.silu(gt)

    y = finish(o_f, o_b, gate)
    y_ctx = finish(oc_f, oc_b, gate_c) if update_ctx else None
    return y, y_ctx


def conv_ffn(h, w_up, w_conv, w_down):
    u = dwconv_centred(h @ w_up, w_conv)
    a, b = jnp.split(u, 2, axis=-1)
    return (jax.nn.silu(a) * b) @ w_down


def setup_inputs(seed: int = 0) -> dict:
    key = jax.random.key(seed)
    ks = iter(jax.random.split(key, 40))
    f32 = jnp.float32

    def nrm(shape, scale):
        return jax.random.normal(next(ks), shape, f32) * scale

    def gain(shape):
        return 1.0 + nrm(shape, 0.05)

    L_ = DEPTH
    dt = jnp.exp(jax.random.uniform(next(ks), (L_, 2, DN_HEADS), f32, math.log(1e-3), math.log(1e-1)))
    return {
        'x': nrm((BATCH, SEQ, D_MODEL), 1.0),
        'c': nrm((BATCH, D_MODEL), 1.0),
        'ctx': nrm((BATCH, CTX_LEN, D_MODEL), 1.0),
        'c_ctx': nrm((D_MODEL,), 1.0),
        'w_ada': nrm((L_, D_MODEL, 6 * D_MODEL), 0.5 * D_MODEL ** -0.5),
        'b_ada': nrm((L_, 6 * D_MODEL), 0.02),
        'norm_mix_pre': gain((L_, D_MODEL)),
        'norm_mix_post': gain((L_, D_MODEL)),
        'norm_ffn_pre': gain((L_, D_MODEL)),
        'norm_ffn_post': gain((L_, D_MODEL)),
        'w_in': nrm((L_, D_MODEL, IN_COLS), D_MODEL ** -0.5),
        'w_out': nrm((L_, D_MIX, D_MODEL), D_MIX ** -0.5),
        'attn_q_norm': gain((L_, A_HEAD_DIM)),
        'attn_k_norm': gain((L_, A_HEAD_DIM)),
        'hy_short': nrm((L_, HY_SHORT, HY_COLS), HY_SHORT ** -0.5),
        'hy_w1': nrm((L_, HY_EMB, HY_FFN), HY_EMB ** -0.5),
        'hy_b1': nrm((L_, HY_FFN), 0.02),
        'hy_w2': nrm((L_, HY_FFN, HY_FFN), HY_FFN ** -0.5),
        'hy_b2': nrm((L_, HY_FFN), 0.02),
        'hy_w3': nrm((L_, HY_FFN, HY_FFN), HY_FFN ** -0.5),
        'hy_b3': nrm((L_, HY_FFN), 0.02),
        'hy_w4': nrm((L_, HY_FFN, 2 * HY_WIDTH), 0.03 * HY_FFN ** -0.5),
        'hy_freq': gain((L_, HY_FFN)),
        'hy_skip': nrm((L_, HY_WIDTH), 0.5),
        'dn_short': nrm((L_, DN_SHORT, 3 * DN_WIDTH), DN_SHORT ** -0.5),
        'dn_a_log': jnp.log(jax.random.uniform(next(ks), (L_, 2, DN_HEADS), f32, 1.0, 16.0)),
        'dn_dt_bias': dt + jnp.log(-jnp.expm1(-dt)),
        'dn_norm': gain((L_, DN_HEAD_DIM)),
        'df_lambda': nrm((L_, 4, DF_HEAD_DIM), 0.1),
        'df_norm': gain((L_, 2 * DF_HEAD_DIM)),
        'ffn_up': nrm((L_, D_MODEL, 2 * D_FF), D_MODEL ** -0.5),
        'ffn_conv': nrm((L_, FFN_CONV, 2 * D_FF), FFN_CONV ** -0.5),
        'ffn_down': nrm((L_, D_FF, D_MODEL), D_FF ** -0.5),
    }


def reference(x, c, ctx, c_ctx, w_ada, b_ada, norm_mix_pre, norm_mix_post, norm_ffn_pre,
              norm_ffn_post, w_in, w_out, attn_q_norm, attn_k_norm, hy_short, hy_w1, hy_b1,
              hy_w2, hy_b2, hy_w3, hy_b3, hy_w4, hy_freq, hy_skip, dn_short, dn_a_log,
              dn_dt_bias, dn_norm, df_lambda, df_norm, ffn_up, ffn_conv, ffn_down):
    rows = x.shape[1] // GRID_W
    rope = axial_rope(rows, A_HEAD_DIM)
    splits = [A_COLS, A_COLS + HY_COLS, A_COLS + HY_COLS + DN_COLS]
    xc = ctx
    for i in range(DEPTH):
        update_ctx = i < DEPTH - 1
        lam_init = 0.8 - 0.6 * math.exp(-0.3 * i)
        mod = jax.nn.silu(c) @ w_ada[i] + b_ada[i]
        mod_c = jax.nn.silu(c_ctx) @ w_ada[i] + b_ada[i]
        sh1, sc1, gt1, sh2, sc2, gt2 = jnp.split(mod[:, None, :], 6, axis=-1)
        sh1c, sc1c, gt1c, sh2c, sc2c, gt2c = jnp.split(mod_c, 6, axis=-1)

        p_lat = modulate(rmsnorm(x, norm_mix_pre[i]), sh1, sc1) @ w_in[i]
        p_ctx = modulate(rmsnorm(xc, norm_mix_pre[i]), sh1c, sc1c) @ w_in[i]
        a_l, b_l, c_l, d_l = jnp.split(p_lat, splits, axis=-1)
        a_c, b_c, c_c, d_c = jnp.split(p_ctx, splits, axis=-1)
        ya, ya_c = mixer_gqa(a_l, a_c, attn_q_norm[i], attn_k_norm[i], rope, update_ctx)
        yb, yb_c = mixer_hyena(b_l, b_c, hy_short[i],
                               (hy_w1[i], hy_b1[i], hy_w2[i], hy_b2[i], hy_w3[i], hy_b3[i],
                                hy_w4[i], hy_freq[i]), hy_skip[i], update_ctx)
        yc, yc_c = mixer_deltanet(c_l, c_c, dn_short[i], dn_a_log[i], dn_dt_bias[i], dn_norm[i], update_ctx)
        yd, yd_c = mixer_diff(d_l, d_c, df_lambda[i], df_norm[i], lam_init, rope, update_ctx)
        y = jnp.concatenate([ya, yb, yc, yd], axis=-1) @ w_out[i]
        x = x + gt1 * rmsnorm(y, norm_mix_post[i])
        if update_ctx:
            y_c = jnp.concatenate([ya_c, yb_c, yc_c, yd_c], axis=-1) @ w_out[i]
            xc = xc + gt1c * rmsnorm(y_c, norm_mix_post[i])

        h = modulate(rmsnorm(x, norm_ffn_pre[i]), sh2, sc2)
        x = x + gt2 * rmsnorm(conv_ffn(h, ffn_up[i], ffn_conv[i], ffn_down[i]), norm_ffn_post[i])
        if update_ctx:
            hc = modulate(rmsnorm(xc, norm_ffn_pre[i]), sh2c, sc2c)
            xc = xc + gt2c * rmsnorm(conv_ffn(hc, ffn_up[i], ffn_conv[i], ffn_down[i]), norm_ffn_post[i])
    return x
```

```python
import functools
import math

import jax
import jax.numpy as jnp
import numpy as np
from jax import lax
from jax.experimental import pallas as pl
from jax.experimental.pallas import tpu as pltpu

F32 = jnp.float32
BF16 = jnp.bfloat16
HI = lax.Precision.HIGHEST
EPS = 1e-6

GRID_W = 64
ROPE_THETA = 10000.0
HEAD_D = 64
GQA_GROUP = 4
DN_HEAD = 128
DN_CHUNK = 64
HY_BANDS = 16
HY_FAST_DECAY, HY_SLOW_DECAY, HY_TARGET = 0.3, 1.5, 1e-2
FFT_N2 = 128
VMEM_LIMIT = 56 * 1024 * 1024
NT = (((1,), (1,)), ((), ()))
TN = (((0,), (0,)), ((), ()))


def _cp(*sem):
    return pltpu.CompilerParams(dimension_semantics=sem, vmem_limit_bytes=VMEM_LIMIT)


def _dot(a, b, precision=None):
    return jnp.dot(a, b, precision=precision, preferred_element_type=F32)


def _silu(x):
    return x * jax.nn.sigmoid(x)


def _row_tile(n, cands):
    for t in cands:
        if n % t == 0:
            return t
    raise ValueError(f"no row tile for {n}")


def _mod_kernel(c_ref, w_ref, b_ref, o_ref):
    o_ref[0] = _dot(_silu(c_ref[...]), w_ref[0], HI) + b_ref[0]


def _mod_vectors(c_rows, w_ada, b_ada):
    depth, d, n = w_ada.shape
    tn = 512
    return pl.pallas_call(
        _mod_kernel,
        out_shape=jax.ShapeDtypeStruct((depth, 8, n), F32),
        grid=(depth, n // tn),
        in_specs=[pl.BlockSpec((8, d), lambda l, j: (0, 0)),
                  pl.BlockSpec((1, d, tn), lambda l, j: (l, 0, j)),
                  pl.BlockSpec((1, 1, tn), lambda l, j: (l, 0, j))],
        out_specs=pl.BlockSpec((1, 8, tn), lambda l, j: (l, 0, j)),
        compiler_params=_cp("parallel", "parallel"),
    )(c_rows, w_ada, b_ada.reshape(depth, 1, n))


HALO = 16


def _k1_kernel(xp_ref, xm_ref, xn_ref, g_ref, mod_ref, *rest, tm, ctx, s_tot, ffn):
    if ffn:
        wa_ref, wb_ref, cwa_ref, cwb_ref, o_ref, h_ref = rest
    else:
        wa_ref, cwa_ref, o_ref, h_ref = rest
    i = pl.program_id(0)
    j = pl.program_id(1)

    @pl.when(j == 0)
    def _():
        def nm(xv, row0):
            ms = jnp.mean(xv * xv, axis=-1, keepdims=True)
            y = xv * lax.rsqrt(ms + EPS) * g_ref[...]
            rows = row0 + lax.broadcasted_iota(jnp.int32, (xv.shape[0], 1), 0)
            is_ctx = rows < ctx
            sh = jnp.where(is_ctx, mod_ref[2:3, :], mod_ref[0:1, :])
            sc = jnp.where(is_ctx, mod_ref[3:4, :], mod_ref[1:2, :])
            return (y * (1.0 + sc) + sh).astype(BF16)

        h_ref[0:HALO] = nm(xp_ref[...], i * tm - HALO)
        h_ref[HALO:HALO + tm] = nm(xm_ref[...], i * tm)
        h_ref[HALO + tm:] = nm(xn_ref[...], i * tm + tm)

    h = h_ref[...]
    rows = i * tm + lax.broadcasted_iota(jnp.int32, (tm, 1), 0)
    mprev = jnp.logical_and(rows != 0, rows != ctx)
    mnext = jnp.logical_and(rows != ctx - 1, rows != s_tot - 1)

    def conv(w_ref, cw_ref):
        acc = _dot(h, w_ref[...])
        prev = pltpu.roll(acc, 1, 0)[HALO:HALO + tm]
        nxt = pltpu.roll(acc, tm + 2 * HALO - 1, 0)[HALO:HALO + tm]
        cur = acc[HALO:HALO + tm]
        cw = cw_ref[...]
        return (cur * cw[1:2] + jnp.where(mprev, prev, 0.0) * cw[0:1]
                + jnp.where(mnext, nxt, 0.0) * cw[2:3])

    if ffn:
        a = conv(wa_ref, cwa_ref)
        b = conv(wb_ref, cwb_ref)
        o_ref[...] = (_silu(a) * b).astype(o_ref.dtype)
    else:
        o_ref[...] = conv(wa_ref, cwa_ref).astype(o_ref.dtype)


def _norm_mod_matmul_conv(xx, gain, mod4, w, cw, *, ctx, ffn):
    s_tot, d = xx.shape
    n = w.shape[1]
    tm = _row_tile(s_tot, (528, 640, 256, 128))
    tn = 512
    nb = s_tot // HALO
    n_out = n // 2 if ffn else n
    x_specs = [
        pl.BlockSpec((HALO, d), lambda i, j: (jnp.maximum(i * (tm // HALO) - 1, 0), 0)),
        pl.BlockSpec((tm, d), lambda i, j: (i, 0)),
        pl.BlockSpec((HALO, d), lambda i, j: (jnp.minimum((i + 1) * (tm // HALO), nb - 1), 0)),
        pl.BlockSpec((1, d), lambda i, j: (0, 0)),
        pl.BlockSpec((8, d), lambda i, j: (0, 0)),
    ]
    if ffn:
        off = n_out // tn
        w_specs = [pl.BlockSpec((d, tn), lambda i, j: (0, j)),
                   pl.BlockSpec((d, tn), lambda i, j: (0, j + off)),
                   pl.BlockSpec((8, tn), lambda i, j: (0, j)),
                   pl.BlockSpec((8, tn), lambda i, j: (0, j + off))]
        args = (w, w, cw, cw)
        out_dtype = BF16
    else:
        w_specs = [pl.BlockSpec((d, tn), lambda i, j: (0, j)),
                   pl.BlockSpec((8, tn), lambda i, j: (0, j))]
        args = (w, cw)
        out_dtype = F32
    return pl.pallas_call(
        functools.partial(_k1_kernel, tm=tm, ctx=ctx, s_tot=s_tot, ffn=ffn),
        out_shape=jax.ShapeDtypeStruct((s_tot, n_out), out_dtype),
        grid=(s_tot // tm, n_out // tn),
        in_specs=x_specs + w_specs,
        out_specs=pl.BlockSpec((tm, tn), lambda i, j: (i, j)),
        scratch_shapes=[pltpu.VMEM((tm + 2 * HALO, d), BF16)],
        compiler_params=_cp("parallel", "arbitrary"),
    )(xx, xx, xx, gain, mod4, *args)


def _k2_kernel(a_ref, w_ref, x_ref, gate_ref, gain_ref, o_ref, acc_ref, *, tm, ctx, nk):
    i = pl.program_id(0)
    k = pl.program_id(1)

    @pl.when(k == 0)
    def _():
        acc_ref[...] = jnp.zeros_like(acc_ref)

    acc_ref[...] += _dot(a_ref[...], w_ref[...])

    @pl.when(k == nk - 1)
    def _():
        y = acc_ref[...]
        r = y * lax.rsqrt(jnp.mean(y * y, axis=-1, keepdims=True) + EPS) * gain_ref[...]
        rows = i * tm + lax.broadcasted_iota(jnp.int32, (tm, 1), 0)
        gt = jnp.where(rows < ctx, gate_ref[1:2, :], gate_ref[0:1, :])
        o_ref[...] = x_ref[...] + gt * r


def _matmul_norm_residual(a, w, xx, gate2, gain, *, ctx):
    s_tot, kdim = a.shape
    d = w.shape[1]
    tm = _row_tile(s_tot, (528, 640, 256, 128))
    tk = 512
    nk = kdim // tk
    return pl.pallas_call(
        functools.partial(_k2_kernel, tm=tm, ctx=ctx, nk=nk),
        out_shape=jax.ShapeDtypeStruct((s_tot, d), F32),
        grid=(s_tot // tm, nk),
        in_specs=[pl.BlockSpec((tm, tk), lambda i, k: (i, k)),
                  pl.BlockSpec((tk, d), lambda i, k: (k, 0)),
                  pl.BlockSpec((tm, d), lambda i, k: (i, 0)),
                  pl.BlockSpec((8, d), lambda i, k: (0, 0)),
                  pl.BlockSpec((1, d), lambda i, k: (0, 0))],
        out_specs=pl.BlockSpec((tm, d), lambda i, k: (i, 0)),
        scratch_shapes=[pltpu.VMEM((tm, d), F32)],
        compiler_params=_cp("parallel", "arbitrary"),
    )(a, w, xx, gate2, gain)


def _attn_prep_kernel(aq_ref, akv_ref, dq_ref, dk_ref, dv_ref, qgain_ref, kgain_ref, cos_ref, sin_ref,
                      gmat_ref, qg_ref, kg_ref, vg_ref, qd_ref, kd_ref, vd_ref):
    cosf = cos_ref[...]
    sins = sin_ref[...]
    lane = lax.broadcasted_iota(jnp.int32, (1, 128), 1)
    first_half = (lane % HEAD_D) < (HEAD_D // 2)
    scale = HEAD_D ** -0.5

    def rope(x):
        partner = jnp.where(first_half, pltpu.roll(x, 128 - HEAD_D // 2, 1), pltpu.roll(x, HEAD_D // 2, 1))
        return x * cosf + partner * sins

    def headnorm(x, gain):
        ms = _dot(x * x, gmat_ref[...], HI)
        return x * lax.rsqrt(ms + EPS) * gain

    def put(ref, pair, val):
        vb = val.astype(ref.dtype)
        ref[2 * pair] = vb[:, :HEAD_D]
        ref[2 * pair + 1] = vb[:, HEAD_D:]

    for ci in range(4):
        sl = slice(ci * 128, (ci + 1) * 128)
        put(qg_ref, ci, rope(headnorm(aq_ref[:, sl], qgain_ref[...])) * scale)
        put(qd_ref, ci, rope(dq_ref[:, sl]) * scale)
        put(kd_ref, ci, rope(dk_ref[:, sl]))
        vd_ref[ci] = dv_ref[:, sl].astype(vd_ref.dtype)
    put(kg_ref, 0, rope(headnorm(akv_ref[:, 0:128], kgain_ref[...])))
    put(vg_ref, 0, akv_ref[:, 128:256])


def _attn_prep(p, qgain, kgain, cos_t, sin_t, gmat):
    s_tot = p.shape[0]
    tm = _row_tile(s_tot, (528, 640, 256, 128))
    hm = lambda nh, dd: jax.ShapeDtypeStruct((nh, s_tot, dd), BF16)
    hspec = lambda nh, dd: pl.BlockSpec((nh, tm, dd), lambda i: (0, i, 0))
    return pl.pallas_call(
        _attn_prep_kernel,
        out_shape=(hm(8, 64), hm(2, 64), hm(2, 64), hm(8, 64), hm(8, 64), hm(4, 128)),
        grid=(s_tot // tm,),
        in_specs=[pl.BlockSpec((tm, 512), lambda i: (i, 10)),
                  pl.BlockSpec((tm, 256), lambda i: (i, 22)),
                  pl.BlockSpec((tm, 512), lambda i: (i, 7)),
                  pl.BlockSpec((tm, 512), lambda i: (i, 8)),
                  pl.BlockSpec((tm, 512), lambda i: (i, 9)),
                  pl.BlockSpec((1, 128), lambda i: (0, 0)),
                  pl.BlockSpec((1, 128), lambda i: (0, 0)),
                  pl.BlockSpec((tm, 128), lambda i: (i, 0)),
                  pl.BlockSpec((tm, 128), lambda i: (i, 0)),
                  pl.BlockSpec((128, 128), lambda i: (0, 0))],
        out_specs=(hspec(8, 64), hspec(2, 64), hspec(2, 64), hspec(8, 64), hspec(8, 64), hspec(4, 128)),
        compiler_params=_cp("parallel"),
    )(p, p, p, p, p, qgain, kgain, cos_t, sin_t, gmat)


def _flash(q, k_ref, kidx, v_ref, nkv, tk, m_ref, l_ref, acc_ref):
    m_ref[...] = jnp.full_like(m_ref, -jnp.inf)
    l_ref[...] = jnp.zeros_like(l_ref)
    acc_ref[...] = jnp.zeros_like(acc_ref)

    def body(c, carry):
        off = pl.multiple_of(c * tk, tk)
        k = k_ref[kidx, pl.ds(off, tk), :]
        v = v_ref[0, pl.ds(off, tk), :]
        s = lax.dot_general(q, k, NT, preferred_element_type=F32)
        m_prev = m_ref[...]
        m_new = jnp.maximum(m_prev, jnp.max(s, axis=1, keepdims=True))
        alpha = jnp.exp(m_prev - m_new)
        pr = jnp.exp(s - m_new)
        l_ref[...] = alpha * l_ref[...] + jnp.sum(pr, axis=1, keepdims=True)
        acc_ref[...] = alpha * acc_ref[...] + _dot(pr.astype(BF16), v)
        m_ref[...] = m_new
        return carry

    lax.fori_loop(0, nkv, body, 0)
    return acc_ref[...] / l_ref[...]


def _gqa_kernel(q_ref, k_ref, v_ref, o_ref, m_ref, l_ref, acc_ref, *, tq, tk, ctx, s_tot):
    nkv = jnp.where(pl.program_id(1) < ctx // tq, ctx // tk, s_tot // tk)
    q = q_ref[...].reshape(GQA_GROUP * tq, HEAD_D)
    o = _flash(q, k_ref, 0, v_ref, nkv, tk, m_ref, l_ref, acc_ref)
    for h in range(GQA_GROUP):
        o_ref[:, h * HEAD_D:(h + 1) * HEAD_D] = o[h * tq:(h + 1) * tq].astype(o_ref.dtype)


def _gqa_attention(qg, kg, vg, *, ctx):
    nh, s_tot, _ = qg.shape
    tq = tk = 256
    m = GQA_GROUP * tq
    return pl.pallas_call(
        functools.partial(_gqa_kernel, tq=tq, tk=tk, ctx=ctx, s_tot=s_tot),
        out_shape=jax.ShapeDtypeStruct((s_tot, nh * HEAD_D), BF16),
        grid=(nh // GQA_GROUP, s_tot // tq),
        in_specs=[pl.BlockSpec((GQA_GROUP, tq, HEAD_D), lambda g, i: (g, i, 0)),
                  pl.BlockSpec((1, s_tot, HEAD_D), lambda g, i: (g, 0, 0)),
                  pl.BlockSpec((1, s_tot, HEAD_D), lambda g, i: (g, 0, 0))],
        out_specs=pl.BlockSpec((tq, GQA_GROUP * HEAD_D), lambda g, i: (i, g)),
        scratch_shapes=[pltpu.VMEM((m, 1), F32), pltpu.VMEM((m, 1), F32), pltpu.VMEM((m, HEAD_D), F32)],
        compiler_params=_cp("parallel", "parallel"),
    )(qg, kg, vg)


def _diff_kernel(q_ref, k_ref, v_ref, lam_ref, gain_ref, o_ref, m_ref, l_ref, acc_ref, *, tq, tk, ctx, s_tot,
                 lam_init):
    nkv = jnp.where(pl.program_id(1) < ctx // tq, ctx // tk, s_tot // tk)
    o0 = _flash(q_ref[0], k_ref, 0, v_ref, nkv, tk, m_ref, l_ref, acc_ref)
    o1 = _flash(q_ref[1], k_ref, 1, v_ref, nkv, tk, m_ref, l_ref, acc_ref)
    lv = lam_ref[...]
    lam = (jnp.exp(jnp.sum(lv[0:1] * lv[1:2], axis=1, keepdims=True))
           - jnp.exp(jnp.sum(lv[2:3] * lv[3:4], axis=1, keepdims=True)) + lam_init)
    o = o0 - lam * o1
    y = o * lax.rsqrt(jnp.mean(o * o, axis=-1, keepdims=True) + EPS) * gain_ref[...]
    o_ref[...] = (y * (1.0 - lam_init)).astype(o_ref.dtype)


def _diff_attention(qd, kd, vd, lam_vecs, sub_gain, *, ctx, lam_init):
    nh, s_tot, dv = vd.shape
    tq = tk = 256
    return pl.pallas_call(
        functools.partial(_diff_kernel, tq=tq, tk=tk, ctx=ctx, s_tot=s_tot, lam_init=lam_init),
        out_shape=jax.ShapeDtypeStruct((s_tot, nh * dv), BF16),
        grid=(nh, s_tot // tq),
        in_specs=[pl.BlockSpec((2, tq, HEAD_D), lambda h, i: (h, i, 0)),
                  pl.BlockSpec((2, s_tot, HEAD_D), lambda h, i: (h, 0, 0)),
                  pl.BlockSpec((1, s_tot, dv), lambda h, i: (h, 0, 0)),
                  pl.BlockSpec(lam_vecs.shape, lambda h, i: (0, 0)),
                  pl.BlockSpec((1, dv), lambda h, i: (0, 0))],
        out_specs=pl.BlockSpec((tq, dv), lambda h, i: (i, h)),
        scratch_shapes=[pltpu.VMEM((tq, 1), F32), pltpu.VMEM((tq, 1), F32), pltpu.VMEM((tq, dv), F32)],
        compiler_params=_cp("parallel", "parallel"),
    )(qd, kd, vd, lam_vecs, sub_gain)


def _hy_filter_kernel(w1_ref, b1_ref, w2_ref, b2_ref, w3_ref, b3_ref, w4_ref, fr_ref, o_ref, *, tp, l, n, width):
    i = pl.program_id(0)
    idx = i * tp + lax.broadcasted_iota(jnp.int32, (tp, 1), 0)
    pos = jnp.where(idx < l, idx, n - idx).astype(F32)
    t = pos / (l - 1.0)
    w = (2.0 * math.pi) * pos / l
    lane = lax.broadcasted_iota(jnp.int32, (1, 128), 1)
    band = jnp.where(lane <= HY_BANDS, lane - 1, lane - 1 - HY_BANDS).astype(F32)
    f = 1e-4 + band * ((HY_BANDS - 1 - 1e-4) / (HY_BANDS - 1))
    ang = w * f
    z = jnp.where(lane == 0, t,
                  jnp.where(lane <= HY_BANDS, jnp.cos(ang),
                            jnp.where(lane <= 2 * HY_BANDS, -jnp.sin(ang), 0.0)))
    fr = fr_ref[...]
    h = jnp.sin(fr * (_dot(z, w1_ref[...], HI) + b1_ref[...]))
    h = jnp.sin(fr * (_dot(h, w2_ref[...], HI) + b2_ref[...]))
    h = jnp.sin(fr * (_dot(h, w3_ref[...], HI) + b3_ref[...]))
    h = _dot(h, w4_ref[...], HI)
    ch = lax.broadcasted_iota(jnp.int32, (1, width), 1).astype(F32)
    min_decay = math.log(HY_TARGET) / HY_SLOW_DECAY
    max_decay = math.log(HY_TARGET) / HY_FAST_DECAY
    deltas = min_decay + ch * ((max_decay - min_decay) / (width - 1))
    window = jnp.exp(-t * jnp.abs(deltas))
    valid = jnp.logical_or(idx < l, idx > n - l)
    o_ref[...] = jnp.where(valid, h * window, 0.0)


def _hy_filter(filt, l, n):
    w1p, b1, w2, b2, w3, b3, w4, fr = filt
    width = w4.shape[1] // 2
    tp = min(512, l)
    full = lambda a: pl.BlockSpec(a.shape, lambda i: (0, 0))
    return pl.pallas_call(
        functools.partial(_hy_filter_kernel, tp=tp, l=l, n=n, width=width),
        out_shape=jax.ShapeDtypeStruct((n, width), F32),
        grid=(n // tp,),
        in_specs=[full(w1p), full(b1), full(w2), full(b2), full(w3), full(b3),
                  pl.BlockSpec((w4.shape[0], width), lambda i: (0, (i * tp >= l).astype(jnp.int32))),
                  full(fr)],
        out_specs=pl.BlockSpec((tp, width), lambda i: (i, 0)),
        compiler_params=_cp("parallel"),
    )(w1p, b1, w2, b2, w3, b3, w4, fr)


def _hy_ctx_kernel(x0_ref, x1_ref, v_ref, kern_ref, skip_ref, o_ref, ks_ref, u_ref, *, l):
    u = x1_ref[...] * v_ref[...]
    u_ref[...] = u
    base = jnp.concatenate([kern_ref[l:2 * l, :], kern_ref[0:l, :]], axis=0)
    ks_ref[0] = base
    for b in range(1, 8):
        ks_ref[b] = pltpu.roll(base, b, 0)

    def body(a, acc):
        start = pl.multiple_of(l - 8 * a, 8)
        ublk = u_ref[pl.ds(pl.multiple_of(8 * a, 8), 8), :]
        for b in range(8):
            acc = acc + ks_ref[b, pl.ds(start, l), :] * ublk[b:b + 1, :]
        return acc

    y = lax.fori_loop(0, l // 8, body, jnp.zeros((l, 128), F32))
    o_ref[...] = (x0_ref[...] * (y + u * skip_ref[...])).astype(o_ref.dtype)


def _hy_ctx(p, kern, skip, *, ctx):
    width = skip.shape[1]
    nb = width // 128
    return pl.pallas_call(
        functools.partial(_hy_ctx_kernel, l=ctx),
        out_shape=jax.ShapeDtypeStruct((ctx, width), BF16),
        grid=(nb,),
        in_specs=[pl.BlockSpec((ctx, 128), lambda c: (0, c)),
                  pl.BlockSpec((ctx, 128), lambda c: (0, nb + c)),
                  pl.BlockSpec((ctx, 128), lambda c: (0, 2 * nb + c)),
                  pl.BlockSpec((2 * ctx, 128), lambda c: (0, c)),
                  pl.BlockSpec((1, 128), lambda c: (0, c))],
        out_specs=pl.BlockSpec((ctx, 128), lambda c: (0, c)),
        scratch_shapes=[pltpu.VMEM((8, 2 * ctx, 128), F32), pltpu.VMEM((ctx, 128), F32)],
        compiler_params=_cp("parallel"),
    )(p, p, p, kern, skip)


def _hy_prep_kernel(x0_ref, x1_ref, v_ref, u_ref, x0o_ref):
    u_ref[...] = x1_ref[...] * v_ref[...]
    x0o_ref[...] = x0_ref[...]


def _hy_prep(p, *, ctx, width):
    s_lat = p.shape[0] - ctx
    tm = 256
    off = ctx // tm
    return pl.pallas_call(
        _hy_prep_kernel,
        out_shape=(jax.ShapeDtypeStruct((s_lat, width), F32), jax.ShapeDtypeStruct((s_lat, width), F32)),
        grid=(s_lat // tm,),
        in_specs=[pl.BlockSpec((tm, width), lambda i: (i + off, 0)),
                  pl.BlockSpec((tm, width), lambda i: (i + off, 1)),
                  pl.BlockSpec((tm, width), lambda i: (i + off, 2))],
        out_specs=(pl.BlockSpec((tm, width), lambda i: (i, 0)), pl.BlockSpec((tm, width), lambda i: (i, 0))),
        compiler_params=_cp("parallel"),
    )(p, p, p)


def _dft_consts(n1):
    n2 = FFT_N2
    n = n1 * n2
    f1n = n1 // 2 + 1
    f1p = -(-f1n // 8) * 8
    f1 = np.arange(f1p)[:, None].astype(np.float64)
    live = (f1 < f1n).astype(np.float64)
    t1 = np.arange(n1)[None, :]
    ang = 2 * np.pi * f1 * t1 / n1
    w_re, w_im = np.cos(ang) * live, -np.sin(ang) * live
    t0 = np.arange(n2)
    ang = 2 * np.pi * f1[None, :, :] * t0[:, None, None] / n
    tw_fwd = np.concatenate([np.cos(ang), -np.sin(ang)], axis=-1)
    ang = 2 * np.pi * f1[:, :, None] * t0[None, :, None] / n
    tw_inv = np.concatenate([np.cos(ang), np.sin(ang)], axis=-1)
    ang = 2 * np.pi * np.outer(t0, t0) / n2
    d_re, d_im = np.cos(ang), -np.sin(ang)
    wgt = np.where((f1 == 0) | (f1 == n1 // 2), 1.0, 2.0) * live
    t1o = np.arange(n1 // 2)[:, None]
    ang = 2 * np.pi * t1o * f1.T / n1
    c_m, s_m = np.cos(ang) * wgt.T / n, -np.sin(ang) * wgt.T / n
    f = lambda a: jnp.asarray(a, F32)
    return dict(f1p=f1p, w_re=f(w_re), w_im=f(w_im), tw_fwd=f(tw_fwd), tw_inv=f(tw_inv),
                d_re=f(d_re), d_im=f(d_im), c_m=f(c_m), s_m=f(s_m))


def _fft1_kernel(u_ref, wre_ref, wim_ref, tw_ref, bre_ref, bim_ref, *, tb, c):
    wre = wre_ref[...]
    wim = wim_ref[...]
    for b in range(tb):
        sl = slice(b * c, (b + 1) * c)
        u = u_ref[:, sl]
        are = _dot(wre, u, HI)
        aim = _dot(wim, u, HI)
        tr = tw_ref[b, :, 0:1]
        ti = tw_ref[b, :, 1:2]
        bre_ref[:, sl] = are * tr - aim * ti
        bim_ref[:, sl] = are * ti + aim * tr


def _fft1(u2, w_re, w_im, tw_fwd, *, c):
    t1_in = u2.shape[0]
    f1p = w_re.shape[0]
    tb = 4
    out = jax.ShapeDtypeStruct((f1p, FFT_N2 * c), F32)
    return pl.pallas_call(
        functools.partial(_fft1_kernel, tb=tb, c=c),
        out_shape=(out, out),
        grid=(FFT_N2 // tb,),
        in_specs=[pl.BlockSpec((t1_in, tb * c), lambda i: (0, i)),
                  pl.BlockSpec((f1p, t1_in), lambda i: (0, 0)),
                  pl.BlockSpec((f1p, t1_in), lambda i: (0, 0)),
                  pl.BlockSpec((tb, f1p, 2), lambda i: (i, 0, 0))],
        out_specs=(pl.BlockSpec((f1p, tb * c), lambda i: (0, i)), pl.BlockSpec((f1p, tb * c), lambda i: (0, i))),
        compiler_params=_cp("parallel"),
    )(u2, w_re[:, :t1_in], w_im[:, :t1_in], tw_fwd)


def _fft_spec_kernel(bre_ref, bim_ref, dre_ref, dim_ref, xre_ref, xim_ref):
    br, bi = bre_ref[0], bim_ref[0]
    dr, di = dre_ref[...], dim_ref[...]
    xre_ref[0] = _dot(dr, br, HI) - _dot(di, bi, HI)
    xim_ref[0] = _dot(dr, bi, HI) + _dot(di, br, HI)


def _fft_spectrum(b_re, b_im, d_re, d_im):
    f1p, n2, c = b_re.shape
    blk = pl.BlockSpec((1, n2, c), lambda i: (i, 0, 0))
    mat = pl.BlockSpec((n2, n2), lambda i: (0, 0))
    out = jax.ShapeDtypeStruct((f1p, n2, c), F32)
    return pl.pallas_call(
        _fft_spec_kernel, out_shape=(out, out), grid=(f1p,),
        in_specs=[blk, blk, mat, mat], out_specs=(blk, blk),
        compiler_params=_cp("parallel"),
    )(b_re, b_im, d_re, d_im)


def _fft_mid_kernel(bre_ref, bim_ref, kre_ref, kim_ref, dre_ref, dim_ref, tw_ref, ore_ref, oim_ref):
    br, bi = bre_ref[0], bim_ref[0]
    dr, di = dre_ref[...], dim_ref[...]
    xr = _dot(dr, br, HI) - _dot(di, bi, HI)
    xi = _dot(dr, bi, HI) + _dot(di, br, HI)
    kr, ki = kre_ref[0], kim_ref[0]
    yr = xr * kr - xi * ki
    yi = xr * ki + xi * kr
    zr = _dot(dr, yr, HI) + _dot(di, yi, HI)
    zi = _dot(dr, yi, HI) - _dot(di, yr, HI)
    tr = tw_ref[0, :, 0:1]
    ti = tw_ref[0, :, 1:2]
    ore_ref[0] = zr * tr - zi * ti
    oim_ref[0] = zr * ti + zi * tr


def _fft_mid(b_re, b_im, k_re, k_im, d_re, d_im, tw_inv):
    f1p, n2, c = b_re.shape
    blk = pl.BlockSpec((1, n2, c), lambda i: (i, 0, 0))
    mat = pl.BlockSpec((n2, n2), lambda i: (0, 0))
    out = jax.ShapeDtypeStruct((f1p, n2, c), F32)
    return pl.pallas_call(
        _fft_mid_kernel, out_shape=(out, out), grid=(f1p,),
        in_specs=[blk, blk, blk, blk, mat, mat, pl.BlockSpec((1, n2, 2), lambda i: (i, 0, 0))],
        out_specs=(blk, blk),
        compiler_params=_cp("parallel"),
    )(b_re, b_im, k_re, k_im, d_re, d_im, tw_inv)


def _fft_fin_kernel(bre_ref, bim_ref, cm_ref, sm_ref, x0_ref, u_ref, skip_ref, o_ref, *, tb, c):
    y = _dot(cm_ref[...], bre_ref[...], HI) + _dot(sm_ref[...], bim_ref[...], HI)
    skip = skip_ref[...]
    for b in range(tb):
        sl = slice(b * c, (b + 1) * c)
        o_ref[:, sl] = (x0_ref[:, sl] * (y[:, sl] + u_ref[:, sl] * skip)).astype(o_ref.dtype)


def _fft_fin(z_re, z_im, c_m, s_m, x0_2, u2, skip, *, c):
    f1p = z_re.shape[0]
    t1o = c_m.shape[0]
    tb = 4
    big = pl.BlockSpec((f1p, tb * c), lambda i: (0, i))
    sig = pl.BlockSpec((t1o, tb * c), lambda i: (0, i))
    mat = pl.BlockSpec((t1o, f1p), lambda i: (0, 0))
    return pl.pallas_call(
        functools.partial(_fft_fin_kernel, tb=tb, c=c),
        out_shape=jax.ShapeDtypeStruct((t1o, FFT_N2 * c), BF16),
        grid=(FFT_N2 // tb,),
        in_specs=[big, big, mat, mat, sig, sig, pl.BlockSpec((1, c), lambda i: (0, 0))],
        out_specs=sig,
        compiler_params=_cp("parallel"),
    )(z_re, z_im, c_m, s_m, x0_2, u2, skip)


def _hyena_latent(p, kern, skip, *, ctx):
    c = skip.shape[1]
    s_lat = p.shape[0] - ctx
    n1 = 2 * s_lat // FFT_N2
    k = _dft_consts(n1)
    f1p = k["f1p"]
    u, x0 = _hy_prep(p, ctx=ctx, width=c)
    u2 = u.reshape(n1 // 2, FFT_N2 * c)
    x0_2 = x0.reshape(n1 // 2, FFT_N2 * c)
    kb_re, kb_im = _fft1(kern.reshape(n1, FFT_N2 * c), k["w_re"], k["w_im"], k["tw_fwd"], c=c)
    kf_re, kf_im = _fft_spectrum(kb_re.reshape(f1p, FFT_N2, c), kb_im.reshape(f1p, FFT_N2, c), k["d_re"], k["d_im"])
    b_re, b_im = _fft1(u2, k["w_re"], k["w_im"], k["tw_fwd"], c=c)
    z_re, z_im = _fft_mid(b_re.reshape(f1p, FFT_N2, c), b_im.reshape(f1p, FFT_N2, c), kf_re, kf_im,
                          k["d_re"], k["d_im"], k["tw_inv"])
    y2 = _fft_fin(z_re.reshape(f1p, FFT_N2 * c), z_im.reshape(f1p, FFT_N2 * c), k["c_m"], k["s_m"],
                  x0_2, u2, skip, c=c)
    return y2.reshape(s_lat, c)


def _softplus(x):
    return jnp.maximum(x, 0.0) + jnp.log(1.0 + jnp.exp(-jnp.abs(x)))


def _dn_intra_kernel(q_ref, k_ref, v_ref, sm_ref, cst_ref, u_ref, w_ref, qd_ref, kd_ref, qk_ref, last_ref):
    c = DN_CHUNK
    nh = q_ref.shape[1] // DN_HEAD
    row = lax.broadcasted_iota(jnp.int32, (c, c), 0)
    col = lax.broadcasted_iota(jnp.int32, (c, c), 1)
    eye = row == col
    ones = jnp.ones((c, c), F32)
    sm = sm_ref[...]
    beta_all = jax.nn.sigmoid(sm)
    g_all = -jnp.exp(cst_ref[0:1, :]) * _softplus(sm + cst_ref[1:2, :])
    tot_all = jnp.sum(g_all, axis=0, keepdims=True)
    incl = (row >= col, row <= col)
    strict = (row > col, row < col)
    gc_all = (_dot(incl[0].astype(F32), g_all, HI), _dot(incl[1].astype(F32), g_all, HI))

    for h in range(nh):
        hs = slice(h * DN_HEAD, (h + 1) * DN_HEAD)
        qh = _silu(q_ref[:, hs])
        kh = _silu(k_ref[:, hs])
        vh = _silu(v_ref[:, hs])
        qh = qh * lax.rsqrt(jnp.sum(qh * qh, axis=-1, keepdims=True) + EPS) * (DN_HEAD ** -0.5)
        kh = kh * lax.rsqrt(jnp.sum(kh * kh, axis=-1, keepdims=True) + EPS)
        qkt = lax.dot_general(qh, kh, NT, precision=HI, preferred_element_type=F32)
        for d in range(2):
            bi = d * nh + h
            gi = 2 * nh + bi
            beta = beta_all[:, bi:bi + 1]
            gcol = gc_all[d][:, gi:gi + 1]
            gtot = tot_all[:, gi:gi + 1]
            grow = _dot(ones, jnp.where(eye, gcol, 0.0), HI)
            gamma = jnp.where(incl[d], jnp.exp(jnp.where(incl[d], gcol - grow, 0.0)), 0.0)
            egc = jnp.exp(gcol)
            kb = kh * beta
            kkt = lax.dot_general(kb, kh, NT, precision=HI, preferred_element_type=F32)
            pw = -jnp.where(strict[d], kkt * gamma, 0.0)
            sol = jnp.concatenate([vh * beta, kb * egc], axis=1)
            steps = int(math.log2(c))
            for t in range(steps):
                sol = sol + _dot(pw, sol, HI)
                if t + 1 < steps:
                    pw = _dot(pw, pw, HI)
            u_ref[d, :, hs] = sol[:, :DN_HEAD]
            w_ref[d, :, hs] = sol[:, DN_HEAD:]
            qd_ref[d, :, hs] = qh * egc
            kd_ref[d, :, hs] = kh * jnp.exp(gtot - gcol)
            qk_ref[d, :, h * c:(h + 1) * c] = qkt * gamma
            last_ref[d, 0, :, hs] = jnp.broadcast_to(jnp.exp(gtot), (8, DN_HEAD))


def _dn_intra(p, cst):
    s_tot = p.shape[0]
    c = DN_CHUNK
    w = 512
    nch = s_tot // c
    big = jax.ShapeDtypeStruct((2, s_tot, w), F32)
    bspec = pl.BlockSpec((2, c, w), lambda j: (0, j, 0))
    return pl.pallas_call(
        _dn_intra_kernel,
        out_shape=(big, big, big, big, jax.ShapeDtypeStruct((2, s_tot, w // 2), F32),
                   jax.ShapeDtypeStruct((2, nch, 8, w), F32)),
        grid=(nch,),
        in_specs=[pl.BlockSpec((c, w), lambda j: (j, 3)),
                  pl.BlockSpec((c, w), lambda j: (j, 4)),
                  pl.BlockSpec((c, w), lambda j: (j, 5)),
                  pl.BlockSpec((c, 128), lambda j: (j, 46)),
                  pl.BlockSpec((8, 128), lambda j: (0, 0))],
        out_specs=(bspec, bspec, bspec, bspec, pl.BlockSpec((2, c, w // 2), lambda j: (0, j, 0)),
                   pl.BlockSpec((2, 1, 8, w), lambda j: (0, j, 0, 0))),
        compiler_params=_cp("parallel"),
    )(p, p, p, p, cst)


def _dn_rec_kernel(*refs, nh):
    ins, (of_ref, ob_ref, s_ref) = refs[:12], refs[12:]
    o_refs = (of_ref, ob_ref)

    @pl.when(pl.program_id(0) == 0)
    def _():
        s_ref[...] = jnp.zeros_like(s_ref)

    c = DN_CHUNK
    for d in range(2):
        u_ref, w_ref, qd_ref, kd_ref, qk_ref, last_ref = ins[6 * d:6 * d + 6]
        for h in range(nh):
            hs = slice(h * DN_HEAD, (h + 1) * DN_HEAD)
            s = s_ref[d * nh + h]
            v_new = u_ref[0, :, hs] - _dot(w_ref[0, :, hs], s, HI)
            o = _dot(qd_ref[0, :, hs], s, HI) + _dot(qk_ref[0, :, h * c:(h + 1) * c], v_new, HI)
            s_ref[d * nh + h] = s * last_ref[0, 0, 0:1, hs] + lax.dot_general(
                kd_ref[0, :, hs], v_new, TN, precision=HI, preferred_element_type=F32)
            o_refs[d][:, hs] = o


def _dn_recurrence(u, w, qd, kd, qk, last, *, ctx):
    _, s_tot, width = u.shape
    c = DN_CHUNK
    nch = s_tot // c
    nctx = ctx // c
    nh = width // DN_HEAD
    fwd = lambda j: j
    bwd = lambda j: jnp.where(j < nctx, nctx - 1 - j, nch - 1 - (j - nctx))
    specs = []
    for d, cm in ((0, fwd), (1, bwd)):
        blk = lambda ww, cm=cm, d=d: pl.BlockSpec((1, c, ww), lambda j: (d, cm(j), 0))
        specs += [blk(width), blk(width), blk(width), blk(width), blk(width // 2),
                  pl.BlockSpec((1, 1, 8, width), lambda j, cm=cm, d=d: (d, cm(j), 0, 0))]
    out = jax.ShapeDtypeStruct((s_tot, width), F32)
    return pl.pallas_call(
        functools.partial(_dn_rec_kernel, nh=nh),
        out_shape=(out, out),
        grid=(nch,),
        in_specs=specs,
        out_specs=(pl.BlockSpec((c, width), lambda j: (fwd(j), 0)), pl.BlockSpec((c, width), lambda j: (bwd(j), 0))),
        scratch_shapes=[pltpu.VMEM((2 * nh, DN_HEAD, DN_HEAD), F32)],
        compiler_params=_cp("arbitrary"),
    )(u, w, qd, kd, qk, last, u, w, qd, kd, qk, last)


def _dn_finish_kernel(of_ref, ob_ref, gate_ref, gain_ref, o_ref):
    nh = of_ref.shape[1] // DN_HEAD
    for h in range(nh):
        hs = slice(h * DN_HEAD, (h + 1) * DN_HEAD)
        o = of_ref[:, hs] + ob_ref[:, hs]
        y = o * lax.rsqrt(jnp.mean(o * o, axis=-1, keepdims=True) + EPS) * gain_ref[...]
        o_ref[:, hs] = (y * _silu(gate_ref[:, hs])).astype(o_ref.dtype)


def _dn_finish(o_f, o_b, p, gain):
    s_tot, width = o_f.shape
    tm = _row_tile(s_tot, (528, 640, 256, 128))
    blk = pl.BlockSpec((tm, width), lambda i: (i, 0))
    return pl.pallas_call(
        _dn_finish_kernel,
        out_shape=jax.ShapeDtypeStruct((s_tot, width), BF16),
        grid=(s_tot // tm,),
        in_specs=[blk, blk, pl.BlockSpec((tm, width), lambda i: (i, 6)), pl.BlockSpec((1, DN_HEAD), lambda i: (0, 0))],
        out_specs=blk,
        compiler_params=_cp("parallel"),
    )(o_f, o_b, p, gain)


def _rope_tables(ctx, s_lat):
    rows = s_lat // GRID_W
    n_freq = HEAD_D // 4
    row = np.repeat(np.arange(rows, dtype=np.float32), GRID_W)
    col = np.tile(np.arange(GRID_W, dtype=np.float32), rows)
    inv = (ROPE_THETA ** (-np.arange(n_freq, dtype=np.float32) / n_freq)).astype(np.float32)
    ang = np.concatenate([row[:, None] * inv, col[:, None] * inv], axis=-1).astype(np.float32)
    cos, sin = np.cos(ang), np.sin(ang)
    cos = np.concatenate([np.ones((ctx, HEAD_D // 2), np.float32), cos], axis=0)
    sin = np.concatenate([np.zeros((ctx, HEAD_D // 2), np.float32), sin], axis=0)
    cos_t = np.tile(np.concatenate([cos, cos], axis=1), (1, 2))
    sin_t = np.tile(np.concatenate([-sin, sin], axis=1), (1, 2))
    return jnp.asarray(cos_t, F32), jnp.asarray(sin_t, F32)


def _pad_rows(a, rows):
    return jnp.concatenate([a, jnp.zeros((rows - a.shape[0],) + a.shape[1:], a.dtype)], axis=0)


def kernel(x, c, ctx, c_ctx, w_ada, b_ada, norm_mix_pre, norm_mix_post, norm_ffn_pre, norm_ffn_post, w_in, w_out, attn_q_norm, attn_k_norm, hy_short, hy_w1, hy_b1, hy_w2, hy_b2, hy_w3, hy_b3, hy_w4, hy_freq, hy_skip, dn_short, dn_a_log, dn_dt_bias, dn_norm, df_lambda, df_norm, ffn_up, ffn_conv, ffn_down):
    batch, s_lat, d = x.shape
    n_ctx = ctx.shape[1]
    depth = w_in.shape[0]
    gw = d // 4
    assert batch == 1 and gw == 512 and n_ctx % 256 == 0 and s_lat % 512 == 0
    s_tot = n_ctx + s_lat

    xx = jnp.concatenate([ctx[0], x[0]], axis=0)
    c_rows = _pad_rows(jnp.concatenate([c, c_ctx[None, :]], axis=0), 8)
    mods = _mod_vectors(c_rows, w_ada, b_ada)

    cos_t, sin_t = _rope_tables(n_ctx, s_lat)
    lane = np.arange(128)
    gmat = jnp.asarray((lane[:, None] // HEAD_D == lane[None, :] // HEAD_D) / HEAD_D, F32)
    ident = jnp.asarray([[0.0], [1.0], [0.0]], F32)

    a_cols = gw + 2 * 2 * HEAD_D
    o_hy, o_dn, o_df = a_cols, a_cols + 3 * gw, a_cols + 3 * gw + 4 * gw + 16
    order = [(o_hy, o_hy + 3 * gw), (o_dn, o_dn + 4 * gw), (o_df, o_df + 3 * gw), (0, a_cols),
             (o_dn + 4 * gw, o_dn + 4 * gw + 16)]
    n_proj = 12 * gw
    used = sum(b - a for a, b in order)

    for i in range(depth):
        lam_init = 0.8 - 0.6 * math.exp(-0.3 * i)
        mod = mods[i]
        sh1, sc1, gt1, sh2, sc2, gt2 = (mod[:, k * d:(k + 1) * d] for k in range(6))
        mod_mix = _pad_rows(jnp.concatenate([sh1[0:1], sc1[0:1], sh1[1:2], sc1[1:2]], axis=0), 8)
        mod_ffn = _pad_rows(jnp.concatenate([sh2[0:1], sc2[0:1], sh2[1:2], sc2[1:2]], axis=0), 8)

        w_in_p = jnp.concatenate([w_in[i][:, a:b] for a, b in order]
                                 + [jnp.zeros((d, n_proj - used), F32)], axis=1).astype(BF16)
        cw_in = jnp.concatenate([hy_short[i], dn_short[i], jnp.broadcast_to(ident, (3, n_proj - 6 * gw))], axis=1)
        p = _norm_mod_matmul_conv(xx, norm_mix_pre[i][None, :], mod_mix, w_in_p, _pad_rows(cw_in, 8),
                                  ctx=n_ctx, ffn=False)

        qg, kg, vg, qd, kd, vd = _attn_prep(p, jnp.tile(attn_q_norm[i], 2)[None, :],
                                            jnp.tile(attn_k_norm[i], 2)[None, :], cos_t, sin_t, gmat)
        ya = _gqa_attention(qg, kg, vg, ctx=n_ctx)
        yd = _diff_attention(qd, kd, vd, df_lambda[i], df_norm[i][None, :], ctx=n_ctx, lam_init=lam_init)

        filt = (_pad_rows(hy_w1[i], 128), hy_b1[i][None, :], hy_w2[i], hy_b2[i][None, :], hy_w3[i],
                hy_b3[i][None, :], hy_w4[i], hy_freq[i][None, :])
        skip = hy_skip[i][None, :]
        yb_lat = _hyena_latent(p, _hy_filter(filt, s_lat, 2 * s_lat), skip, ctx=n_ctx)
        if i < depth - 1:
            yb_ctx = _hy_ctx(p, _hy_filter(filt, n_ctx, 2 * n_ctx), skip, ctx=n_ctx)
        else:
            yb_ctx = jnp.zeros((n_ctx, gw), BF16)
        yb = jnp.concatenate([yb_ctx, yb_lat], axis=0)

        zeros8 = jnp.zeros((8,), F32)
        cst = _pad_rows(jnp.stack([jnp.concatenate([zeros8, dn_a_log[i].reshape(-1), jnp.zeros((112,), F32)]),
                                   jnp.concatenate([zeros8, dn_dt_bias[i].reshape(-1), jnp.zeros((112,), F32)])]), 8)
        o_f, o_b = _dn_recurrence(*_dn_intra(p, cst), ctx=n_ctx)
        yc = _dn_finish(o_f, o_b, p, dn_norm[i][None, :])

        y_cat = jnp.concatenate([ya, yb, yc, yd], axis=1)
        xx = _matmul_norm_residual(y_cat, w_out[i].astype(BF16), xx, gt1, norm_mix_post[i][None, :], ctx=n_ctx)

        g = _norm_mod_matmul_conv(xx, norm_ffn_pre[i][None, :], mod_ffn, ffn_up[i].astype(BF16),
                                  _pad_rows(ffn_conv[i], 8), ctx=n_ctx, ffn=True)
        xx = _matmul_norm_residual(g, ffn_down[i].astype(BF16), xx, gt2, norm_ffn_post[i][None, :], ctx=n_ctx)

    return xx[n_ctx:][None]
```

```python
import functools
import math

import jax
import jax.numpy as jnp
import numpy as np
from jax import lax
from jax.experimental import pallas as pl
from jax.experimental.pallas import tpu as pltpu

F32 = jnp.float32
BF16 = jnp.bfloat16
HI = lax.Precision.HIGHEST
EPS = 1e-6

GRID_W = 64
ROPE_THETA = 10000.0
HEAD_D = 64
GQA_GROUP = 4
DN_HEAD = 128
DN_CHUNK = 64
HY_BANDS = 16
HY_FAST_DECAY, HY_SLOW_DECAY, HY_TARGET = 0.3, 1.5, 1e-2
FFT_N2 = 128
ATT_TK = 256
ATT_LB = 256
LOG2E = 1.4426950408889634
VMEM_LIMIT = 56 * 1024 * 1024
NT = (((1,), (1,)), ((), ()))
TN = (((0,), (0,)), ((), ()))


def _cp(*sem):
    return pltpu.CompilerParams(dimension_semantics=sem, vmem_limit_bytes=VMEM_LIMIT)


def _dot(a, b, precision=None):
    return jnp.dot(a, b, precision=precision, preferred_element_type=F32)


def _silu(x):
    return x * jax.nn.sigmoid(x)


def _row_tile(n, cands):
    for t in cands:
        if n % t == 0:
            return t
    raise ValueError(f"no row tile for {n}")


def _mod_kernel(c_ref, w_ref, b_ref, o_ref):
    o_ref[0] = _dot(_silu(c_ref[...]), w_ref[0], HI) + b_ref[0]


def _mod_vectors(c_rows, w_ada, b_ada):
    depth, d, n = w_ada.shape
    tn = 512
    return pl.pallas_call(
        _mod_kernel,
        name="mod_vectors",
        out_shape=jax.ShapeDtypeStruct((depth, 8, n), F32),
        grid=(depth, n // tn),
        in_specs=[pl.BlockSpec((8, d), lambda l, j: (0, 0)),
                  pl.BlockSpec((1, d, tn), lambda l, j: (l, 0, j)),
                  pl.BlockSpec((1, 1, tn), lambda l, j: (l, 0, j))],
        out_specs=pl.BlockSpec((1, 8, tn), lambda l, j: (l, 0, j)),
        compiler_params=_cp("parallel", "parallel"),
    )(c_rows, w_ada, b_ada.reshape(depth, 1, n))


HALO = 16


def _k1_kernel(xp_ref, xm_ref, xn_ref, g_ref, mod_ref, *rest, tm, ctx, s_tot, ffn):
    if ffn:
        wa_ref, wb_ref, cwa_ref, cwb_ref, o_ref, h_ref = rest
    else:
        wa_ref, cwa_ref, o_ref, h_ref = rest
    i = pl.program_id(0)
    j = pl.program_id(1)

    @pl.when(j == 0)
    def _():
        def nm(xv, row0):
            ms = jnp.mean(xv * xv, axis=-1, keepdims=True)
            y = xv * lax.rsqrt(ms + EPS) * g_ref[...]
            rows = row0 + lax.broadcasted_iota(jnp.int32, (xv.shape[0], 1), 0)
            is_ctx = rows < ctx
            sh = jnp.where(is_ctx, mod_ref[2:3, :], mod_ref[0:1, :])
            sc = jnp.where(is_ctx, mod_ref[3:4, :], mod_ref[1:2, :])
            return (y * (1.0 + sc) + sh).astype(BF16)

        h_ref[0:HALO] = nm(xp_ref[...], i * tm - HALO)
        h_ref[HALO:HALO + tm] = nm(xm_ref[...], i * tm)
        h_ref[HALO + tm:] = nm(xn_ref[...], i * tm + tm)

    h = h_ref[...]
    rows = i * tm + lax.broadcasted_iota(jnp.int32, (tm, 1), 0)
    mprev = jnp.logical_and(rows != 0, rows != ctx)
    mnext = jnp.logical_and(rows != ctx - 1, rows != s_tot - 1)

    def conv(w_ref, cw_ref):
        acc = _dot(h, w_ref[...])
        prev = pltpu.roll(acc, 1, 0)[HALO:HALO + tm]
        nxt = pltpu.roll(acc, tm + 2 * HALO - 1, 0)[HALO:HALO + tm]
        cur = acc[HALO:HALO + tm]
        cw = cw_ref[...]
        return (cur * cw[1:2] + jnp.where(mprev, prev, 0.0) * cw[0:1]
                + jnp.where(mnext, nxt, 0.0) * cw[2:3])

    if ffn:
        a = conv(wa_ref, cwa_ref)
        b = conv(wb_ref, cwb_ref)
        o_ref[...] = (_silu(a) * b).astype(o_ref.dtype)
    else:
        o_ref[...] = conv(wa_ref, cwa_ref).astype(o_ref.dtype)


def _norm_mod_matmul_conv(xx, gain, mod4, w, cw, *, ctx, ffn):
    s_tot, d = xx.shape
    n = w.shape[1]
    tm = _row_tile(s_tot, (528, 640, 256, 128))
    tn = 512
    nb = s_tot // HALO
    n_out = n // 2 if ffn else n
    x_specs = [
        pl.BlockSpec((HALO, d), lambda i, j: (jnp.maximum(i * (tm // HALO) - 1, 0), 0)),
        pl.BlockSpec((tm, d), lambda i, j: (i, 0)),
        pl.BlockSpec((HALO, d), lambda i, j: (jnp.minimum((i + 1) * (tm // HALO), nb - 1), 0)),
        pl.BlockSpec((1, d), lambda i, j: (0, 0)),
        pl.BlockSpec((8, d), lambda i, j: (0, 0)),
    ]
    if ffn:
        off = n_out // tn
        w_specs = [pl.BlockSpec((d, tn), lambda i, j: (0, j)),
                   pl.BlockSpec((d, tn), lambda i, j: (0, j + off)),
                   pl.BlockSpec((8, tn), lambda i, j: (0, j)),
                   pl.BlockSpec((8, tn), lambda i, j: (0, j + off))]
        args = (w, w, cw, cw)
        out_dtype = BF16
    else:
        w_specs = [pl.BlockSpec((d, tn), lambda i, j: (0, j)),
                   pl.BlockSpec((8, tn), lambda i, j: (0, j))]
        args = (w, cw)
        out_dtype = F32
    return pl.pallas_call(
        functools.partial(_k1_kernel, tm=tm, ctx=ctx, s_tot=s_tot, ffn=ffn),
        name="ffn_up_conv_gate" if ffn else "in_proj_conv",
        out_shape=jax.ShapeDtypeStruct((s_tot, n_out), out_dtype),
        grid=(s_tot // tm, n_out // tn),
        in_specs=x_specs + w_specs,
        out_specs=pl.BlockSpec((tm, tn), lambda i, j: (i, j)),
        scratch_shapes=[pltpu.VMEM((tm + 2 * HALO, d), BF16)],
        compiler_params=_cp("parallel", "arbitrary"),
    )(xx, xx, xx, gain, mod4, *args)


def _k2_kernel(a_ref, w_ref, x_ref, gate_ref, gain_ref, o_ref, acc_ref, *, tm, ctx, nk):
    i = pl.program_id(0)
    k = pl.program_id(1)

    @pl.when(k == 0)
    def _():
        acc_ref[...] = jnp.zeros_like(acc_ref)

    acc_ref[...] += _dot(a_ref[...], w_ref[...])

    @pl.when(k == nk - 1)
    def _():
        y = acc_ref[...]
        r = y * lax.rsqrt(jnp.mean(y * y, axis=-1, keepdims=True) + EPS) * gain_ref[...]
        rows = i * tm + lax.broadcasted_iota(jnp.int32, (tm, 1), 0)
        gt = jnp.where(rows < ctx, gate_ref[1:2, :], gate_ref[0:1, :])
        o_ref[...] = x_ref[...] + gt * r


def _matmul_norm_residual(a, w, xx, gate2, gain, *, ctx):
    s_tot, kdim = a.shape
    d = w.shape[1]
    tm = _row_tile(s_tot, (528, 640, 256, 128))
    tk = 512
    nk = kdim // tk
    return pl.pallas_call(
        functools.partial(_k2_kernel, tm=tm, ctx=ctx, nk=nk),
        name="matmul_norm_residual",
        out_shape=jax.ShapeDtypeStruct((s_tot, d), F32),
        grid=(s_tot // tm, nk),
        in_specs=[pl.BlockSpec((tm, tk), lambda i, k: (i, k)),
                  pl.BlockSpec((tk, d), lambda i, k: (k, 0)),
                  pl.BlockSpec((tm, d), lambda i, k: (i, 0)),
                  pl.BlockSpec((8, d), lambda i, k: (0, 0)),
                  pl.BlockSpec((1, d), lambda i, k: (0, 0))],
        out_specs=pl.BlockSpec((tm, d), lambda i, k: (i, 0)),
        scratch_shapes=[pltpu.VMEM((tm, d), F32)],
        compiler_params=_cp("parallel", "arbitrary"),
    )(a, w, xx, gate2, gain)


def _attn_prep_kernel(aq_ref, akv_ref, dq_ref, dk_ref, dv_ref, qgain_ref, kgain_ref, cos_ref, sin_ref,
                      gmat_ref, qg_ref, kg_ref, vgt_ref, qd_ref, kd_ref, vdt_ref):
    cosf = cos_ref[...]
    sins = sin_ref[...]
    lane = lax.broadcasted_iota(jnp.int32, (1, 128), 1)
    first_half = (lane & (HEAD_D - 1)) < (HEAD_D // 2)
    scale = HEAD_D ** -0.5 * LOG2E

    def rope(x):
        partner = jnp.where(first_half, pltpu.roll(x, 128 - HEAD_D // 2, 1), pltpu.roll(x, HEAD_D // 2, 1))
        return x * cosf + partner * sins

    def headnorm(x, gain):
        ms = _dot(x * x, gmat_ref[...], HI)
        return x * lax.rsqrt(ms + EPS) * gain

    def put(ref, pair, val):
        vb = val.astype(ref.dtype)
        ref[2 * pair] = vb[:, :HEAD_D]
        ref[2 * pair + 1] = vb[:, HEAD_D:]

    for ci in range(4):
        sl = slice(ci * 128, (ci + 1) * 128)
        put(qg_ref, ci, rope(headnorm(aq_ref[:, sl], qgain_ref[...])) * scale)
        put(qd_ref, ci, rope(dq_ref[:, sl]) * scale)
        put(kd_ref, ci, rope(dk_ref[:, sl]))
        vdt_ref[ci, 0] = dv_ref[:, sl].T.astype(vdt_ref.dtype)
    put(kg_ref, 0, rope(headnorm(akv_ref[:, 0:128], kgain_ref[...])))
    vt = akv_ref[:, 128:256].T.astype(vgt_ref.dtype)
    vgt_ref[0, 0] = vt[:HEAD_D]
    vgt_ref[1, 0] = vt[HEAD_D:]


def _attn_prep(p, qgain, kgain, cos_t, sin_t, gmat):
    s_tot = p.shape[0]
    tm = ATT_TK
    hm = lambda nh, dd: jax.ShapeDtypeStruct((nh, s_tot, dd), BF16)
    hspec = lambda nh, dd: pl.BlockSpec((nh, tm, dd), lambda i: (0, i, 0))
    vt = lambda nh, dd: jax.ShapeDtypeStruct((nh, s_tot // tm, dd, tm), BF16)
    vtspec = lambda nh, dd: pl.BlockSpec((nh, 1, dd, tm), lambda i: (0, i, 0, 0))
    return pl.pallas_call(
        _attn_prep_kernel,
        name="attn_prep",
        out_shape=(hm(8, 64), hm(2, 64), vt(2, 64), hm(8, 64), hm(8, 64), vt(4, 128)),
        grid=(s_tot // tm,),
        in_specs=[pl.BlockSpec((tm, 512), lambda i: (i, 10)),
                  pl.BlockSpec((tm, 256), lambda i: (i, 22)),
                  pl.BlockSpec((tm, 512), lambda i: (i, 7)),
                  pl.BlockSpec((tm, 512), lambda i: (i, 8)),
                  pl.BlockSpec((tm, 512), lambda i: (i, 9)),
                  pl.BlockSpec((1, 128), lambda i: (0, 0)),
                  pl.BlockSpec((1, 128), lambda i: (0, 0)),
                  pl.BlockSpec((tm, 128), lambda i: (i, 0)),
                  pl.BlockSpec((tm, 128), lambda i: (i, 0)),
                  pl.BlockSpec((128, 128), lambda i: (0, 0))],
        out_specs=(hspec(8, 64), hspec(2, 64), vtspec(2, 64), hspec(8, 64), hspec(8, 64), vtspec(4, 128)),
        compiler_params=_cp("parallel"),
    )(p, p, p, p, p, qgain, kgain, cos_t, sin_t, gmat)


def _flash_t(pairs, vt_ref, is_ctx_tile, n_ctx_chunks, n_chunks, m_ref, l_ref, acc_ref, st_ref):
    tk = ATT_TK
    m_ref[...] = jnp.full_like(m_ref, -jnp.inf)
    l_ref[...] = jnp.zeros_like(l_ref)
    acc_ref[...] = jnp.zeros_like(acc_ref)

    n = jnp.where(is_ctx_tile, n_ctx_chunks, n_chunks)

    def scores(c, slot):
        off = pl.multiple_of(c * tk, tk)
        parts = [lax.dot_general(k_ref[kidx, pl.ds(off, tk), :], q, NT, preferred_element_type=F32)
                 for q, k_ref, kidx in pairs]
        st_ref[slot] = parts[0] if len(parts) == 1 else jnp.concatenate(parts, axis=1)

    def update(c, slot):
        st = st_ref[slot]
        m_prev = m_ref[...]
        m_new = jnp.maximum(m_prev, jnp.max(st, axis=0, keepdims=True))
        alpha = jnp.exp2(m_prev - m_new)
        pt = jnp.exp2(st - m_new)
        l_ref[...] = alpha * l_ref[...] + jnp.sum(pt, axis=0, keepdims=True)
        acc_ref[...] = alpha * acc_ref[...] + _dot(vt_ref[0, c], pt.astype(BF16))
        m_ref[...] = m_new

    scores(0, 0)

    def body(it, carry):
        c = 2 * it
        scores(c + 1, 1)
        update(c, 0)
        scores(jnp.minimum(c + 2, n - 1), 0)
        update(c + 1, 1)
        return carry

    lax.fori_loop(0, n // 2, body, 0)

    @pl.when(n % 2 == 1)
    def _():
        update(n - 1, 0)

    return acc_ref[...] / l_ref[...]


def _gqa_kernel(q_ref, k_ref, vt_ref, o_ref, m_ref, l_ref, acc_ref, st_ref, *, tq, ctx, s_tot):
    q = q_ref[...].reshape(GQA_GROUP * tq, HEAD_D)
    ot = _flash_t([(q, k_ref, 0)], vt_ref, pl.program_id(1) < ctx // tq, ctx // ATT_TK, s_tot // ATT_TK,
                  m_ref, l_ref, acc_ref, st_ref)
    o2 = jnp.concatenate([ot[:, h * tq:(h + 1) * tq] for h in range(GQA_GROUP)], axis=0)
    o_ref[...] = o2.T.astype(o_ref.dtype)


def _gqa_attention(qg, kg, vgt, *, ctx):
    nh, s_tot, _ = qg.shape
    tq = ATT_TK
    m = GQA_GROUP * tq
    return pl.pallas_call(
        functools.partial(_gqa_kernel, tq=tq, ctx=ctx, s_tot=s_tot),
        name="gqa_attention",
        out_shape=jax.ShapeDtypeStruct((s_tot, nh * HEAD_D), BF16),
        grid=(nh // GQA_GROUP, s_tot // tq),
        in_specs=[pl.BlockSpec((GQA_GROUP, tq, HEAD_D), lambda g, i: (g, i, 0)),
                  pl.BlockSpec((1, s_tot, HEAD_D), lambda g, i: (g, 0, 0)),
                  pl.BlockSpec((1, s_tot // ATT_TK, HEAD_D, ATT_TK), lambda g, i: (g, 0, 0, 0))],
        out_specs=pl.BlockSpec((tq, GQA_GROUP * HEAD_D), lambda g, i: (i, g)),
        scratch_shapes=[pltpu.VMEM((1, m), F32), pltpu.VMEM((1, m), F32), pltpu.VMEM((HEAD_D, m), F32),
                        pltpu.VMEM((2, ATT_TK, m), F32)],
        compiler_params=_cp("parallel", "parallel"),
    )(qg, kg, vgt)


def _diff_kernel(q_ref, k_ref, vt_ref, lam_ref, gain_ref, o_ref, m_ref, l_ref, acc_ref, st_ref, *, tq, ctx, s_tot,
                 lam_init):
    ot = _flash_t([(q_ref[0], k_ref, 0), (q_ref[1], k_ref, 1)], vt_ref, pl.program_id(1) < ctx // tq,
                  ctx // ATT_TK, s_tot // ATT_TK, m_ref, l_ref, acc_ref, st_ref)
    lv = lam_ref[...]
    lam = (jnp.exp(jnp.sum(lv[0:1] * lv[1:2], axis=1, keepdims=True))
           - jnp.exp(jnp.sum(lv[2:3] * lv[3:4], axis=1, keepdims=True)) + lam_init)
    o = (ot[:, :tq] - lam * ot[:, tq:]).T
    y = o * lax.rsqrt(jnp.mean(o * o, axis=-1, keepdims=True) + EPS) * gain_ref[...]
    o_ref[...] = (y * (1.0 - lam_init)).astype(o_ref.dtype)


def _diff_attention(qd, kd, vdt, lam_vecs, sub_gain, *, ctx, lam_init):
    nh, n_chunks, dv, _ = vdt.shape
    s_tot = qd.shape[1]
    tq = ATT_TK
    return pl.pallas_call(
        functools.partial(_diff_kernel, tq=tq, ctx=ctx, s_tot=s_tot, lam_init=lam_init),
        name="diff_attention",
        out_shape=jax.ShapeDtypeStruct((s_tot, nh * dv), BF16),
        grid=(nh, s_tot // tq),
        in_specs=[pl.BlockSpec((2, tq, HEAD_D), lambda h, i: (h, i, 0)),
                  pl.BlockSpec((2, s_tot, HEAD_D), lambda h, i: (h, 0, 0)),
                  pl.BlockSpec((1, n_chunks, dv, ATT_TK), lambda h, i: (h, 0, 0, 0)),
                  pl.BlockSpec(lam_vecs.shape, lambda h, i: (0, 0)),
                  pl.BlockSpec((1, dv), lambda h, i: (0, 0))],
        out_specs=pl.BlockSpec((tq, dv), lambda h, i: (i, h)),
        scratch_shapes=[pltpu.VMEM((1, 2 * tq), F32), pltpu.VMEM((1, 2 * tq), F32), pltpu.VMEM((dv, 2 * tq), F32),
                        pltpu.VMEM((2, ATT_TK, 2 * tq), F32)],
        compiler_params=_cp("parallel", "parallel"),
    )(qd, kd, vdt, lam_vecs, sub_gain)


def _hy_filter_kernel(w1_ref, b1_ref, w2_ref, b2_ref, w3_ref, b3_ref, w4_ref, fr_ref, o_ref, *, tp, l, n, width):
    i = pl.program_id(0)
    idx = i * tp + lax.broadcasted_iota(jnp.int32, (tp, 1), 0)
    pos = jnp.where(idx < l, idx, n - idx).astype(F32)
    t = pos / (l - 1.0)
    w = (2.0 * math.pi) * pos / l
    lane = lax.broadcasted_iota(jnp.int32, (1, 128), 1)
    band = jnp.where(lane <= HY_BANDS, lane - 1, lane - 1 - HY_BANDS).astype(F32)
    f = 1e-4 + band * ((HY_BANDS - 1 - 1e-4) / (HY_BANDS - 1))
    ang = w * f
    z = jnp.where(lane == 0, t,
                  jnp.where(lane <= HY_BANDS, jnp.cos(ang),
                            jnp.where(lane <= 2 * HY_BANDS, -jnp.sin(ang), 0.0)))
    fr = fr_ref[...]
    h = jnp.sin(fr * (_dot(z, w1_ref[...], HI) + b1_ref[...]))
    h = jnp.sin(fr * (_dot(h, w2_ref[...], HI) + b2_ref[...]))
    h = jnp.sin(fr * (_dot(h, w3_ref[...], HI) + b3_ref[...]))
    h = _dot(h, w4_ref[...], HI)
    ch = lax.broadcasted_iota(jnp.int32, (1, width), 1).astype(F32)
    min_decay = math.log(HY_TARGET) / HY_SLOW_DECAY
    max_decay = math.log(HY_TARGET) / HY_FAST_DECAY
    deltas = min_decay + ch * ((max_decay - min_decay) / (width - 1))
    window = jnp.exp(-t * jnp.abs(deltas))
    valid = jnp.logical_or(idx < l, idx > n - l)
    o_ref[...] = jnp.where(valid, h * window, 0.0)


def _hy_filter(filt, l, n):
    w1p, b1, w2, b2, w3, b3, w4, fr = filt
    width = w4.shape[1] // 2
    tp = min(512, l)
    full = lambda a: pl.BlockSpec(a.shape, lambda i: (0, 0))
    return pl.pallas_call(
        functools.partial(_hy_filter_kernel, tp=tp, l=l, n=n, width=width),
        name="hy_filter",
        out_shape=jax.ShapeDtypeStruct((n, width), F32),
        grid=(n // tp,),
        in_specs=[full(w1p), full(b1), full(w2), full(b2), full(w3), full(b3),
                  pl.BlockSpec((w4.shape[0], width), lambda i: (0, (i * tp >= l).astype(jnp.int32))),
                  full(fr)],
        out_specs=pl.BlockSpec((tp, width), lambda i: (i, 0)),
        compiler_params=_cp("parallel"),
    )(w1p, b1, w2, b2, w3, b3, w4, fr)


def _hy_ctx_kernel(x0_ref, x1_ref, v_ref, kern_ref, skip_ref, o_ref, ks_ref, u_ref, *, l):
    u = x1_ref[...] * v_ref[...]
    u_ref[...] = u
    base = jnp.concatenate([kern_ref[l:2 * l, :], kern_ref[0:l, :]], axis=0)
    ks_ref[0] = base
    for b in range(1, 8):
        ks_ref[b] = pltpu.roll(base, b, 0)

    def body(a, acc):
        start = pl.multiple_of(l - 8 * a, 8)
        ublk = u_ref[pl.ds(pl.multiple_of(8 * a, 8), 8), :]
        for b in range(8):
            acc = acc + ks_ref[b, pl.ds(start, l), :] * ublk[b:b + 1, :]
        return acc

    y = lax.fori_loop(0, l // 8, body, jnp.zeros((l, 128), F32))
    o_ref[...] = (x0_ref[...] * (y + u * skip_ref[...])).astype(o_ref.dtype)


def _hy_ctx(p, kern, skip, *, ctx):
    width = skip.shape[1]
    nb = width // 128
    return pl.pallas_call(
        functools.partial(_hy_ctx_kernel, l=ctx),
        name="hy_ctx_conv",
        out_shape=jax.ShapeDtypeStruct((ctx, width), BF16),
        grid=(nb,),
        in_specs=[pl.BlockSpec((ctx, 128), lambda c: (0, c)),
                  pl.BlockSpec((ctx, 128), lambda c: (0, nb + c)),
                  pl.BlockSpec((ctx, 128), lambda c: (0, 2 * nb + c)),
                  pl.BlockSpec((2 * ctx, 128), lambda c: (0, c)),
                  pl.BlockSpec((1, 128), lambda c: (0, c))],
        out_specs=pl.BlockSpec((ctx, 128), lambda c: (0, c)),
        scratch_shapes=[pltpu.VMEM((8, 2 * ctx, 128), F32), pltpu.VMEM((ctx, 128), F32)],
        compiler_params=_cp("parallel"),
    )(p, p, p, kern, skip)


def _hy_prep_kernel(x0_ref, x1_ref, v_ref, u_ref, x0o_ref):
    u_ref[...] = x1_ref[...] * v_ref[...]
    x0o_ref[...] = x0_ref[...]


def _hy_prep(p, *, ctx, width):
    s_lat = p.shape[0] - ctx
    tm = 256
    off = ctx // tm
    return pl.pallas_call(
        _hy_prep_kernel,
        name="hy_prep",
        out_shape=(jax.ShapeDtypeStruct((s_lat, width), F32), jax.ShapeDtypeStruct((s_lat, width), F32)),
        grid=(s_lat // tm,),
        in_specs=[pl.BlockSpec((tm, width), lambda i: (i + off, 0)),
                  pl.BlockSpec((tm, width), lambda i: (i + off, 1)),
                  pl.BlockSpec((tm, width), lambda i: (i + off, 2))],
        out_specs=(pl.BlockSpec((tm, width), lambda i: (i, 0)), pl.BlockSpec((tm, width), lambda i: (i, 0))),
        compiler_params=_cp("parallel"),
    )(p, p, p)


def _dft_consts(n1):
    n2 = FFT_N2
    n = n1 * n2
    f1n = n1 // 2 + 1
    f1p = -(-f1n // 8) * 8
    f1 = np.arange(f1p)[:, None].astype(np.float64)
    live = (f1 < f1n).astype(np.float64)
    t1 = np.arange(n1)[None, :]
    ang = 2 * np.pi * f1 * t1 / n1
    w_re, w_im = np.cos(ang) * live, -np.sin(ang) * live
    t0 = np.arange(n2)
    ang = 2 * np.pi * f1[None, :, :] * t0[:, None, None] / n
    tw_fwd = np.concatenate([np.cos(ang), -np.sin(ang)], axis=-1)
    ang = 2 * np.pi * f1[:, :, None] * t0[None, :, None] / n
    tw_inv = np.concatenate([np.cos(ang), np.sin(ang)], axis=-1)
    ang = 2 * np.pi * np.outer(t0, t0) / n2
    d_re, d_im = np.cos(ang), -np.sin(ang)
    wgt = np.where((f1 == 0) | (f1 == n1 // 2), 1.0, 2.0) * live
    t1o = np.arange(n1 // 2)[:, None]
    ang = 2 * np.pi * t1o * f1.T / n1
    c_m, s_m = np.cos(ang) * wgt.T / n, -np.sin(ang) * wgt.T / n
    f = lambda a: jnp.asarray(a, F32)
    return dict(f1p=f1p, w_re=f(w_re), w_im=f(w_im), tw_fwd=f(tw_fwd), tw_inv=f(tw_inv),
                d_re=f(d_re), d_im=f(d_im), c_m=f(c_m), s_m=f(s_m))


def _fft1_kernel(u_ref, wre_ref, wim_ref, tw_ref, bre_ref, bim_ref, *, tb, c):
    wre = wre_ref[...]
    wim = wim_ref[...]
    for b in range(tb):
        sl = slice(b * c, (b + 1) * c)
        u = u_ref[:, sl]
        are = _dot(wre, u, HI)
        aim = _dot(wim, u, HI)
        tr = tw_ref[b, :, 0:1]
        ti = tw_ref[b, :, 1:2]
        bre_ref[:, sl] = are * tr - aim * ti
        bim_ref[:, sl] = are * ti + aim * tr


def _fft1(u2, w_re, w_im, tw_fwd, *, c):
    t1_in = u2.shape[0]
    f1p = w_re.shape[0]
    tb = 4
    out = jax.ShapeDtypeStruct((f1p, FFT_N2 * c), F32)
    return pl.pallas_call(
        functools.partial(_fft1_kernel, tb=tb, c=c),
        name="fft_outer",
        out_shape=(out, out),
        grid=(FFT_N2 // tb,),
        in_specs=[pl.BlockSpec((t1_in, tb * c), lambda i: (0, i)),
                  pl.BlockSpec((f1p, t1_in), lambda i: (0, 0)),
                  pl.BlockSpec((f1p, t1_in), lambda i: (0, 0)),
                  pl.BlockSpec((tb, f1p, 2), lambda i: (i, 0, 0))],
        out_specs=(pl.BlockSpec((f1p, tb * c), lambda i: (0, i)), pl.BlockSpec((f1p, tb * c), lambda i: (0, i))),
        compiler_params=_cp("parallel"),
    )(u2, w_re[:, :t1_in], w_im[:, :t1_in], tw_fwd)


def _fft_spec_kernel(bre_ref, bim_ref, dre_ref, dim_ref, xre_ref, xim_ref):
    br, bi = bre_ref[0], bim_ref[0]
    dr, di = dre_ref[...], dim_ref[...]
    xre_ref[0] = _dot(dr, br, HI) - _dot(di, bi, HI)
    xim_ref[0] = _dot(dr, bi, HI) + _dot(di, br, HI)


def _fft_spectrum(b_re, b_im, d_re, d_im):
    f1p, n2, c = b_re.shape
    blk = pl.BlockSpec((1, n2, c), lambda i: (i, 0, 0))
    mat = pl.BlockSpec((n2, n2), lambda i: (0, 0))
    out = jax.ShapeDtypeStruct((f1p, n2, c), F32)
    return pl.pallas_call(
        _fft_spec_kernel, name="fft_spectrum", out_shape=(out, out), grid=(f1p,),
        in_specs=[blk, blk, mat, mat], out_specs=(blk, blk),
        compiler_params=_cp("parallel"),
    )(b_re, b_im, d_re, d_im)


def _fft_mid_kernel(bre_ref, bim_ref, kre_ref, kim_ref, dre_ref, dim_ref, tw_ref, ore_ref, oim_ref):
    br, bi = bre_ref[0], bim_ref[0]
    dr, di = dre_ref[...], dim_ref[...]
    xr = _dot(dr, br, HI) - _dot(di, bi, HI)
    xi = _dot(dr, bi, HI) + _dot(di, br, HI)
    kr, ki = kre_ref[0], kim_ref[0]
    yr = xr * kr - xi * ki
    yi = xr * ki + xi * kr
    zr = _dot(dr, yr, HI) + _dot(di, yi, HI)
    zi = _dot(dr, yi, HI) - _dot(di, yr, HI)
    tr = tw_ref[0, :, 0:1]
    ti = tw_ref[0, :, 1:2]
    ore_ref[0] = zr * tr - zi * ti
    oim_ref[0] = zr * ti + zi * tr


def _fft_mid(b_re, b_im, k_re, k_im, d_re, d_im, tw_inv):
    f1p, n2, c = b_re.shape
    blk = pl.BlockSpec((1, n2, c), lambda i: (i, 0, 0))
    mat = pl.BlockSpec((n2, n2), lambda i: (0, 0))
    out = jax.ShapeDtypeStruct((f1p, n2, c), F32)
    return pl.pallas_call(
        _fft_mid_kernel, name="fft_mid", out_shape=(out, out), grid=(f1p,),
        in_specs=[blk, blk, blk, blk, mat, mat, pl.BlockSpec((1, n2, 2), lambda i: (i, 0, 0))],
        out_specs=(blk, blk),
        compiler_params=_cp("parallel"),
    )(b_re, b_im, k_re, k_im, d_re, d_im, tw_inv)


def _fft_fin_kernel(bre_ref, bim_ref, cm_ref, sm_ref, x0_ref, u_ref, skip_ref, o_ref, *, tb, c):
    y = _dot(cm_ref[...], bre_ref[...], HI) + _dot(sm_ref[...], bim_ref[...], HI)
    skip = skip_ref[...]
    for b in range(tb):
        sl = slice(b * c, (b + 1) * c)
        o_ref[:, sl] = (x0_ref[:, sl] * (y[:, sl] + u_ref[:, sl] * skip)).astype(o_ref.dtype)


def _fft_fin(z_re, z_im, c_m, s_m, x0_2, u2, skip, *, c):
    f1p = z_re.shape[0]
    t1o = c_m.shape[0]
    tb = 4
    big = pl.BlockSpec((f1p, tb * c), lambda i: (0, i))
    sig = pl.BlockSpec((t1o, tb * c), lambda i: (0, i))
    mat = pl.BlockSpec((t1o, f1p), lambda i: (0, 0))
    return pl.pallas_call(
        functools.partial(_fft_fin_kernel, tb=tb, c=c),
        name="fft_final",
        out_shape=jax.ShapeDtypeStruct((t1o, FFT_N2 * c), BF16),
        grid=(FFT_N2 // tb,),
        in_specs=[big, big, mat, mat, sig, sig, pl.BlockSpec((1, c), lambda i: (0, 0))],
        out_specs=sig,
        compiler_params=_cp("parallel"),
    )(z_re, z_im, c_m, s_m, x0_2, u2, skip)


def _hyena_latent(p, kern, skip, *, ctx):
    c = skip.shape[1]
    s_lat = p.shape[0] - ctx
    n1 = 2 * s_lat // FFT_N2
    k = _dft_consts(n1)
    f1p = k["f1p"]
    u, x0 = _hy_prep(p, ctx=ctx, width=c)
    u2 = u.reshape(n1 // 2, FFT_N2 * c)
    x0_2 = x0.reshape(n1 // 2, FFT_N2 * c)
    kb_re, kb_im = _fft1(kern.reshape(n1, FFT_N2 * c), k["w_re"], k["w_im"], k["tw_fwd"], c=c)
    kf_re, kf_im = _fft_spectrum(kb_re.reshape(f1p, FFT_N2, c), kb_im.reshape(f1p, FFT_N2, c), k["d_re"], k["d_im"])
    b_re, b_im = _fft1(u2, k["w_re"], k["w_im"], k["tw_fwd"], c=c)
    z_re, z_im = _fft_mid(b_re.reshape(f1p, FFT_N2, c), b_im.reshape(f1p, FFT_N2, c), kf_re, kf_im,
                          k["d_re"], k["d_im"], k["tw_inv"])
    y2 = _fft_fin(z_re.reshape(f1p, FFT_N2 * c), z_im.reshape(f1p, FFT_N2 * c), k["c_m"], k["s_m"],
                  x0_2, u2, skip, c=c)
    return y2.reshape(s_lat, c)


def _softplus(x):
    return jnp.maximum(x, 0.0) + jnp.log(1.0 + jnp.exp(-jnp.abs(x)))


def _dn_intra_kernel(q_ref, k_ref, v_ref, sm_ref, cst_ref, u_ref, w_ref, qd_ref, kd_ref, qk_ref, last_ref):
    c = DN_CHUNK
    nh = q_ref.shape[1] // DN_HEAD
    row = lax.broadcasted_iota(jnp.int32, (c, c), 0)
    col = lax.broadcasted_iota(jnp.int32, (c, c), 1)
    eye = row == col
    ones = jnp.ones((c, c), F32)
    sm = sm_ref[...]
    beta_all = jax.nn.sigmoid(sm)
    g_all = -jnp.exp(cst_ref[0:1, :]) * _softplus(sm + cst_ref[1:2, :])
    tot_all = jnp.sum(g_all, axis=0, keepdims=True)
    incl = (row >= col, row <= col)
    strict = (row > col, row < col)
    gc_all = (_dot(incl[0].astype(F32), g_all, HI), _dot(incl[1].astype(F32), g_all, HI))

    for h in range(nh):
        hs = slice(h * DN_HEAD, (h + 1) * DN_HEAD)
        qh = _silu(q_ref[:, hs])
        kh = _silu(k_ref[:, hs])
        vh = _silu(v_ref[:, hs])
        qh = qh * lax.rsqrt(jnp.sum(qh * qh, axis=-1, keepdims=True) + EPS) * (DN_HEAD ** -0.5)
        kh = kh * lax.rsqrt(jnp.sum(kh * kh, axis=-1, keepdims=True) + EPS)
        qkt = lax.dot_general(qh, kh, NT, precision=HI, preferred_element_type=F32)
        for d in range(2):
            bi = d * nh + h
            gi = 2 * nh + bi
            beta = beta_all[:, bi:bi + 1]
            gcol = gc_all[d][:, gi:gi + 1]
            gtot = tot_all[:, gi:gi + 1]
            grow = _dot(ones, jnp.where(eye, gcol, 0.0), HI)
            gamma = jnp.where(incl[d], jnp.exp(jnp.where(incl[d], gcol - grow, 0.0)), 0.0)
            egc = jnp.exp(gcol)
            kb = kh * beta
            kkt = lax.dot_general(kb, kh, NT, precision=HI, preferred_element_type=F32)
            pw = -jnp.where(strict[d], kkt * gamma, 0.0)
            sol = jnp.concatenate([vh * beta, kb * egc], axis=1)
            steps = int(math.log2(c))
            for t in range(steps):
                sol = sol + _dot(pw, sol, HI)
                if t + 1 < steps:
                    pw = _dot(pw, pw, HI)
            u_ref[d, :, hs] = sol[:, :DN_HEAD]
            w_ref[d, :, hs] = sol[:, DN_HEAD:]
            qd_ref[d, :, hs] = qh * egc
            kd_ref[d, :, hs] = kh * jnp.exp(gtot - gcol)
            qk_ref[d, :, h * c:(h + 1) * c] = qkt * gamma
            last_ref[d, 0, :, hs] = jnp.broadcast_to(jnp.exp(gtot), (8, DN_HEAD))


def _dn_intra(p, cst):
    s_tot = p.shape[0]
    c = DN_CHUNK
    w = 512
    nch = s_tot // c
    big = jax.ShapeDtypeStruct((2, s_tot, w), F32)
    bspec = pl.BlockSpec((2, c, w), lambda j: (0, j, 0))
    return pl.pallas_call(
        _dn_intra_kernel,
        name="dn_intra",
        out_shape=(big, big, big, big, jax.ShapeDtypeStruct((2, s_tot, w // 2), F32),
                   jax.ShapeDtypeStruct((2, nch, 8, w), F32)),
        grid=(nch,),
        in_specs=[pl.BlockSpec((c, w), lambda j: (j, 3)),
                  pl.BlockSpec((c, w), lambda j: (j, 4)),
                  pl.BlockSpec((c, w), lambda j: (j, 5)),
                  pl.BlockSpec((c, 128), lambda j: (j, 46)),
                  pl.BlockSpec((8, 128), lambda j: (0, 0))],
        out_specs=(bspec, bspec, bspec, bspec, pl.BlockSpec((2, c, w // 2), lambda j: (0, j, 0)),
                   pl.BlockSpec((2, 1, 8, w), lambda j: (0, j, 0, 0))),
        compiler_params=_cp("parallel"),
    )(p, p, p, p, cst)


def _dn_rec_kernel(*refs, nh):
    ins, (of_ref, ob_ref, s_ref) = refs[:12], refs[12:]
    o_refs = (of_ref, ob_ref)

    @pl.when(pl.program_id(0) == 0)
    def _():
        s_ref[...] = jnp.zeros_like(s_ref)

    c = DN_CHUNK
    for d in range(2):
        u_ref, w_ref, qd_ref, kd_ref, qk_ref, last_ref = ins[6 * d:6 * d + 6]
        for h in range(nh):
            hs = slice(h * DN_HEAD, (h + 1) * DN_HEAD)
            s = s_ref[d * nh + h]
            v_new = u_ref[0, :, hs] - _dot(w_ref[0, :, hs], s, HI)
            o = _dot(qd_ref[0, :, hs], s, HI) + _dot(qk_ref[0, :, h * c:(h + 1) * c], v_new, HI)
            s_ref[d * nh + h] = s * last_ref[0, 0, 0:1, hs] + lax.dot_general(
                kd_ref[0, :, hs], v_new, TN, precision=HI, preferred_element_type=F32)
            o_refs[d][:, hs] = o


def _dn_recurrence(u, w, qd, kd, qk, last, *, ctx):
    _, s_tot, width = u.shape
    c = DN_CHUNK
    nch = s_tot // c
    nctx = ctx // c
    nh = width // DN_HEAD
    fwd = lambda j: j
    bwd = lambda j: jnp.where(j < nctx, nctx - 1 - j, nch - 1 - (j - nctx))
    specs = []
    for d, cm in ((0, fwd), (1, bwd)):
        blk = lambda ww, cm=cm, d=d: pl.BlockSpec((1, c, ww), lambda j: (d, cm(j), 0))
        specs += [blk(width), blk(width), blk(width), blk(width), blk(width // 2),
                  pl.BlockSpec((1, 1, 8, width), lambda j, cm=cm, d=d: (d, cm(j), 0, 0))]
    out = jax.ShapeDtypeStruct((s_tot, width), F32)
    return pl.pallas_call(
        functools.partial(_dn_rec_kernel, nh=nh),
        name="dn_scan",
        out_shape=(out, out),
        grid=(nch,),
        in_specs=specs,
        out_specs=(pl.BlockSpec((c, width), lambda j: (fwd(j), 0)), pl.BlockSpec((c, width), lambda j: (bwd(j), 0))),
        scratch_shapes=[pltpu.VMEM((2 * nh, DN_HEAD, DN_HEAD), F32)],
        compiler_params=_cp("arbitrary"),
    )(u, w, qd, kd, qk, last, u, w, qd, kd, qk, last)


def _dn_finish_kernel(of_ref, ob_ref, gate_ref, gain_ref, o_ref):
    nh = of_ref.shape[1] // DN_HEAD
    for h in range(nh):
        hs = slice(h * DN_HEAD, (h + 1) * DN_HEAD)
        o = of_ref[:, hs] + ob_ref[:, hs]
        y = o * lax.rsqrt(jnp.mean(o * o, axis=-1, keepdims=True) + EPS) * gain_ref[...]
        o_ref[:, hs] = (y * _silu(gate_ref[:, hs])).astype(o_ref.dtype)


def _dn_finish(o_f, o_b, p, gain):
    s_tot, width = o_f.shape
    tm = _row_tile(s_tot, (528, 640, 256, 128))
    blk = pl.BlockSpec((tm, width), lambda i: (i, 0))
    return pl.pallas_call(
        _dn_finish_kernel,
        name="dn_finish",
        out_shape=jax.ShapeDtypeStruct((s_tot, width), BF16),
        grid=(s_tot // tm,),
        in_specs=[blk, blk, pl.BlockSpec((tm, width), lambda i: (i, 6)), pl.BlockSpec((1, DN_HEAD), lambda i: (0, 0))],
        out_specs=blk,
        compiler_params=_cp("parallel"),
    )(o_f, o_b, p, gain)


def _rope_tables(ctx, s_lat):
    rows = s_lat // GRID_W
    n_freq = HEAD_D // 4
    row = np.repeat(np.arange(rows, dtype=np.float32), GRID_W)
    col = np.tile(np.arange(GRID_W, dtype=np.float32), rows)
    inv = (ROPE_THETA ** (-np.arange(n_freq, dtype=np.float32) / n_freq)).astype(np.float32)
    ang = np.concatenate([row[:, None] * inv, col[:, None] * inv], axis=-1).astype(np.float32)
    cos, sin = np.cos(ang), np.sin(ang)
    cos = np.concatenate([np.ones((ctx, HEAD_D // 2), np.float32), cos], axis=0)
    sin = np.concatenate([np.zeros((ctx, HEAD_D // 2), np.float32), sin], axis=0)
    cos_t = np.tile(np.concatenate([cos, cos], axis=1), (1, 2))
    sin_t = np.tile(np.concatenate([-sin, sin], axis=1), (1, 2))
    return jnp.asarray(cos_t, F32), jnp.asarray(sin_t, F32)


def _pad_rows(a, rows):
    return jnp.concatenate([a, jnp.zeros((rows - a.shape[0],) + a.shape[1:], a.dtype)], axis=0)


def kernel(x, c, ctx, c_ctx, w_ada, b_ada, norm_mix_pre, norm_mix_post, norm_ffn_pre, norm_ffn_post, w_in, w_out, attn_q_norm, attn_k_norm, hy_short, hy_w1, hy_b1, hy_w2, hy_b2, hy_w3, hy_b3, hy_w4, hy_freq, hy_skip, dn_short, dn_a_log, dn_dt_bias, dn_norm, df_lambda, df_norm, ffn_up, ffn_conv, ffn_down):
    batch, s_lat, d = x.shape
    n_ctx = ctx.shape[1]
    depth = w_in.shape[0]
    gw = d // 4
    assert batch == 1 and gw == 512 and n_ctx % 256 == 0 and s_lat % 512 == 0
    s_tot = n_ctx + s_lat

    xx = jnp.concatenate([ctx[0], x[0]], axis=0)
    c_rows = _pad_rows(jnp.concatenate([c, c_ctx[None, :]], axis=0), 8)
    mods = _mod_vectors(c_rows, w_ada, b_ada)

    cos_t, sin_t = _rope_tables(n_ctx, s_lat)
    lane = np.arange(128)
    gmat = jnp.asarray((lane[:, None] // HEAD_D == lane[None, :] // HEAD_D) / HEAD_D, F32)
    ident = jnp.asarray([[0.0], [1.0], [0.0]], F32)

    a_cols = gw + 2 * 2 * HEAD_D
    o_hy, o_dn, o_df = a_cols, a_cols + 3 * gw, a_cols + 3 * gw + 4 * gw + 16
    order = [(o_hy, o_hy + 3 * gw), (o_dn, o_dn + 4 * gw), (o_df, o_df + 3 * gw), (0, a_cols),
             (o_dn + 4 * gw, o_dn + 4 * gw + 16)]
    n_proj = 12 * gw
    used = sum(b - a for a, b in order)

    for i in range(depth):
        lam_init = 0.8 - 0.6 * math.exp(-0.3 * i)
        mod = mods[i]
        sh1, sc1, gt1, sh2, sc2, gt2 = (mod[:, k * d:(k + 1) * d] for k in range(6))
        mod_mix = _pad_rows(jnp.concatenate([sh1[0:1], sc1[0:1], sh1[1:2], sc1[1:2]], axis=0), 8)
        mod_ffn = _pad_rows(jnp.concatenate([sh2[0:1], sc2[0:1], sh2[1:2], sc2[1:2]], axis=0), 8)

        w_in_p = jnp.concatenate([w_in[i][:, a:b] for a, b in order]
                                 + [jnp.zeros((d, n_proj - used), F32)], axis=1).astype(BF16)
        cw_in = jnp.concatenate([hy_short[i], dn_short[i], jnp.broadcast_to(ident, (3, n_proj - 6 * gw))], axis=1)
        p = _norm_mod_matmul_conv(xx, norm_mix_pre[i][None, :], mod_mix, w_in_p, _pad_rows(cw_in, 8),
                                  ctx=n_ctx, ffn=False)

        qg, kg, vgt, qd, kd, vdt = _attn_prep(p, jnp.tile(attn_q_norm[i], 2)[None, :],
                                              jnp.tile(attn_k_norm[i], 2)[None, :], cos_t, sin_t, gmat)
        ya = _gqa_attention(qg, kg, vgt, ctx=n_ctx)
        yd = _diff_attention(qd, kd, vdt, df_lambda[i], df_norm[i][None, :], ctx=n_ctx, lam_init=lam_init)

        filt = (_pad_rows(hy_w1[i], 128), hy_b1[i][None, :], hy_w2[i], hy_b2[i][None, :], hy_w3[i],
                hy_b3[i][None, :], hy_w4[i], hy_freq[i][None, :])
        skip = hy_skip[i][None, :]
        yb_lat = _hyena_latent(p, _hy_filter(filt, s_lat, 2 * s_lat), skip, ctx=n_ctx)
        if i < depth - 1:
            yb_ctx = _hy_ctx(p, _hy_filter(filt, n_ctx, 2 * n_ctx), skip, ctx=n_ctx)
        else:
            yb_ctx = jnp.zeros((n_ctx, gw), BF16)
        yb = jnp.concatenate([yb_ctx, yb_lat], axis=0)

        zeros8 = jnp.zeros((8,), F32)
        cst = _pad_rows(jnp.stack([jnp.concatenate([zeros8, dn_a_log[i].reshape(-1), jnp.zeros((112,), F32)]),
                                   jnp.concatenate([zeros8, dn_dt_bias[i].reshape(-1), jnp.zeros((112,), F32)])]), 8)
        o_f, o_b = _dn_recurrence(*_dn_intra(p, cst), ctx=n_ctx)
        yc = _dn_finish(o_f, o_b, p, dn_norm[i][None, :])

        y_cat = jnp.concatenate([ya, yb, yc, yd], axis=1)
        xx = _matmul_norm_residual(y_cat, w_out[i].astype(BF16), xx, gt1, norm_mix_post[i][None, :], ctx=n_ctx)

        g = _norm_mod_matmul_conv(xx, norm_ffn_pre[i][None, :], mod_ffn, ffn_up[i].astype(BF16),
                                  _pad_rows(ffn_conv[i], 8), ctx=n_ctx, ffn=True)
        xx = _matmul_norm_residual(g, ffn_down[i].astype(BF16), xx, gt2, norm_ffn_post[i][None, :], ctx=n_ctx)

    return xx[n_ctx:][None]
```

```python
import functools
import math

import jax
import jax.numpy as jnp
import numpy as np
from jax import lax
from jax.experimental import pallas as pl
from jax.experimental.pallas import tpu as pltpu

F32 = jnp.float32
BF16 = jnp.bfloat16
HI = lax.Precision.HIGHEST
EPS = 1e-6

GRID_W = 64
ROPE_THETA = 10000.0
HEAD_D = 64
GQA_GROUP = 4
DN_HEAD = 128
DN_CHUNK = 64
DN_PASSES = 3
HY_BANDS = 16
HY_FAST_DECAY, HY_SLOW_DECAY, HY_TARGET = 0.3, 1.5, 1e-2
FFT_N2 = 128
ATT_TK = 256
DF_HPS = 2
LOG2E = 1.4426950408889634
VMEM_LIMIT = 56 * 1024 * 1024
NT = (((1,), (1,)), ((), ()))
TN = (((0,), (0,)), ((), ()))


def _cp(*sem):
    return pltpu.CompilerParams(dimension_semantics=sem, vmem_limit_bytes=VMEM_LIMIT)


def _dot(a, b, precision=None):
    return jnp.dot(a, b, precision=precision, preferred_element_type=F32)


def _silu(x):
    return x * jax.nn.sigmoid(x)


def _row_tile(n, cands):
    for t in cands:
        if n % t == 0:
            return t
    raise ValueError(f"no row tile for {n}")


def _mod_kernel(c_ref, w_ref, b_ref, o_ref):
    o_ref[0] = _dot(_silu(c_ref[...]), w_ref[0], HI) + b_ref[0]


def _mod_vectors(c_rows, w_ada, b_ada):
    depth, d, n = w_ada.shape
    tn = 512
    return pl.pallas_call(
        _mod_kernel,
        name="mod_vectors",
        out_shape=jax.ShapeDtypeStruct((depth, 8, n), F32),
        grid=(depth, n // tn),
        in_specs=[pl.BlockSpec((8, d), lambda l, j: (0, 0)),
                  pl.BlockSpec((1, d, tn), lambda l, j: (l, 0, j)),
                  pl.BlockSpec((1, 1, tn), lambda l, j: (l, 0, j))],
        out_specs=pl.BlockSpec((1, 8, tn), lambda l, j: (l, 0, j)),
        compiler_params=_cp("parallel", "parallel"),
    )(c_rows, w_ada, b_ada.reshape(depth, 1, n))


HALO = 16


def _k1_kernel(xp_ref, xm_ref, xn_ref, g_ref, mod_ref, *rest, tm, ctx, s_tot, ffn, n_conv):
    if ffn:
        wa_ref, wb_ref, cwa_ref, cwb_ref, o_ref, h_ref, acc_ref = rest
    else:
        wa_ref, cwa_ref, o_ref, h_ref, acc_ref = rest
    i = pl.program_id(0)
    j = pl.program_id(1)

    @pl.when(j == 0)
    def _():
        def nm(xv, row0):
            ms = jnp.mean(xv * xv, axis=-1, keepdims=True)
            y = xv * lax.rsqrt(ms + EPS) * g_ref[...]
            rows = row0 + lax.broadcasted_iota(jnp.int32, (xv.shape[0], 1), 0)
            is_ctx = rows < ctx
            sh = jnp.where(is_ctx, mod_ref[2:3, :], mod_ref[0:1, :])
            sc = jnp.where(is_ctx, mod_ref[3:4, :], mod_ref[1:2, :])
            return (y * (1.0 + sc) + sh).astype(BF16)

        h_ref[0:HALO] = nm(xp_ref[...], i * tm - HALO)
        h_ref[HALO:HALO + tm] = nm(xm_ref[...], i * tm)
        h_ref[HALO + tm:] = nm(xn_ref[...], i * tm + tm)

    def conv(w_ref, cw_ref, edge):
        acc_ref[...] = _dot(h_ref[...], w_ref[...])
        prev = acc_ref[HALO - 1:HALO - 1 + tm]
        nxt = acc_ref[HALO + 1:HALO + 1 + tm]
        if edge:
            rows = i * tm + lax.broadcasted_iota(jnp.int32, (tm, 1), 0)
            prev = jnp.where(jnp.logical_and(rows != 0, rows != ctx), prev, 0.0)
            nxt = jnp.where(jnp.logical_and(rows != ctx - 1, rows != s_tot - 1), nxt, 0.0)
        cw = cw_ref[...]
        return acc_ref[HALO:HALO + tm] * cw[1:2] + prev * cw[0:1] + nxt * cw[2:3]

    def emit(edge):
        if ffn:
            a = conv(wa_ref, cwa_ref, edge)
            b = conv(wb_ref, cwb_ref, edge)
            o_ref[...] = (_silu(a) * b).astype(o_ref.dtype)
        else:
            o_ref[...] = conv(wa_ref, cwa_ref, edge).astype(o_ref.dtype)

    edge_tiles = sorted({r // tm for r in (0, ctx - 1, ctx, s_tot - 1)})
    is_edge = functools.reduce(jnp.logical_or, [i == e for e in edge_tiles])
    is_conv = j < n_conv
    pl.when(jnp.logical_and(is_conv, is_edge))(lambda: emit(True))
    pl.when(jnp.logical_and(is_conv, jnp.logical_not(is_edge)))(lambda: emit(False))

    if not ffn:
        @pl.when(jnp.logical_not(is_conv))
        def _():
            o_ref[...] = _dot(h_ref[HALO:HALO + tm], wa_ref[...]).astype(o_ref.dtype)


def _norm_mod_matmul_conv(xx, gain, mod4, w, cw, *, ctx, ffn, conv_cols):
    s_tot, d = xx.shape
    n = w.shape[1]
    tm = _row_tile(s_tot, (528, 640, 256, 128))
    tn = 512
    nb = s_tot // HALO
    n_out = n // 2 if ffn else n
    x_specs = [
        pl.BlockSpec((HALO, d), lambda i, j: (jnp.maximum(i * (tm // HALO) - 1, 0), 0)),
        pl.BlockSpec((tm, d), lambda i, j: (i, 0)),
        pl.BlockSpec((HALO, d), lambda i, j: (jnp.minimum((i + 1) * (tm // HALO), nb - 1), 0)),
        pl.BlockSpec((1, d), lambda i, j: (0, 0)),
        pl.BlockSpec((8, d), lambda i, j: (0, 0)),
    ]
    if ffn:
        off = n_out // tn
        w_specs = [pl.BlockSpec((d, tn), lambda i, j: (0, j)),
                   pl.BlockSpec((d, tn), lambda i, j: (0, j + off)),
                   pl.BlockSpec((8, tn), lambda i, j: (0, j)),
                   pl.BlockSpec((8, tn), lambda i, j: (0, j + off))]
        args = (w, w, cw, cw)
        out_dtype = BF16
    else:
        w_specs = [pl.BlockSpec((d, tn), lambda i, j: (0, j)),
                   pl.BlockSpec((8, tn), lambda i, j: (0, j))]
        args = (w, cw)
        out_dtype = F32
    return pl.pallas_call(
        functools.partial(_k1_kernel, tm=tm, ctx=ctx, s_tot=s_tot, ffn=ffn, n_conv=conv_cols // tn),
        name="ffn_up_conv_gate" if ffn else "in_proj_conv",
        out_shape=jax.ShapeDtypeStruct((s_tot, n_out), out_dtype),
        grid=(s_tot // tm, n_out // tn),
        in_specs=x_specs + w_specs,
        out_specs=pl.BlockSpec((tm, tn), lambda i, j: (i, j)),
        scratch_shapes=[pltpu.VMEM((tm + 2 * HALO, d), BF16), pltpu.VMEM((tm + 2 * HALO, tn), F32)],
        compiler_params=_cp("parallel", "arbitrary"),
    )(xx, xx, xx, gain, mod4, *args)


def _k2_kernel(a_ref, w_ref, x_ref, gate_ref, gain_ref, o_ref, acc_ref, *, tm, ctx, nk):
    i = pl.program_id(0)
    k = pl.program_id(1)

    part = _dot(a_ref[...], w_ref[...])

    def finish(y):
        r = y * lax.rsqrt(jnp.mean(y * y, axis=-1, keepdims=True) + EPS) * gain_ref[...]
        rows = i * tm + lax.broadcasted_iota(jnp.int32, (tm, 1), 0)
        gt = jnp.where(rows < ctx, gate_ref[1:2, :], gate_ref[0:1, :])
        o_ref[...] = x_ref[...] + gt * r

    if nk == 1:
        finish(part)
        return

    @pl.when(k == 0)
    def _():
        acc_ref[...] = part

    @pl.when(jnp.logical_and(k > 0, k < nk - 1))
    def _():
        acc_ref[...] += part

    @pl.when(k == nk - 1)
    def _():
        finish(acc_ref[...] + part)


def _matmul_norm_residual(a, w, xx, gate2, gain, *, ctx):
    s_tot, kdim = a.shape
    d = w.shape[1]
    tm = _row_tile(s_tot, (528, 640, 256, 128))
    nk = 1 if kdim <= 2048 else 4
    tk = kdim // nk
    assert tk % 128 == 0
    return pl.pallas_call(
        functools.partial(_k2_kernel, tm=tm, ctx=ctx, nk=nk),
        name="matmul_norm_residual",
        out_shape=jax.ShapeDtypeStruct((s_tot, d), F32),
        grid=(s_tot // tm, nk),
        in_specs=[pl.BlockSpec((tm, tk), lambda i, k: (i, k)),
                  pl.BlockSpec((tk, d), lambda i, k: (k, 0)),
                  pl.BlockSpec((tm, d), lambda i, k: (i, 0)),
                  pl.BlockSpec((8, d), lambda i, k: (0, 0)),
                  pl.BlockSpec((1, d), lambda i, k: (0, 0))],
        out_specs=pl.BlockSpec((tm, d), lambda i, k: (i, 0)),
        scratch_shapes=[pltpu.VMEM((tm, d), F32)],
        compiler_params=_cp("parallel", "arbitrary"),
    )(a, w, xx, gate2, gain)


def _attn_prep_kernel(aq_ref, akv_ref, dq_ref, dk_ref, dv_ref, qgain_ref, kgain_ref, cos_ref, sin_ref,
                      gmat_ref, qg_ref, kg_ref, vgt_ref, qd_ref, kd_ref, vdt_ref):
    cosf = cos_ref[...]
    sins = sin_ref[...]
    lane = lax.broadcasted_iota(jnp.int32, (1, 128), 1)
    first_half = (lane & (HEAD_D - 1)) < (HEAD_D // 2)
    scale = HEAD_D ** -0.5 * LOG2E

    def rope(x):
        partner = jnp.where(first_half, pltpu.roll(x, 128 - HEAD_D // 2, 1), pltpu.roll(x, HEAD_D // 2, 1))
        return x * cosf + partner * sins

    def headnorm(x, gain):
        ms = _dot(x * x, gmat_ref[...], HI)
        return x * lax.rsqrt(ms + EPS) * gain

    def put(ref, pair, val):
        vb = val.astype(ref.dtype)
        ref[2 * pair] = vb[:, :HEAD_D]
        ref[2 * pair + 1] = vb[:, HEAD_D:]

    for ci in range(4):
        sl = slice(ci * 128, (ci + 1) * 128)
        put(qg_ref, ci, rope(headnorm(aq_ref[:, sl], qgain_ref[...])) * scale)
        put(qd_ref, ci, rope(dq_ref[:, sl]) * scale)
        put(kd_ref, ci, rope(dk_ref[:, sl]))
        vdt_ref[ci, 0] = dv_ref[:, sl].T.astype(vdt_ref.dtype)
    put(kg_ref, 0, rope(headnorm(akv_ref[:, 0:128], kgain_ref[...])))
    vt = akv_ref[:, 128:256].T.astype(vgt_ref.dtype)
    vgt_ref[0, 0] = vt[:HEAD_D]
    vgt_ref[1, 0] = vt[HEAD_D:]


def _attn_prep(p, qgain, kgain, cos_t, sin_t, gmat):
    s_tot = p.shape[0]
    tm = ATT_TK
    hm = lambda nh, dd: jax.ShapeDtypeStruct((nh, s_tot, dd), BF16)
    hspec = lambda nh, dd: pl.BlockSpec((nh, tm, dd), lambda i: (0, i, 0))
    vt = lambda nh, dd: jax.ShapeDtypeStruct((nh, s_tot // tm, dd, tm), BF16)
    vtspec = lambda nh, dd: pl.BlockSpec((nh, 1, dd, tm), lambda i: (0, i, 0, 0))
    return pl.pallas_call(
        _attn_prep_kernel,
        name="attn_prep",
        out_shape=(hm(8, 64), hm(2, 64), vt(2, 64), hm(8, 64), hm(8, 64), vt(4, 128)),
        grid=(s_tot // tm,),
        in_specs=[pl.BlockSpec((tm, 512), lambda i: (i, 10)),
                  pl.BlockSpec((tm, 256), lambda i: (i, 22)),
                  pl.BlockSpec((tm, 512), lambda i: (i, 7)),
                  pl.BlockSpec((tm, 512), lambda i: (i, 8)),
                  pl.BlockSpec((tm, 512), lambda i: (i, 9)),
                  pl.BlockSpec((1, 128), lambda i: (0, 0)),
                  pl.BlockSpec((1, 128), lambda i: (0, 0)),
                  pl.BlockSpec((tm, 128), lambda i: (i, 0)),
                  pl.BlockSpec((tm, 128), lambda i: (i, 0)),
                  pl.BlockSpec((128, 128), lambda i: (0, 0))],
        out_specs=(hspec(8, 64), hspec(2, 64), vtspec(2, 64), hspec(8, 64), hspec(8, 64), vtspec(4, 128)),
        compiler_params=_cp("parallel"),
    )(p, p, p, p, p, qgain, kgain, cos_t, sin_t, gmat)


def _flash_t(pairs, vt_ref, n_val, is_ctx_tile, n_ctx_chunks, n_chunks, m_ref, l_ref, acc_ref, st_ref):
    tk = ATT_TK
    m_ref[...] = jnp.full_like(m_ref, -jnp.inf)
    l_ref[...] = jnp.zeros_like(l_ref)
    acc_ref[...] = jnp.zeros_like(acc_ref)

    n = jnp.where(is_ctx_tile, n_ctx_chunks, n_chunks)

    def scores(c, slot):
        off = pl.multiple_of(c * tk, tk)
        parts = [lax.dot_general(k_ref[kidx, pl.ds(off, tk), :], q, NT, preferred_element_type=F32)
                 for q, k_ref, kidx in pairs]
        st_ref[slot] = parts[0] if len(parts) == 1 else jnp.concatenate(parts, axis=1)

    def update(c, slot):
        st = st_ref[slot]
        m_prev = m_ref[...]
        m_new = jnp.maximum(m_prev, jnp.max(st, axis=0, keepdims=True))
        alpha = jnp.exp2(m_prev - m_new)
        pt = jnp.exp2(st - m_new)
        l_ref[...] = alpha * l_ref[...] + jnp.sum(pt, axis=0, keepdims=True)
        pb = pt.astype(BF16)
        w = pb.shape[1] // n_val
        for g in range(n_val):
            sl = slice(g * w, (g + 1) * w)
            acc_ref[:, sl] = alpha[:, sl] * acc_ref[:, sl] + _dot(vt_ref[g, c], pb[:, sl])
        m_ref[...] = m_new

    scores(0, 0)

    def body(it, carry):
        c = 2 * it
        scores(c + 1, 1)
        update(c, 0)
        scores(jnp.minimum(c + 2, n - 1), 0)
        update(c + 1, 1)
        return carry

    lax.fori_loop(0, n // 2, body, 0)

    @pl.when(n % 2 == 1)
    def _():
        update(n - 1, 0)

    return acc_ref[...] / l_ref[...]


def _gqa_kernel(q_ref, k_ref, vt_ref, o_ref, m_ref, l_ref, acc_ref, st_ref, *, tq, ctx, s_tot):
    q = q_ref[...].reshape(GQA_GROUP * tq, HEAD_D)
    ot = _flash_t([(q, k_ref, 0)], vt_ref, 1, pl.program_id(1) < ctx // tq, ctx // ATT_TK, s_tot // ATT_TK,
                  m_ref, l_ref, acc_ref, st_ref)
    o2 = jnp.concatenate([ot[:, h * tq:(h + 1) * tq] for h in range(GQA_GROUP)], axis=0)
    o_ref[...] = o2.T.astype(o_ref.dtype)


def _gqa_attention(qg, kg, vgt, *, ctx):
    nh, s_tot, _ = qg.shape
    tq = ATT_TK
    m = GQA_GROUP * tq
    return pl.pallas_call(
        functools.partial(_gqa_kernel, tq=tq, ctx=ctx, s_tot=s_tot),
        name="gqa_attention",
        out_shape=jax.ShapeDtypeStruct((s_tot, nh * HEAD_D), BF16),
        grid=(nh // GQA_GROUP, s_tot // tq),
        in_specs=[pl.BlockSpec((GQA_GROUP, tq, HEAD_D), lambda g, i: (g, i, 0)),
                  pl.BlockSpec((1, s_tot, HEAD_D), lambda g, i: (g, 0, 0)),
                  pl.BlockSpec((1, s_tot // ATT_TK, HEAD_D, ATT_TK), lambda g, i: (g, 0, 0, 0))],
        out_specs=pl.BlockSpec((tq, GQA_GROUP * HEAD_D), lambda g, i: (i, g)),
        scratch_shapes=[pltpu.VMEM((1, m), F32), pltpu.VMEM((1, m), F32), pltpu.VMEM((HEAD_D, m), F32),
                        pltpu.VMEM((2, ATT_TK, m), F32)],
        compiler_params=_cp("parallel", "parallel"),
    )(qg, kg, vgt)


def _diff_kernel(q_ref, k_ref, vt_ref, lam_ref, gain_ref, o_ref, m_ref, l_ref, acc_ref, st_ref, *, tq, ctx, s_tot,
                 lam_init):
    pairs = [(q_ref[j], k_ref, j) for j in range(2 * DF_HPS)]
    ot = _flash_t(pairs, vt_ref, DF_HPS, pl.program_id(1) < ctx // tq, ctx // ATT_TK, s_tot // ATT_TK,
                  m_ref, l_ref, acc_ref, st_ref)
    lv = lam_ref[...]
    lam = (jnp.exp(jnp.sum(lv[0:1] * lv[1:2], axis=1, keepdims=True))
           - jnp.exp(jnp.sum(lv[2:3] * lv[3:4], axis=1, keepdims=True)) + lam_init)
    dv = ot.shape[0]
    for g in range(DF_HPS):
        o = (ot[:, 2 * g * tq:(2 * g + 1) * tq] - lam * ot[:, (2 * g + 1) * tq:(2 * g + 2) * tq]).T
        y = o * lax.rsqrt(jnp.mean(o * o, axis=-1, keepdims=True) + EPS) * gain_ref[...]
        o_ref[:, g * dv:(g + 1) * dv] = (y * (1.0 - lam_init)).astype(o_ref.dtype)


def _diff_attention(qd, kd, vdt, lam_vecs, sub_gain, *, ctx, lam_init):
    nh, n_chunks, dv, _ = vdt.shape
    s_tot = qd.shape[1]
    tq = ATT_TK
    g = DF_HPS
    m = 2 * g * tq
    return pl.pallas_call(
        functools.partial(_diff_kernel, tq=tq, ctx=ctx, s_tot=s_tot, lam_init=lam_init),
        name="diff_attention",
        out_shape=jax.ShapeDtypeStruct((s_tot, nh * dv), BF16),
        grid=(nh // g, s_tot // tq),
        in_specs=[pl.BlockSpec((2 * g, tq, HEAD_D), lambda h, i: (h, i, 0)),
                  pl.BlockSpec((2 * g, s_tot, HEAD_D), lambda h, i: (h, 0, 0)),
                  pl.BlockSpec((g, n_chunks, dv, ATT_TK), lambda h, i: (h, 0, 0, 0)),
                  pl.BlockSpec(lam_vecs.shape, lambda h, i: (0, 0)),
                  pl.BlockSpec((1, dv), lambda h, i: (0, 0))],
        out_specs=pl.BlockSpec((tq, g * dv), lambda h, i: (i, h)),
        scratch_shapes=[pltpu.VMEM((1, m), F32), pltpu.VMEM((1, m), F32), pltpu.VMEM((dv, m), F32),
                        pltpu.VMEM((2, ATT_TK, m), F32)],
        compiler_params=_cp("parallel", "parallel"),
    )(qd, kd, vdt, lam_vecs, sub_gain)


def _hy_filter_kernel(w1_ref, b1_ref, w2_ref, b2_ref, w3_ref, b3_ref, w4_ref, fr_ref, o_ref, *, tp, l, n, width):
    i = pl.program_id(0)
    idx = i * tp + lax.broadcasted_iota(jnp.int32, (tp, 1), 0)
    pos = jnp.where(idx < l, idx, n - idx).astype(F32)
    t = pos / (l - 1.0)
    w = (2.0 * math.pi) * pos / l
    lane = lax.broadcasted_iota(jnp.int32, (1, 128), 1)
    band = jnp.where(lane <= HY_BANDS, lane - 1, lane - 1 - HY_BANDS).astype(F32)
    f = 1e-4 + band * ((HY_BANDS - 1 - 1e-4) / (HY_BANDS - 1))
    phase = jnp.where(lane <= HY_BANDS, 0.5 * math.pi, math.pi)
    z = jnp.where(lane == 0, t, jnp.where(lane <= 2 * HY_BANDS, jnp.sin(w * f + phase), 0.0))
    fr = fr_ref[...]
    h = jnp.sin(fr * (_dot(z, w1_ref[...], HI) + b1_ref[...]))
    h = jnp.sin(fr * (_dot(h, w2_ref[...], HI) + b2_ref[...]))
    h = jnp.sin(fr * (_dot(h, w3_ref[...], HI) + b3_ref[...]))
    h = _dot(h, w4_ref[...], HI)
    ch = lax.broadcasted_iota(jnp.int32, (1, width), 1).astype(F32)
    min_decay = math.log(HY_TARGET) / HY_SLOW_DECAY
    max_decay = math.log(HY_TARGET) / HY_FAST_DECAY
    deltas = min_decay + ch * ((max_decay - min_decay) / (width - 1))
    window = jnp.exp(-t * jnp.abs(deltas))
    valid = jnp.logical_or(idx < l, idx > n - l)
    o_ref[...] = jnp.where(valid, h * window, 0.0)


def _hy_filter(filt, l, n):
    w1p, b1, w2, b2, w3, b3, w4, fr = filt
    width = w4.shape[1] // 2
    tp = min(512, l)
    full = lambda a: pl.BlockSpec(a.shape, lambda i: (0, 0))
    return pl.pallas_call(
        functools.partial(_hy_filter_kernel, tp=tp, l=l, n=n, width=width),
        name="hy_filter",
        out_shape=jax.ShapeDtypeStruct((n, width), F32),
        grid=(n // tp,),
        in_specs=[full(w1p), full(b1), full(w2), full(b2), full(w3), full(b3),
                  pl.BlockSpec((w4.shape[0], width), lambda i: (0, (i * tp >= l).astype(jnp.int32))),
                  full(fr)],
        out_specs=pl.BlockSpec((tp, width), lambda i: (i, 0)),
        compiler_params=_cp("parallel"),
    )(w1p, b1, w2, b2, w3, b3, w4, fr)


def _hy_ctx_kernel(x0_ref, x1_ref, v_ref, kern_ref, skip_ref, o_ref, ks_ref, u_ref, *, l):
    u = x1_ref[...] * v_ref[...]
    u_ref[...] = u
    base = jnp.concatenate([kern_ref[l:2 * l, :], kern_ref[0:l, :]], axis=0)
    ks_ref[0] = base
    for b in range(1, 8):
        ks_ref[b] = pltpu.roll(base, b, 0)

    def body(a, acc):
        start = pl.multiple_of(l - 8 * a, 8)
        ublk = u_ref[pl.ds(pl.multiple_of(8 * a, 8), 8), :]
        for b in range(8):
            acc = acc + ks_ref[b, pl.ds(start, l), :] * ublk[b:b + 1, :]
        return acc

    y = lax.fori_loop(0, l // 8, body, jnp.zeros((l, 128), F32))
    o_ref[...] = (x0_ref[...] * (y + u * skip_ref[...])).astype(o_ref.dtype)


def _hy_ctx(p, kern, skip, *, ctx):
    width = skip.shape[1]
    nb = width // 128
    return pl.pallas_call(
        functools.partial(_hy_ctx_kernel, l=ctx),
        name="hy_ctx_conv",
        out_shape=jax.ShapeDtypeStruct((ctx, width), BF16),
        grid=(nb,),
        in_specs=[pl.BlockSpec((ctx, 128), lambda c: (0, c)),
                  pl.BlockSpec((ctx, 128), lambda c: (0, nb + c)),
                  pl.BlockSpec((ctx, 128), lambda c: (0, 2 * nb + c)),
                  pl.BlockSpec((2 * ctx, 128), lambda c: (0, c)),
                  pl.BlockSpec((1, 128), lambda c: (0, c))],
        out_specs=pl.BlockSpec((ctx, 128), lambda c: (0, c)),
        scratch_shapes=[pltpu.VMEM((8, 2 * ctx, 128), F32), pltpu.VMEM((ctx, 128), F32)],
        compiler_params=_cp("parallel"),
    )(p, p, p, kern, skip)


def _hy_prep_kernel(x0_ref, x1_ref, v_ref, u_ref, x0o_ref):
    u_ref[...] = x1_ref[...] * v_ref[...]
    x0o_ref[...] = x0_ref[...]


def _hy_prep(p, *, ctx, width):
    s_lat = p.shape[0] - ctx
    tm = 256
    off = ctx // tm
    return pl.pallas_call(
        _hy_prep_kernel,
        name="hy_prep",
        out_shape=(jax.ShapeDtypeStruct((s_lat, width), F32), jax.ShapeDtypeStruct((s_lat, width), F32)),
        grid=(s_lat // tm,),
        in_specs=[pl.BlockSpec((tm, width), lambda i: (i + off, 0)),
                  pl.BlockSpec((tm, width), lambda i: (i + off, 1)),
                  pl.BlockSpec((tm, width), lambda i: (i + off, 2))],
        out_specs=(pl.BlockSpec((tm, width), lambda i: (i, 0)), pl.BlockSpec((tm, width), lambda i: (i, 0))),
        compiler_params=_cp("parallel"),
    )(p, p, p)


def _dft_consts(n1):
    n2 = FFT_N2
    n = n1 * n2
    f1n = n1 // 2 + 1
    f1p = -(-f1n // 8) * 8
    f1 = np.arange(f1p)[:, None].astype(np.float64)
    live = (f1 < f1n).astype(np.float64)
    t1 = np.arange(n1)[None, :]
    ang = 2 * np.pi * f1 * t1 / n1
    w_re, w_im = np.cos(ang) * live, -np.sin(ang) * live
    t0 = np.arange(n2)
    ang = 2 * np.pi * f1[None, :, :] * t0[:, None, None] / n
    tw_fwd = np.concatenate([np.cos(ang), -np.sin(ang)], axis=-1)
    ang = 2 * np.pi * f1[:, :, None] * t0[None, :, None] / n
    tw_inv = np.concatenate([np.cos(ang), np.sin(ang)], axis=-1)
    ang = 2 * np.pi * np.outer(t0, t0) / n2
    d_re, d_im = np.cos(ang), -np.sin(ang)
    wgt = np.where((f1 == 0) | (f1 == n1 // 2), 1.0, 2.0) * live
    t1o = np.arange(n1 // 2)[:, None]
    ang = 2 * np.pi * t1o * f1.T / n1
    c_m, s_m = np.cos(ang) * wgt.T / n, -np.sin(ang) * wgt.T / n
    f = lambda a: jnp.asarray(a, F32)
    return dict(f1p=f1p, w_re=f(w_re), w_im=f(w_im), tw_fwd=f(tw_fwd), tw_inv=f(tw_inv),
                d_re=f(d_re), d_im=f(d_im), c_m=f(c_m), s_m=f(s_m))


def _fft1_kernel(u_ref, wre_ref, wim_ref, tw_ref, bre_ref, bim_ref, *, tb, c):
    wre = wre_ref[...]
    wim = wim_ref[...]
    for b in range(tb):
        sl = slice(b * c, (b + 1) * c)
        u = u_ref[:, sl]
        are = _dot(wre, u, HI)
        aim = _dot(wim, u, HI)
        tr = tw_ref[b, :, 0:1]
        ti = tw_ref[b, :, 1:2]
        bre_ref[:, sl] = are * tr - aim * ti
        bim_ref[:, sl] = are * ti + aim * tr


def _fft1(u2, w_re, w_im, tw_fwd, *, c):
    t1_in = u2.shape[0]
    f1p = w_re.shape[0]
    tb = 4
    out = jax.ShapeDtypeStruct((f1p, FFT_N2 * c), F32)
    return pl.pallas_call(
        functools.partial(_fft1_kernel, tb=tb, c=c),
        name="fft_outer",
        out_shape=(out, out),
        grid=(FFT_N2 // tb,),
        in_specs=[pl.BlockSpec((t1_in, tb * c), lambda i: (0, i)),
                  pl.BlockSpec((f1p, t1_in), lambda i: (0, 0)),
                  pl.BlockSpec((f1p, t1_in), lambda i: (0, 0)),
                  pl.BlockSpec((tb, f1p, 2), lambda i: (i, 0, 0))],
        out_specs=(pl.BlockSpec((f1p, tb * c), lambda i: (0, i)), pl.BlockSpec((f1p, tb * c), lambda i: (0, i))),
        compiler_params=_cp("parallel"),
    )(u2, w_re[:, :t1_in], w_im[:, :t1_in], tw_fwd)


def _fft_spec_kernel(bre_ref, bim_ref, dre_ref, dim_ref, xre_ref, xim_ref):
    br, bi = bre_ref[0], bim_ref[0]
    dr, di = dre_ref[...], dim_ref[...]
    xre_ref[0] = _dot(dr, br, HI) - _dot(di, bi, HI)
    xim_ref[0] = _dot(dr, bi, HI) + _dot(di, br, HI)


def _fft_spectrum(b_re, b_im, d_re, d_im):
    f1p, n2, c = b_re.shape
    blk = pl.BlockSpec((1, n2, c), lambda i: (i, 0, 0))
    mat = pl.BlockSpec((n2, n2), lambda i: (0, 0))
    out = jax.ShapeDtypeStruct((f1p, n2, c), F32)
    return pl.pallas_call(
        _fft_spec_kernel, name="fft_spectrum", out_shape=(out, out), grid=(f1p,),
        in_specs=[blk, blk, mat, mat], out_specs=(blk, blk),
        compiler_params=_cp("parallel"),
    )(b_re, b_im, d_re, d_im)


def _fft_mid_kernel(bre_ref, bim_ref, kre_ref, kim_ref, dre_ref, dim_ref, tw_ref, ore_ref, oim_ref):
    br, bi = bre_ref[0], bim_ref[0]
    dr, di = dre_ref[...], dim_ref[...]
    xr = _dot(dr, br, HI) - _dot(di, bi, HI)
    xi = _dot(dr, bi, HI) + _dot(di, br, HI)
    kr, ki = kre_ref[0], kim_ref[0]
    yr = xr * kr - xi * ki
    yi = xr * ki + xi * kr
    zr = _dot(dr, yr, HI) + _dot(di, yi, HI)
    zi = _dot(dr, yi, HI) - _dot(di, yr, HI)
    tr = tw_ref[0, :, 0:1]
    ti = tw_ref[0, :, 1:2]
    ore_ref[0] = zr * tr - zi * ti
    oim_ref[0] = zr * ti + zi * tr


def _fft_mid(b_re, b_im, k_re, k_im, d_re, d_im, tw_inv):
    f1p, n2, c = b_re.shape
    blk = pl.BlockSpec((1, n2, c), lambda i: (i, 0, 0))
    mat = pl.BlockSpec((n2, n2), lambda i: (0, 0))
    out = jax.ShapeDtypeStruct((f1p, n2, c), F32)
    return pl.pallas_call(
        _fft_mid_kernel, name="fft_mid", out_shape=(out, out), grid=(f1p,),
        in_specs=[blk, blk, blk, blk, mat, mat, pl.BlockSpec((1, n2, 2), lambda i: (i, 0, 0))],
        out_specs=(blk, blk),
        compiler_params=_cp("parallel"),
    )(b_re, b_im, k_re, k_im, d_re, d_im, tw_inv)


def _fft_fin_kernel(bre_ref, bim_ref, cm_ref, sm_ref, x0_ref, u_ref, skip_ref, o_ref, *, tb, c):
    y = _dot(cm_ref[...], bre_ref[...], HI) + _dot(sm_ref[...], bim_ref[...], HI)
    skip = skip_ref[...]
    for b in range(tb):
        sl = slice(b * c, (b + 1) * c)
        o_ref[:, sl] = (x0_ref[:, sl] * (y[:, sl] + u_ref[:, sl] * skip)).astype(o_ref.dtype)


def _fft_fin(z_re, z_im, c_m, s_m, x0_2, u2, skip, *, c):
    f1p = z_re.shape[0]
    t1o = c_m.shape[0]
    tb = 4
    big = pl.BlockSpec((f1p, tb * c), lambda i: (0, i))
    sig = pl.BlockSpec((t1o, tb * c), lambda i: (0, i))
    mat = pl.BlockSpec((t1o, f1p), lambda i: (0, 0))
    return pl.pallas_call(
        functools.partial(_fft_fin_kernel, tb=tb, c=c),
        name="fft_final",
        out_shape=jax.ShapeDtypeStruct((t1o, FFT_N2 * c), BF16),
        grid=(FFT_N2 // tb,),
        in_specs=[big, big, mat, mat, sig, sig, pl.BlockSpec((1, c), lambda i: (0, 0))],
        out_specs=sig,
        compiler_params=_cp("parallel"),
    )(z_re, z_im, c_m, s_m, x0_2, u2, skip)


def _hyena_latent(p, kern, skip, *, ctx):
    c = skip.shape[1]
    s_lat = p.shape[0] - ctx
    n1 = 2 * s_lat // FFT_N2
    k = _dft_consts(n1)
    f1p = k["f1p"]
    u, x0 = _hy_prep(p, ctx=ctx, width=c)
    u2 = u.reshape(n1 // 2, FFT_N2 * c)
    x0_2 = x0.reshape(n1 // 2, FFT_N2 * c)
    kb_re, kb_im = _fft1(kern.reshape(n1, FFT_N2 * c), k["w_re"], k["w_im"], k["tw_fwd"], c=c)
    kf_re, kf_im = _fft_spectrum(kb_re.reshape(f1p, FFT_N2, c), kb_im.reshape(f1p, FFT_N2, c), k["d_re"], k["d_im"])
    b_re, b_im = _fft1(u2, k["w_re"], k["w_im"], k["tw_fwd"], c=c)
    z_re, z_im = _fft_mid(b_re.reshape(f1p, FFT_N2, c), b_im.reshape(f1p, FFT_N2, c), kf_re, kf_im,
                          k["d_re"], k["d_im"], k["tw_inv"])
    y2 = _fft_fin(z_re.reshape(f1p, FFT_N2 * c), z_im.reshape(f1p, FFT_N2 * c), k["c_m"], k["s_m"],
                  x0_2, u2, skip, c=c)
    return y2.reshape(s_lat, c)


def _softplus(x):
    return jnp.maximum(x, 0.0) + jnp.log(1.0 + jnp.exp(-jnp.abs(x)))


def _split_bf16(a):
    hi = a.astype(BF16)
    return hi, (a - hi.astype(F32)).astype(BF16)


def _dotp(a, b, dims=(((1,), (0,)), ((), ()))):
    f = lambda x, y: lax.dot_general(x, y, dims, preferred_element_type=F32)
    if DN_PASSES == 1:
        return f(a.astype(BF16), b.astype(BF16))
    ah, al = _split_bf16(a)
    bh, bl = _split_bf16(b)
    return f(ah, bh) + (f(al, bh) + f(ah, bl))


def _dn_intra_kernel(q_ref, k_ref, v_ref, sm_ref, cst_ref, u_ref, w_ref, qd_ref, kd_ref, qk_ref, last_ref):
    c = DN_CHUNK
    nh = q_ref.shape[1] // DN_HEAD
    r = nh * c
    row = lax.broadcasted_iota(jnp.int32, (c, c), 0)
    col = lax.broadcasted_iota(jnp.int32, (c, c), 1)
    sm = sm_ref[...]
    beta_all = jax.nn.sigmoid(sm)
    g_all = -jnp.exp(cst_ref[0:1, :]) * _softplus(sm + cst_ref[1:2, :])
    tot_all = jnp.sum(g_all, axis=0, keepdims=True)
    gc_all = (_dot((row >= col).astype(F32), g_all, HI), _dot((row <= col).astype(F32), g_all, HI))
    zpad = jnp.zeros((128 - c, 128), F32)
    gct_all = tuple(jnp.concatenate([g, zpad], axis=0).T for g in gc_all)

    qs, ks, vs = [], [], []
    for h in range(nh):
        hs = slice(h * DN_HEAD, (h + 1) * DN_HEAD)
        qh = _silu(q_ref[:, hs])
        kh = _silu(k_ref[:, hs])
        qs.append(qh * lax.rsqrt(jnp.sum(qh * qh, axis=-1, keepdims=True) + EPS) * (DN_HEAD ** -0.5))
        ks.append(kh * lax.rsqrt(jnp.sum(kh * kh, axis=-1, keepdims=True) + EPS))
        vs.append(_silu(v_ref[:, hs]))
    q_rows = jnp.concatenate(qs, axis=0)
    k_rows = jnp.concatenate(ks, axis=0)
    v_rows = jnp.concatenate(vs, axis=0)
    qkt = _dotp(q_rows, k_rows, NT)

    rr = lax.broadcasted_iota(jnp.int32, (r, r), 0)
    cc = lax.broadcasted_iota(jnp.int32, (r, r), 1)
    sh = int(math.log2(c))
    same = jnp.right_shift(rr, sh) == jnp.right_shift(cc, sh)
    incl = (jnp.logical_and(same, rr >= cc), jnp.logical_and(same, rr <= cc))
    strict = (jnp.logical_and(same, rr > cc), jnp.logical_and(same, rr < cc))

    gamma, egc, kdec, pw, sol = [], [], [], [], []
    for d in range(2):
        bis = [d * nh + h for h in range(nh)]
        gis = [2 * nh + b for b in bis]
        beta = jnp.concatenate([beta_all[:, b:b + 1] for b in bis], axis=0)
        gcol = jnp.concatenate([gc_all[d][:, g:g + 1] for g in gis], axis=0)
        grow = jnp.concatenate([gct_all[d][g:g + 1, :c] for g in gis], axis=1)
        gtot = jnp.concatenate([jnp.broadcast_to(tot_all[:, g:g + 1], (c, 1)) for g in gis], axis=0)
        gamma.append(jnp.where(incl[d], jnp.exp(jnp.where(incl[d], gcol - grow, 0.0)), 0.0))
        egc.append(jnp.exp(gcol))
        kdec.append(jnp.exp(gtot - gcol))
        kb = k_rows * beta
        pw.append(-jnp.where(strict[d], _dotp(kb, k_rows, NT) * gamma[d], 0.0))
        sol.append(jnp.concatenate([v_rows * beta, kb * egc[d]], axis=1))
    steps = int(math.log2(c))
    for t in range(steps):
        for d in range(2):
            sol[d] = sol[d] + _dotp(pw[d], sol[d])
        if t + 1 < steps:
            for d in range(2):
                pw[d] = _dotp(pw[d], pw[d])
    for d in range(2):
        qdec = q_rows * egc[d]
        kd = k_rows * kdec[d]
        qk = qkt * gamma[d]
        for h in range(nh):
            hs = slice(h * DN_HEAD, (h + 1) * DN_HEAD)
            rs = slice(h * c, (h + 1) * c)
            gi = 2 * nh + d * nh + h
            u_ref[d, :, hs] = sol[d][rs, :DN_HEAD]
            w_ref[d, :, hs] = sol[d][rs, DN_HEAD:]
            qd_ref[d, :, hs] = qdec[rs]
            kd_ref[d, :, hs] = kd[rs]
            qk_ref[d, :, rs] = qk[rs, rs]
            last_ref[d, 0, :, hs] = jnp.broadcast_to(jnp.exp(tot_all[:, gi:gi + 1]), (8, DN_HEAD))


def _dn_intra(p, cst):
    s_tot = p.shape[0]
    c = DN_CHUNK
    w = 512
    nch = s_tot // c
    big = jax.ShapeDtypeStruct((2, s_tot, w), F32)
    bspec = pl.BlockSpec((2, c, w), lambda j: (0, j, 0))
    return pl.pallas_call(
        _dn_intra_kernel,
        name="dn_intra",
        out_shape=(big, big, big, big, jax.ShapeDtypeStruct((2, s_tot, w // 2), F32),
                   jax.ShapeDtypeStruct((2, nch, 8, w), F32)),
        grid=(nch,),
        in_specs=[pl.BlockSpec((c, w), lambda j: (j, 3)),
                  pl.BlockSpec((c, w), lambda j: (j, 4)),
                  pl.BlockSpec((c, w), lambda j: (j, 5)),
                  pl.BlockSpec((c, 128), lambda j: (j, 46)),
                  pl.BlockSpec((8, 128), lambda j: (0, 0))],
        out_specs=(bspec, bspec, bspec, bspec, pl.BlockSpec((2, c, w // 2), lambda j: (0, j, 0)),
                   pl.BlockSpec((2, 1, 8, w), lambda j: (0, j, 0, 0))),
        compiler_params=_cp("parallel"),
    )(p, p, p, p, cst)


def _dn_rec_kernel(*refs, nh):
    ins, (of_ref, ob_ref, s_ref) = refs[:12], refs[12:]
    o_refs = (of_ref, ob_ref)

    @pl.when(pl.program_id(0) == 0)
    def _():
        s_ref[...] = jnp.zeros_like(s_ref)

    c = DN_CHUNK
    chains = [(d, h) for d in range(2) for h in range(nh)]
    hs = lambda h: slice(h * DN_HEAD, (h + 1) * DN_HEAD)
    s = [s_ref[d * nh + h] for d, h in chains]
    v_new = [ins[6 * d][0, :, hs(h)] - _dotp(ins[6 * d + 1][0, :, hs(h)], s[i]) for i, (d, h) in enumerate(chains)]
    qs = [_dotp(ins[6 * d + 2][0, :, hs(h)], s[i]) for i, (d, h) in enumerate(chains)]
    for i, (d, h) in enumerate(chains):
        o_refs[d][:, hs(h)] = qs[i] + _dotp(ins[6 * d + 4][0, :, h * c:(h + 1) * c], v_new[i])
    for i, (d, h) in enumerate(chains):
        s_ref[d * nh + h] = s[i] * ins[6 * d + 5][0, 0, 0:1, hs(h)] + _dotp(ins[6 * d + 3][0, :, hs(h)], v_new[i], TN)


def _dn_recurrence(u, w, qd, kd, qk, last, *, ctx):
    _, s_tot, width = u.shape
    c = DN_CHUNK
    nch = s_tot // c
    nctx = ctx // c
    nh = width // DN_HEAD
    fwd = lambda j: j
    bwd = lambda j: jnp.where(j < nctx, nctx - 1 - j, nch - 1 - (j - nctx))
    specs = []
    for d, cm in ((0, fwd), (1, bwd)):
        blk = lambda ww, cm=cm, d=d: pl.BlockSpec((1, c, ww), lambda j: (d, cm(j), 0))
        specs += [blk(width), blk(width), blk(width), blk(width), blk(width // 2),
                  pl.BlockSpec((1, 1, 8, width), lambda j, cm=cm, d=d: (d, cm(j), 0, 0))]
    out = jax.ShapeDtypeStruct((s_tot, width), F32)
    return pl.pallas_call(
        functools.partial(_dn_rec_kernel, nh=nh),
        name="dn_scan",
        out_shape=(out, out),
        grid=(nch,),
        in_specs=specs,
        out_specs=(pl.BlockSpec((c, width), lambda j: (fwd(j), 0)), pl.BlockSpec((c, width), lambda j: (bwd(j), 0))),
        scratch_shapes=[pltpu.VMEM((2 * nh, DN_HEAD, DN_HEAD), F32)],
        compiler_params=_cp("arbitrary"),
    )(u, w, qd, kd, qk, last, u, w, qd, kd, qk, last)


def _dn_finish_kernel(of_ref, ob_ref, gate_ref, gain_ref, o_ref):
    nh = of_ref.shape[1] // DN_HEAD
    for h in range(nh):
        hs = slice(h * DN_HEAD, (h + 1) * DN_HEAD)
        o = of_ref[:, hs] + ob_ref[:, hs]
        y = o * lax.rsqrt(jnp.mean(o * o, axis=-1, keepdims=True) + EPS) * gain_ref[...]
        o_ref[:, hs] = (y * _silu(gate_ref[:, hs])).astype(o_ref.dtype)


def _dn_finish(o_f, o_b, p, gain):
    s_tot, width = o_f.shape
    tm = _row_tile(s_tot, (528, 640, 256, 128))
    blk = pl.BlockSpec((tm, width), lambda i: (i, 0))
    return pl.pallas_call(
        _dn_finish_kernel,
        name="dn_finish",
        out_shape=jax.ShapeDtypeStruct((s_tot, width), BF16),
        grid=(s_tot // tm,),
        in_specs=[blk, blk, pl.BlockSpec((tm, width), lambda i: (i, 6)), pl.BlockSpec((1, DN_HEAD), lambda i: (0, 0))],
        out_specs=blk,
        compiler_params=_cp("parallel"),
    )(o_f, o_b, p, gain)


def _rope_tables(ctx, s_lat):
    rows = s_lat // GRID_W
    n_freq = HEAD_D // 4
    row = np.repeat(np.arange(rows, dtype=np.float32), GRID_W)
    col = np.tile(np.arange(GRID_W, dtype=np.float32), rows)
    inv = (ROPE_THETA ** (-np.arange(n_freq, dtype=np.float32) / n_freq)).astype(np.float32)
    ang = np.concatenate([row[:, None] * inv, col[:, None] * inv], axis=-1).astype(np.float32)
    cos, sin = np.cos(ang), np.sin(ang)
    cos = np.concatenate([np.ones((ctx, HEAD_D // 2), np.float32), cos], axis=0)
    sin = np.concatenate([np.zeros((ctx, HEAD_D // 2), np.float32), sin], axis=0)
    cos_t = np.tile(np.concatenate([cos, cos], axis=1), (1, 2))
    sin_t = np.tile(np.concatenate([-sin, sin], axis=1), (1, 2))
    return jnp.asarray(cos_t, F32), jnp.asarray(sin_t, F32)


def _pad_rows(a, rows):
    return jnp.concatenate([a, jnp.zeros((rows - a.shape[0],) + a.shape[1:], a.dtype)], axis=0)


def kernel(x, c, ctx, c_ctx, w_ada, b_ada, norm_mix_pre, norm_mix_post, norm_ffn_pre, norm_ffn_post, w_in, w_out, attn_q_norm, attn_k_norm, hy_short, hy_w1, hy_b1, hy_w2, hy_b2, hy_w3, hy_b3, hy_w4, hy_freq, hy_skip, dn_short, dn_a_log, dn_dt_bias, dn_norm, df_lambda, df_norm, ffn_up, ffn_conv, ffn_down):
    batch, s_lat, d = x.shape
    n_ctx = ctx.shape[1]
    depth = w_in.shape[0]
    gw = d // 4
    assert batch == 1 and gw == 512 and n_ctx % 256 == 0 and s_lat % 512 == 0
    s_tot = n_ctx + s_lat

    xx = jnp.concatenate([ctx[0], x[0]], axis=0)
    c_rows = _pad_rows(jnp.concatenate([c, c_ctx[None, :]], axis=0), 8)
    mods = _mod_vectors(c_rows, w_ada, b_ada)

    cos_t, sin_t = _rope_tables(n_ctx, s_lat)
    lane = np.arange(128)
    gmat = jnp.asarray((lane[:, None] // HEAD_D == lane[None, :] // HEAD_D) / HEAD_D, F32)
    ident = jnp.asarray([[0.0], [1.0], [0.0]], F32)

    a_cols = gw + 2 * 2 * HEAD_D
    o_hy, o_dn, o_df = a_cols, a_cols + 3 * gw, a_cols + 3 * gw + 4 * gw + 16
    order = [(o_hy, o_hy + 3 * gw), (o_dn, o_dn + 4 * gw), (o_df, o_df + 3 * gw), (0, a_cols),
             (o_dn + 4 * gw, o_dn + 4 * gw + 16)]
    n_proj = 12 * gw
    used = sum(b - a for a, b in order)

    for i in range(depth):
        lam_init = 0.8 - 0.6 * math.exp(-0.3 * i)
        mod = mods[i]
        sh1, sc1, gt1, sh2, sc2, gt2 = (mod[:, k * d:(k + 1) * d] for k in range(6))
        mod_mix = _pad_rows(jnp.concatenate([sh1[0:1], sc1[0:1], sh1[1:2], sc1[1:2]], axis=0), 8)
        mod_ffn = _pad_rows(jnp.concatenate([sh2[0:1], sc2[0:1], sh2[1:2], sc2[1:2]], axis=0), 8)

        w_in_p = jnp.concatenate([w_in[i][:, a:b] for a, b in order]
                                 + [jnp.zeros((d, n_proj - used), F32)], axis=1).astype(BF16)
        cw_in = jnp.concatenate([hy_short[i], dn_short[i], jnp.broadcast_to(ident, (3, n_proj - 6 * gw))], axis=1)
        p = _norm_mod_matmul_conv(xx, norm_mix_pre[i][None, :], mod_mix, w_in_p, _pad_rows(cw_in, 8),
                                  ctx=n_ctx, ffn=False, conv_cols=6 * gw)

        qg, kg, vgt, qd, kd, vdt = _attn_prep(p, jnp.tile(attn_q_norm[i], 2)[None, :],
                                              jnp.tile(attn_k_norm[i], 2)[None, :], cos_t, sin_t, gmat)
        ya = _gqa_attention(qg, kg, vgt, ctx=n_ctx)
        yd = _diff_attention(qd, kd, vdt, df_lambda[i], df_norm[i][None, :], ctx=n_ctx, lam_init=lam_init)

        filt = (_pad_rows(hy_w1[i], 128), hy_b1[i][None, :], hy_w2[i], hy_b2[i][None, :], hy_w3[i],
                hy_b3[i][None, :], hy_w4[i], hy_freq[i][None, :])
        skip = hy_skip[i][None, :]
        yb_lat = _hyena_latent(p, _hy_filter(filt, s_lat, 2 * s_lat), skip, ctx=n_ctx)
        if i < depth - 1:
            yb_ctx = _hy_ctx(p, _hy_filter(filt, n_ctx, 2 * n_ctx), skip, ctx=n_ctx)
        else:
            yb_ctx = jnp.zeros((n_ctx, gw), BF16)
        yb = jnp.concatenate([yb_ctx, yb_lat], axis=0)

        zeros8 = jnp.zeros((8,), F32)
        cst = _pad_rows(jnp.stack([jnp.concatenate([zeros8, dn_a_log[i].reshape(-1), jnp.zeros((112,), F32)]),
                                   jnp.concatenate([zeros8, dn_dt_bias[i].reshape(-1), jnp.zeros((112,), F32)])]), 8)
        o_f, o_b = _dn_recurrence(*_dn_intra(p, cst), ctx=n_ctx)
        yc = _dn_finish(o_f, o_b, p, dn_norm[i][None, :])

        y_cat = jnp.concatenate([ya, yb, yc, yd], axis=1)
        xx = _matmul_norm_residual(y_cat, w_out[i].astype(BF16), xx, gt1, norm_mix_post[i][None, :], ctx=n_ctx)

        g = _norm_mod_matmul_conv(xx, norm_ffn_pre[i][None, :], mod_ffn, ffn_up[i].astype(BF16),
                                  _pad_rows(ffn_conv[i], 8), ctx=n_ctx, ffn=True, conv_cols=ffn_down.shape[1])
        xx = _matmul_norm_residual(g, ffn_down[i].astype(BF16), xx, gt2, norm_ffn_post[i][None, :], ctx=n_ctx)

    return xx[n_ctx:][None]
```

```python
import functools
import math

import jax
import jax.numpy as jnp
import numpy as np
from jax import lax
from jax.experimental import pallas as pl
from jax.experimental.pallas import tpu as pltpu

F32 = jnp.float32
BF16 = jnp.bfloat16
HI = lax.Precision.HIGHEST
EPS = 1e-6

GRID_W = 64
ROPE_THETA = 10000.0
HEAD_D = 64
GQA_GROUP = 4
DN_HEAD = 128
DN_CHUNK = 64
DN_CHAIN_PASSES = 3
HY_BANDS = 16
HY_FAST_DECAY, HY_SLOW_DECAY, HY_TARGET = 0.3, 1.5, 1e-2
FFT_N2 = 128
ATT_TK = 256
DF_HPS = 2
LOG2E = 1.4426950408889634
VMEM_LIMIT = 56 * 1024 * 1024
NT = (((1,), (1,)), ((), ()))
TN = (((0,), (0,)), ((), ()))


def _cp(*sem):
    return pltpu.CompilerParams(dimension_semantics=sem, vmem_limit_bytes=VMEM_LIMIT)


def _dot(a, b, precision=None):
    return jnp.dot(a, b, precision=precision, preferred_element_type=F32)


def _split_bf16(a):
    hi = a.astype(BF16)
    return hi, (a - hi.astype(F32)).astype(BF16)


def _mm3(a, b):
    return _dot(a[0], b[0]) + (_dot(a[1], b[0]) + _dot(a[0], b[1]))


def _silu(x):
    return x * jax.nn.sigmoid(x)


def _row_tile(n, cands):
    for t in cands:
        if n % t == 0:
            return t
    raise ValueError(f"no row tile for {n}")


def _mod_kernel(c_ref, w_ref, b_ref, o_ref):
    o_ref[0] = _dot(_silu(c_ref[...]), w_ref[0], HI) + b_ref[0]


def _mod_vectors(c_rows, w_ada, b_ada):
    depth, d, n = w_ada.shape
    tn = 512
    return pl.pallas_call(
        _mod_kernel,
        name="mod_vectors",
        out_shape=jax.ShapeDtypeStruct((depth, 8, n), F32),
        grid=(depth, n // tn),
        in_specs=[pl.BlockSpec((8, d), lambda l, j: (0, 0)),
                  pl.BlockSpec((1, d, tn), lambda l, j: (l, 0, j)),
                  pl.BlockSpec((1, 1, tn), lambda l, j: (l, 0, j))],
        out_specs=pl.BlockSpec((1, 8, tn), lambda l, j: (l, 0, j)),
        compiler_params=_cp("parallel", "parallel"),
    )(c_rows, w_ada, b_ada.reshape(depth, 1, n))


HALO = 16


def _k1_kernel(xp_ref, xm_ref, xn_ref, g_ref, mod_ref, *rest, tm, ctx, s_tot, ffn, n_conv):
    if ffn:
        wa_ref, wb_ref, cwa_ref, cwb_ref, o_ref, h_ref, acc_ref = rest
    else:
        wa_ref, cwa_ref, o_ref, h_ref, acc_ref = rest
    i = pl.program_id(0)
    j = pl.program_id(1)

    @pl.when(j == 0)
    def _():
        def nm(xv, row0):
            ms = jnp.mean(xv * xv, axis=-1, keepdims=True)
            y = xv * lax.rsqrt(ms + EPS) * g_ref[...]
            rows = row0 + lax.broadcasted_iota(jnp.int32, (xv.shape[0], 1), 0)
            is_ctx = rows < ctx
            sh = jnp.where(is_ctx, mod_ref[2:3, :], mod_ref[0:1, :])
            sc = jnp.where(is_ctx, mod_ref[3:4, :], mod_ref[1:2, :])
            return (y * (1.0 + sc) + sh).astype(BF16)

        h_ref[0:HALO] = nm(xp_ref[...], i * tm - HALO)
        n_sub = 3 if (tm > 640 and tm % 48 == 0) else 1
        rc = tm // n_sub
        for r0 in range(0, tm, rc):
            h_ref[HALO + r0:HALO + r0 + rc] = nm(xm_ref[r0:r0 + rc], i * tm + r0)
        h_ref[HALO + tm:] = nm(xn_ref[...], i * tm + tm)

    def conv(w_ref, cw_ref, edge):
        acc_ref[...] = _dot(h_ref[...], w_ref[...])
        prev = acc_ref[HALO - 1:HALO - 1 + tm]
        nxt = acc_ref[HALO + 1:HALO + 1 + tm]
        if edge:
            rows = i * tm + lax.broadcasted_iota(jnp.int32, (tm, 1), 0)
            prev = jnp.where(jnp.logical_and(rows != 0, rows != ctx), prev, 0.0)
            nxt = jnp.where(jnp.logical_and(rows != ctx - 1, rows != s_tot - 1), nxt, 0.0)
        cw = cw_ref[...]
        return acc_ref[HALO:HALO + tm] * cw[1:2] + prev * cw[0:1] + nxt * cw[2:3]

    def emit(edge):
        if ffn:
            a = conv(wa_ref, cwa_ref, edge)
            b = conv(wb_ref, cwb_ref, edge)
            o_ref[...] = (_silu(a) * b).astype(o_ref.dtype)
        else:
            o_ref[...] = conv(wa_ref, cwa_ref, edge).astype(o_ref.dtype)

    edge_tiles = sorted({r // tm for r in (0, ctx - 1, ctx, s_tot - 1)})
    is_edge = functools.reduce(jnp.logical_or, [i == e for e in edge_tiles])
    is_conv = j < n_conv
    pl.when(jnp.logical_and(is_conv, is_edge))(lambda: emit(True))
    pl.when(jnp.logical_and(is_conv, jnp.logical_not(is_edge)))(lambda: emit(False))

    if not ffn:
        @pl.when(jnp.logical_not(is_conv))
        def _():
            o_ref[...] = _dot(h_ref[HALO:HALO + tm], wa_ref[...]).astype(o_ref.dtype)


def _norm_mod_matmul_conv(xx, gain, mod4, w, cw, *, ctx, ffn, conv_cols):
    s_tot, d = xx.shape
    n = w.shape[1]
    tm = _row_tile(s_tot, (1056, 640, 256, 128))
    tn = 512
    nb = s_tot // HALO
    n_out = n // 2 if ffn else n
    x_specs = [
        pl.BlockSpec((HALO, d), lambda i, j: (jnp.maximum(i * (tm // HALO) - 1, 0), 0)),
        pl.BlockSpec((tm, d), lambda i, j: (i, 0)),
        pl.BlockSpec((HALO, d), lambda i, j: (jnp.minimum((i + 1) * (tm // HALO), nb - 1), 0)),
        pl.BlockSpec((1, d), lambda i, j: (0, 0)),
        pl.BlockSpec((8, d), lambda i, j: (0, 0)),
    ]
    if ffn:
        off = n_out // tn
        w_specs = [pl.BlockSpec((d, tn), lambda i, j: (0, j)),
                   pl.BlockSpec((d, tn), lambda i, j: (0, j + off)),
                   pl.BlockSpec((8, tn), lambda i, j: (0, j)),
                   pl.BlockSpec((8, tn), lambda i, j: (0, j + off))]
        args = (w, w, cw, cw)
        out_dtype = BF16
    else:
        w_specs = [pl.BlockSpec((d, tn), lambda i, j: (0, j)),
                   pl.BlockSpec((8, tn), lambda i, j: (0, j))]
        args = (w, cw)
        out_dtype = F32
    return pl.pallas_call(
        functools.partial(_k1_kernel, tm=tm, ctx=ctx, s_tot=s_tot, ffn=ffn, n_conv=conv_cols // tn),
        name="ffn_up_conv_gate" if ffn else "in_proj_conv",
        out_shape=jax.ShapeDtypeStruct((s_tot, n_out), out_dtype),
        grid=(s_tot // tm, n_out // tn),
        in_specs=x_specs + w_specs,
        out_specs=pl.BlockSpec((tm, tn), lambda i, j: (i, j)),
        scratch_shapes=[pltpu.VMEM((tm + 2 * HALO, d), BF16), pltpu.VMEM((tm + 2 * HALO, tn), F32)],
        compiler_params=_cp("parallel", "arbitrary"),
    )(xx, xx, xx, gain, mod4, *args)


def _k2_kernel(a_ref, w_ref, x_ref, gate_ref, gain_ref, o_ref, *, tm, ctx, nk):
    i = pl.program_id(0)
    k = pl.program_id(1)

    part = _dot(a_ref[...], w_ref[...])

    def finish(y):
        r = y * lax.rsqrt(jnp.mean(y * y, axis=-1, keepdims=True) + EPS) * gain_ref[...]
        rows = i * tm + lax.broadcasted_iota(jnp.int32, (tm, 1), 0)
        gt = jnp.where(rows < ctx, gate_ref[1:2, :], gate_ref[0:1, :])
        o_ref[...] = x_ref[...] + gt * r

    if nk == 1:
        finish(part)
        return

    @pl.when(k == 0)
    def _():
        o_ref[...] = part

    @pl.when(jnp.logical_and(k > 0, k < nk - 1))
    def _():
        o_ref[...] += part

    @pl.when(k == nk - 1)
    def _():
        finish(o_ref[...] + part)


def _matmul_norm_residual(a, w, xx, gate2, gain, *, ctx):
    s_tot, kdim = a.shape
    d = w.shape[1]
    tm = _row_tile(s_tot, (704, 640, 256, 128))
    nk = 1 if kdim <= 2048 else 4
    tk = kdim // nk
    assert tk % 128 == 0
    return pl.pallas_call(
        functools.partial(_k2_kernel, tm=tm, ctx=ctx, nk=nk),
        name="matmul_norm_residual",
        out_shape=jax.ShapeDtypeStruct((s_tot, d), F32),
        grid=(s_tot // tm, nk),
        in_specs=[pl.BlockSpec((tm, tk), lambda i, k: (i, k)),
                  pl.BlockSpec((tk, d), lambda i, k: (k, 0)),
                  pl.BlockSpec((tm, d), lambda i, k: (i, 0)),
                  pl.BlockSpec((8, d), lambda i, k: (0, 0)),
                  pl.BlockSpec((1, d), lambda i, k: (0, 0))],
        out_specs=pl.BlockSpec((tm, d), lambda i, k: (i, 0)),
        compiler_params=_cp("parallel", "arbitrary"),
    )(a, w, xx, gate2, gain)


def _attn_prep_kernel(aq_ref, akv_ref, dq_ref, dk_ref, dv_ref, qgain_ref, kgain_ref, cos_ref, sin_ref,
                      gmat_ref, qg_ref, kg_ref, vgt_ref, qd_ref, kd_ref, vdt_ref):
    cosf = cos_ref[...]
    sins = sin_ref[...]
    lane = lax.broadcasted_iota(jnp.int32, (1, 128), 1)
    first_half = (lane & (HEAD_D - 1)) < (HEAD_D // 2)
    scale = HEAD_D ** -0.5 * LOG2E

    def rope(x):
        partner = jnp.where(first_half, pltpu.roll(x, 128 - HEAD_D // 2, 1), pltpu.roll(x, HEAD_D // 2, 1))
        return x * cosf + partner * sins

    def headnorm(x, gain):
        ms = _dot(x * x, gmat_ref[...], HI)
        return x * lax.rsqrt(ms + EPS) * gain

    def put(ref, pair, val):
        vb = val.astype(ref.dtype)
        ref[2 * pair] = vb[:, :HEAD_D]
        ref[2 * pair + 1] = vb[:, HEAD_D:]

    for ci in range(4):
        sl = slice(ci * 128, (ci + 1) * 128)
        put(qg_ref, ci, rope(headnorm(aq_ref[:, sl], qgain_ref[...])) * scale)
        put(qd_ref, ci, rope(dq_ref[:, sl]) * scale)
        put(kd_ref, ci, rope(dk_ref[:, sl]))
        vdt_ref[ci, 0] = dv_ref[:, sl].T.astype(vdt_ref.dtype)
    put(kg_ref, 0, rope(headnorm(akv_ref[:, 0:128], kgain_ref[...])))
    vt = akv_ref[:, 128:256].T.astype(vgt_ref.dtype)
    vgt_ref[0, 0] = vt[:HEAD_D]
    vgt_ref[1, 0] = vt[HEAD_D:]


def _attn_prep(p, qgain, kgain, cos_t, sin_t, gmat):
    s_tot = p.shape[0]
    tm = ATT_TK
    hm = lambda nh, dd: jax.ShapeDtypeStruct((nh, s_tot, dd), BF16)
    hspec = lambda nh, dd: pl.BlockSpec((nh, tm, dd), lambda i: (0, i, 0))
    vt = lambda nh, dd: jax.ShapeDtypeStruct((nh, s_tot // tm, dd, tm), BF16)
    vtspec = lambda nh, dd: pl.BlockSpec((nh, 1, dd, tm), lambda i: (0, i, 0, 0))
    return pl.pallas_call(
        _attn_prep_kernel,
        name="attn_prep",
        out_shape=(hm(8, 64), hm(2, 64), vt(2, 64), hm(8, 64), hm(8, 64), vt(4, 128)),
        grid=(s_tot // tm,),
        in_specs=[pl.BlockSpec((tm, 512), lambda i: (i, 10)),
                  pl.BlockSpec((tm, 256), lambda i: (i, 22)),
                  pl.BlockSpec((tm, 512), lambda i: (i, 7)),
                  pl.BlockSpec((tm, 512), lambda i: (i, 8)),
                  pl.BlockSpec((tm, 512), lambda i: (i, 9)),
                  pl.BlockSpec((1, 128), lambda i: (0, 0)),
                  pl.BlockSpec((1, 128), lambda i: (0, 0)),
                  pl.BlockSpec((tm, 128), lambda i: (i, 0)),
                  pl.BlockSpec((tm, 128), lambda i: (i, 0)),
                  pl.BlockSpec((128, 128), lambda i: (0, 0))],
        out_specs=(hspec(8, 64), hspec(2, 64), vtspec(2, 64), hspec(8, 64), hspec(8, 64), vtspec(4, 128)),
        compiler_params=_cp("parallel"),
    )(p, p, p, p, p, qgain, kgain, cos_t, sin_t, gmat)


def _flash_t(pairs, vt_ref, n_val, is_ctx_tile, n_ctx_chunks, n_chunks, m_ref, l_ref, acc_ref, st_ref):
    tk = ATT_TK
    m_ref[...] = jnp.full_like(m_ref, -jnp.inf)
    l_ref[...] = jnp.zeros_like(l_ref)
    acc_ref[...] = jnp.zeros_like(acc_ref)

    n = jnp.where(is_ctx_tile, n_ctx_chunks, n_chunks)

    def scores(c, slot):
        off = pl.multiple_of(c * tk, tk)
        parts = [lax.dot_general(k_ref[kidx, pl.ds(off, tk), :], q, NT, preferred_element_type=F32)
                 for q, k_ref, kidx in pairs]
        st_ref[slot] = parts[0] if len(parts) == 1 else jnp.concatenate(parts, axis=1)

    def update(c, slot):
        st = st_ref[slot]
        m_prev = m_ref[...]
        m_new = jnp.maximum(m_prev, jnp.max(st, axis=0, keepdims=True))
        alpha = jnp.exp2(m_prev - m_new)
        pt = jnp.exp2(st - m_new)
        l_ref[...] = alpha * l_ref[...] + jnp.sum(pt, axis=0, keepdims=True)
        pb = pt.astype(BF16)
        w = pb.shape[1] // n_val
        for g in range(n_val):
            sl = slice(g * w, (g + 1) * w)
            acc_ref[:, sl] = alpha[:, sl] * acc_ref[:, sl] + _dot(vt_ref[g, c], pb[:, sl])
        m_ref[...] = m_new

    scores(0, 0)

    def body(it, carry):
        c = 2 * it
        scores(c + 1, 1)
        update(c, 0)
        scores(jnp.minimum(c + 2, n - 1), 0)
        update(c + 1, 1)
        return carry

    lax.fori_loop(0, n // 2, body, 0)

    @pl.when(n % 2 == 1)
    def _():
        update(n - 1, 0)

    return acc_ref[...] / l_ref[...]


def _gqa_kernel(q_ref, k_ref, vt_ref, o_ref, m_ref, l_ref, acc_ref, st_ref, *, tq, ctx, s_tot):
    q = q_ref[...].reshape(GQA_GROUP * tq, HEAD_D)
    ot = _flash_t([(q, k_ref, 0)], vt_ref, 1, pl.program_id(1) < ctx // tq, ctx // ATT_TK, s_tot // ATT_TK,
                  m_ref, l_ref, acc_ref, st_ref)
    o2 = jnp.concatenate([ot[:, h * tq:(h + 1) * tq] for h in range(GQA_GROUP)], axis=0)
    o_ref[...] = o2.T.astype(o_ref.dtype)


def _gqa_attention(qg, kg, vgt, *, ctx):
    nh, s_tot, _ = qg.shape
    tq = ATT_TK
    m = GQA_GROUP * tq
    return pl.pallas_call(
        functools.partial(_gqa_kernel, tq=tq, ctx=ctx, s_tot=s_tot),
        name="gqa_attention",
        out_shape=jax.ShapeDtypeStruct((s_tot, nh * HEAD_D), BF16),
        grid=(nh // GQA_GROUP, s_tot // tq),
        in_specs=[pl.BlockSpec((GQA_GROUP, tq, HEAD_D), lambda g, i: (g, i, 0)),
                  pl.BlockSpec((1, s_tot, HEAD_D), lambda g, i: (g, 0, 0)),
                  pl.BlockSpec((1, s_tot // ATT_TK, HEAD_D, ATT_TK), lambda g, i: (g, 0, 0, 0))],
        out_specs=pl.BlockSpec((tq, GQA_GROUP * HEAD_D), lambda g, i: (i, g)),
        scratch_shapes=[pltpu.VMEM((1, m), F32), pltpu.VMEM((1, m), F32), pltpu.VMEM((HEAD_D, m), F32),
                        pltpu.VMEM((2, ATT_TK, m), F32)],
        compiler_params=_cp("parallel", "parallel"),
    )(qg, kg, vgt)


def _diff_kernel(q_ref, k_ref, vt_ref, lam_ref, gain_ref, o_ref, m_ref, l_ref, acc_ref, st_ref, *, tq, ctx, s_tot,
                 lam_init):
    pairs = [(q_ref[j], k_ref, j) for j in range(2 * DF_HPS)]
    ot = _flash_t(pairs, vt_ref, DF_HPS, pl.program_id(1) < ctx // tq, ctx // ATT_TK, s_tot // ATT_TK,
                  m_ref, l_ref, acc_ref, st_ref)
    lv = lam_ref[...]
    lam = (jnp.exp(jnp.sum(lv[0:1] * lv[1:2], axis=1, keepdims=True))
           - jnp.exp(jnp.sum(lv[2:3] * lv[3:4], axis=1, keepdims=True)) + lam_init)
    dv = ot.shape[0]
    for g in range(DF_HPS):
        o = (ot[:, 2 * g * tq:(2 * g + 1) * tq] - lam * ot[:, (2 * g + 1) * tq:(2 * g + 2) * tq]).T
        y = o * lax.rsqrt(jnp.mean(o * o, axis=-1, keepdims=True) + EPS) * gain_ref[...]
        o_ref[:, g * dv:(g + 1) * dv] = (y * (1.0 - lam_init)).astype(o_ref.dtype)


def _diff_attention(qd, kd, vdt, lam_vecs, sub_gain, *, ctx, lam_init):
    nh, n_chunks, dv, _ = vdt.shape
    s_tot = qd.shape[1]
    tq = ATT_TK
    g = DF_HPS
    m = 2 * g * tq
    return pl.pallas_call(
        functools.partial(_diff_kernel, tq=tq, ctx=ctx, s_tot=s_tot, lam_init=lam_init),
        name="diff_attention",
        out_shape=jax.ShapeDtypeStruct((s_tot, nh * dv), BF16),
        grid=(nh // g, s_tot // tq),
        in_specs=[pl.BlockSpec((2 * g, tq, HEAD_D), lambda h, i: (h, i, 0)),
                  pl.BlockSpec((2 * g, s_tot, HEAD_D), lambda h, i: (h, 0, 0)),
                  pl.BlockSpec((g, n_chunks, dv, ATT_TK), lambda h, i: (h, 0, 0, 0)),
                  pl.BlockSpec(lam_vecs.shape, lambda h, i: (0, 0)),
                  pl.BlockSpec((1, dv), lambda h, i: (0, 0))],
        out_specs=pl.BlockSpec((tq, g * dv), lambda h, i: (i, h)),
        scratch_shapes=[pltpu.VMEM((1, m), F32), pltpu.VMEM((1, m), F32), pltpu.VMEM((dv, m), F32),
                        pltpu.VMEM((2, ATT_TK, m), F32)],
        compiler_params=_cp("parallel", "parallel"),
    )(qd, kd, vdt, lam_vecs, sub_gain)


def _hy_filter_kernel(w1_ref, b1_ref, w2_ref, b2_ref, w3_ref, b3_ref, w4_ref, fr_ref, o_ref, *, tp, l, n, width):
    i = pl.program_id(0)
    idx = i * tp + lax.broadcasted_iota(jnp.int32, (tp, 1), 0)
    pos = jnp.where(idx < l, idx, n - idx).astype(F32)
    t = pos / (l - 1.0)
    w = (2.0 * math.pi) * pos / l
    lane = lax.broadcasted_iota(jnp.int32, (1, 128), 1)
    band = jnp.where(lane <= HY_BANDS, lane - 1, lane - 1 - HY_BANDS).astype(F32)
    f = 1e-4 + band * ((HY_BANDS - 1 - 1e-4) / (HY_BANDS - 1))
    phase = jnp.where(lane <= HY_BANDS, 0.5 * math.pi, math.pi)
    z = jnp.where(lane == 0, t, jnp.where(lane <= 2 * HY_BANDS, jnp.sin(w * f + phase), 0.0))
    fr = fr_ref[...]
    h = jnp.sin(fr * (_dot(z, w1_ref[...], HI) + b1_ref[...]))
    h = jnp.sin(fr * (_dot(h, w2_ref[...], HI) + b2_ref[...]))
    h = jnp.sin(fr * (_dot(h, w3_ref[...], HI) + b3_ref[...]))
    h = _dot(h, w4_ref[...], HI)
    ch = lax.broadcasted_iota(jnp.int32, (1, width), 1).astype(F32)
    min_decay = math.log(HY_TARGET) / HY_SLOW_DECAY
    max_decay = math.log(HY_TARGET) / HY_FAST_DECAY
    deltas = min_decay + ch * ((max_decay - min_decay) / (width - 1))
    window = jnp.exp(-t * jnp.abs(deltas))
    valid = jnp.logical_or(idx < l, idx > n - l)
    o_ref[...] = jnp.where(valid, h * window, 0.0)


def _hy_filter(filt, l, n):
    w1p, b1, w2, b2, w3, b3, w4, fr = filt
    width = w4.shape[1] // 2
    tp = min(512, l)
    full = lambda a: pl.BlockSpec(a.shape, lambda i: (0, 0))
    return pl.pallas_call(
        functools.partial(_hy_filter_kernel, tp=tp, l=l, n=n, width=width),
        name="hy_filter",
        out_shape=jax.ShapeDtypeStruct((n, width), F32),
        grid=(n // tp,),
        in_specs=[full(w1p), full(b1), full(w2), full(b2), full(w3), full(b3),
                  pl.BlockSpec((w4.shape[0], width), lambda i: (0, (i * tp >= l).astype(jnp.int32))),
                  full(fr)],
        out_specs=pl.BlockSpec((tp, width), lambda i: (i, 0)),
        compiler_params=_cp("parallel"),
    )(w1p, b1, w2, b2, w3, b3, w4, fr)


def _hy_ctx_kernel(x0_ref, x1_ref, v_ref, kern_ref, skip_ref, o_ref, ks_ref, u_ref, *, l):
    u = x1_ref[...] * v_ref[...]
    u_ref[...] = u
    base = jnp.concatenate([kern_ref[l:2 * l, :], kern_ref[0:l, :]], axis=0)
    ks_ref[0] = base
    for b in range(1, 8):
        ks_ref[b] = pltpu.roll(base, b, 0)

    def body(a, acc):
        start = pl.multiple_of(l - 8 * a, 8)
        ublk = u_ref[pl.ds(pl.multiple_of(8 * a, 8), 8), :]
        for b in range(8):
            acc = acc + ks_ref[b, pl.ds(start, l), :] * ublk[b:b + 1, :]
        return acc

    y = lax.fori_loop(0, l // 8, body, jnp.zeros((l, 128), F32))
    o_ref[...] = (x0_ref[...] * (y + u * skip_ref[...])).astype(o_ref.dtype)


def _hy_ctx(p, kern, skip, *, ctx):
    width = skip.shape[1]
    nb = width // 128
    return pl.pallas_call(
        functools.partial(_hy_ctx_kernel, l=ctx),
        name="hy_ctx_conv",
        out_shape=jax.ShapeDtypeStruct((ctx, width), BF16),
        grid=(nb,),
        in_specs=[pl.BlockSpec((ctx, 128), lambda c: (0, c)),
                  pl.BlockSpec((ctx, 128), lambda c: (0, nb + c)),
                  pl.BlockSpec((ctx, 128), lambda c: (0, 2 * nb + c)),
                  pl.BlockSpec((2 * ctx, 128), lambda c: (0, c)),
                  pl.BlockSpec((1, 128), lambda c: (0, c))],
        out_specs=pl.BlockSpec((ctx, 128), lambda c: (0, c)),
        scratch_shapes=[pltpu.VMEM((8, 2 * ctx, 128), F32), pltpu.VMEM((ctx, 128), F32)],
        compiler_params=_cp("parallel"),
    )(p, p, p, kern, skip)


def _hy_prep_kernel(x0_ref, x1_ref, v_ref, u_ref, x0o_ref):
    u_ref[...] = x1_ref[...] * v_ref[...]
    x0o_ref[...] = x0_ref[...]


def _hy_prep(p, *, ctx, width):
    s_lat = p.shape[0] - ctx
    tm = 256
    off = ctx // tm
    return pl.pallas_call(
        _hy_prep_kernel,
        name="hy_prep",
        out_shape=(jax.ShapeDtypeStruct((s_lat, width), F32), jax.ShapeDtypeStruct((s_lat, width), F32)),
        grid=(s_lat // tm,),
        in_specs=[pl.BlockSpec((tm, width), lambda i: (i + off, 0)),
                  pl.BlockSpec((tm, width), lambda i: (i + off, 1)),
                  pl.BlockSpec((tm, width), lambda i: (i + off, 2))],
        out_specs=(pl.BlockSpec((tm, width), lambda i: (i, 0)), pl.BlockSpec((tm, width), lambda i: (i, 0))),
        compiler_params=_cp("parallel"),
    )(p, p, p)


def _dft_consts(n1):
    n2 = FFT_N2
    n = n1 * n2
    f1n = n1 // 2 + 1
    f1p = -(-f1n // 8) * 8
    f1 = np.arange(f1p)[:, None].astype(np.float64)
    live = (f1 < f1n).astype(np.float64)
    t1 = np.arange(n1)[None, :]
    ang = 2 * np.pi * f1 * t1 / n1
    w_re, w_im = np.cos(ang) * live, -np.sin(ang) * live
    t0 = np.arange(n2)
    ang = 2 * np.pi * f1[None, :, :] * t0[:, None, None] / n
    tw_fwd = np.concatenate([np.cos(ang), -np.sin(ang)], axis=-1)
    ang = 2 * np.pi * f1[:, :, None] * t0[None, :, None] / n
    tw_inv = np.concatenate([np.cos(ang), np.sin(ang)], axis=-1)
    ang = 2 * np.pi * np.outer(t0, t0) / n2
    d_re, d_im = np.cos(ang), -np.sin(ang)
    wgt = np.where((f1 == 0) | (f1 == n1 // 2), 1.0, 2.0) * live
    t1o = np.arange(n1 // 2)[:, None]
    ang = 2 * np.pi * t1o * f1.T / n1
    c_m, s_m = np.cos(ang) * wgt.T / n, -np.sin(ang) * wgt.T / n
    f = lambda a: jnp.asarray(a, F32)
    return dict(f1p=f1p, w_re=f(w_re), w_im=f(w_im), tw_fwd=f(tw_fwd), tw_inv=f(tw_inv),
                d_re=f(d_re), d_im=f(d_im), c_m=f(c_m), s_m=f(s_m))


def _fft1_kernel(u_ref, wre_ref, wim_ref, tw_ref, bre_ref, bim_ref, *, tb, c):
    wre = _split_bf16(wre_ref[...])
    wim = _split_bf16(wim_ref[...])
    for b in range(tb):
        sl = slice(b * c, (b + 1) * c)
        u = _split_bf16(u_ref[:, sl])
        are = _mm3(wre, u)
        aim = _mm3(wim, u)
        tr = tw_ref[b, :, 0:1]
        ti = tw_ref[b, :, 1:2]
        bre_ref[:, sl] = are * tr - aim * ti
        bim_ref[:, sl] = are * ti + aim * tr


def _fft1(u2, w_re, w_im, tw_fwd, *, c):
    t1_in = u2.shape[0]
    f1p = w_re.shape[0]
    tb = 4
    out = jax.ShapeDtypeStruct((f1p, FFT_N2 * c), F32)
    return pl.pallas_call(
        functools.partial(_fft1_kernel, tb=tb, c=c),
        name="fft_outer",
        out_shape=(out, out),
        grid=(FFT_N2 // tb,),
        in_specs=[pl.BlockSpec((t1_in, tb * c), lambda i: (0, i)),
                  pl.BlockSpec((f1p, t1_in), lambda i: (0, 0)),
                  pl.BlockSpec((f1p, t1_in), lambda i: (0, 0)),
                  pl.BlockSpec((tb, f1p, 2), lambda i: (i, 0, 0))],
        out_specs=(pl.BlockSpec((f1p, tb * c), lambda i: (0, i)), pl.BlockSpec((f1p, tb * c), lambda i: (0, i))),
        compiler_params=_cp("parallel"),
    )(u2, w_re[:, :t1_in], w_im[:, :t1_in], tw_fwd)


def _fft_spec_kernel(bre_ref, bim_ref, dre_ref, dim_ref, xre_ref, xim_ref):
    br, bi = _split_bf16(bre_ref[0]), _split_bf16(bim_ref[0])
    dr, di = _split_bf16(dre_ref[...]), _split_bf16(dim_ref[...])
    xre_ref[0] = _mm3(dr, br) - _mm3(di, bi)
    xim_ref[0] = _mm3(dr, bi) + _mm3(di, br)


def _fft_spectrum(b_re, b_im, d_re, d_im):
    f1p, n2, c = b_re.shape
    blk = pl.BlockSpec((1, n2, c), lambda i: (i, 0, 0))
    mat = pl.BlockSpec((n2, n2), lambda i: (0, 0))
    out = jax.ShapeDtypeStruct((f1p, n2, c), F32)
    return pl.pallas_call(
        _fft_spec_kernel, name="fft_spectrum", out_shape=(out, out), grid=(f1p,),
        in_specs=[blk, blk, mat, mat], out_specs=(blk, blk),
        compiler_params=_cp("parallel"),
    )(b_re, b_im, d_re, d_im)


def _fft_mid_kernel(bre_ref, bim_ref, kre_ref, kim_ref, dre_ref, dim_ref, tw_ref, ore_ref, oim_ref):
    br, bi = _split_bf16(bre_ref[0]), _split_bf16(bim_ref[0])
    dr, di = _split_bf16(dre_ref[...]), _split_bf16(dim_ref[...])
    xr = _mm3(dr, br) - _mm3(di, bi)
    xi = _mm3(dr, bi) + _mm3(di, br)
    kr, ki = kre_ref[0], kim_ref[0]
    yr = _split_bf16(xr * kr - xi * ki)
    yi = _split_bf16(xr * ki + xi * kr)
    zr = _mm3(dr, yr) + _mm3(di, yi)
    zi = _mm3(dr, yi) - _mm3(di, yr)
    tr = tw_ref[0, :, 0:1]
    ti = tw_ref[0, :, 1:2]
    ore_ref[0] = zr * tr - zi * ti
    oim_ref[0] = zr * ti + zi * tr


def _fft_mid(b_re, b_im, k_re, k_im, d_re, d_im, tw_inv):
    f1p, n2, c = b_re.shape
    blk = pl.BlockSpec((1, n2, c), lambda i: (i, 0, 0))
    mat = pl.BlockSpec((n2, n2), lambda i: (0, 0))
    out = jax.ShapeDtypeStruct((f1p, n2, c), F32)
    return pl.pallas_call(
        _fft_mid_kernel, name="fft_mid", out_shape=(out, out), grid=(f1p,),
        in_specs=[blk, blk, blk, blk, mat, mat, pl.BlockSpec((1, n2, 2), lambda i: (i, 0, 0))],
        out_specs=(blk, blk),
        compiler_params=_cp("parallel"),
    )(b_re, b_im, k_re, k_im, d_re, d_im, tw_inv)


def _fft_fin_kernel(bre_ref, bim_ref, cm_ref, sm_ref, x0_ref, u_ref, skip_ref, o_ref, *, tb, c):
    y = (_mm3(_split_bf16(cm_ref[...]), _split_bf16(bre_ref[...]))
         + _mm3(_split_bf16(sm_ref[...]), _split_bf16(bim_ref[...])))
    skip = skip_ref[...]
    for b in range(tb):
        sl = slice(b * c, (b + 1) * c)
        o_ref[:, sl] = (x0_ref[:, sl] * (y[:, sl] + u_ref[:, sl] * skip)).astype(o_ref.dtype)


def _fft_fin(z_re, z_im, c_m, s_m, x0_2, u2, skip, *, c):
    f1p = z_re.shape[0]
    t1o = c_m.shape[0]
    tb = 4
    big = pl.BlockSpec((f1p, tb * c), lambda i: (0, i))
    sig = pl.BlockSpec((t1o, tb * c), lambda i: (0, i))
    mat = pl.BlockSpec((t1o, f1p), lambda i: (0, 0))
    return pl.pallas_call(
        functools.partial(_fft_fin_kernel, tb=tb, c=c),
        name="fft_final",
        out_shape=jax.ShapeDtypeStruct((t1o, FFT_N2 * c), BF16),
        grid=(FFT_N2 // tb,),
        in_specs=[big, big, mat, mat, sig, sig, pl.BlockSpec((1, c), lambda i: (0, 0))],
        out_specs=sig,
        compiler_params=_cp("parallel"),
    )(z_re, z_im, c_m, s_m, x0_2, u2, skip)


def _hyena_latent(p, kern, skip, *, ctx):
    c = skip.shape[1]
    s_lat = p.shape[0] - ctx
    n1 = 2 * s_lat // FFT_N2
    k = _dft_consts(n1)
    f1p = k["f1p"]
    u, x0 = _hy_prep(p, ctx=ctx, width=c)
    u2 = u.reshape(n1 // 2, FFT_N2 * c)
    x0_2 = x0.reshape(n1 // 2, FFT_N2 * c)
    kb_re, kb_im = _fft1(kern.reshape(n1, FFT_N2 * c), k["w_re"], k["w_im"], k["tw_fwd"], c=c)
    kf_re, kf_im = _fft_spectrum(kb_re.reshape(f1p, FFT_N2, c), kb_im.reshape(f1p, FFT_N2, c), k["d_re"], k["d_im"])
    b_re, b_im = _fft1(u2, k["w_re"], k["w_im"], k["tw_fwd"], c=c)
    z_re, z_im = _fft_mid(b_re.reshape(f1p, FFT_N2, c), b_im.reshape(f1p, FFT_N2, c), kf_re, kf_im,
                          k["d_re"], k["d_im"], k["tw_inv"])
    y2 = _fft_fin(z_re.reshape(f1p, FFT_N2 * c), z_im.reshape(f1p, FFT_N2 * c), k["c_m"], k["s_m"],
                  x0_2, u2, skip, c=c)
    return y2.reshape(s_lat, c)


def _softplus(x):
    return jnp.maximum(x, 0.0) + jnp.log(1.0 + jnp.exp(-jnp.abs(x)))


def _dotp(a, b, dims=(((1,), (0,)), ((), ())), passes=1):
    f = lambda x, y: lax.dot_general(x, y, dims, preferred_element_type=F32)
    if passes == 1:
        return f(a.astype(BF16), b.astype(BF16))
    ah, al = _split_bf16(a)
    bh, bl = _split_bf16(b)
    return f(ah, bh) + (f(al, bh) + f(ah, bl))


def _dn_intra_kernel(q_ref, k_ref, v_ref, sm_ref, cst_ref, u_ref, w_ref, qd_ref, kd_ref, qk_ref, last_ref):
    c = DN_CHUNK
    nh = q_ref.shape[1] // DN_HEAD
    r = nh * c
    row = lax.broadcasted_iota(jnp.int32, (c, c), 0)
    col = lax.broadcasted_iota(jnp.int32, (c, c), 1)
    sm = sm_ref[...]
    beta_all = jax.nn.sigmoid(sm)
    g_all = -jnp.exp(cst_ref[0:1, :]) * _softplus(sm + cst_ref[1:2, :])
    tot_all = jnp.sum(g_all, axis=0, keepdims=True)
    gc_all = (_dot((row >= col).astype(F32), g_all, HI), _dot((row <= col).astype(F32), g_all, HI))
    zpad = jnp.zeros((128 - c, 128), F32)
    gct_all = tuple(jnp.concatenate([g, zpad], axis=0).T for g in gc_all)

    qs, ks, vs = [], [], []
    for h in range(nh):
        hs = slice(h * DN_HEAD, (h + 1) * DN_HEAD)
        qh = _silu(q_ref[:, hs])
        kh = _silu(k_ref[:, hs])
        qs.append(qh * lax.rsqrt(jnp.sum(qh * qh, axis=-1, keepdims=True) + EPS) * (DN_HEAD ** -0.5))
        ks.append(kh * lax.rsqrt(jnp.sum(kh * kh, axis=-1, keepdims=True) + EPS))
        vs.append(_silu(v_ref[:, hs]))
    q_rows = jnp.concatenate(qs, axis=0)
    k_rows = jnp.concatenate(ks, axis=0)
    v_rows = jnp.concatenate(vs, axis=0)
    qkt = _dotp(q_rows, k_rows, NT)

    rr = lax.broadcasted_iota(jnp.int32, (r, r), 0)
    cc = lax.broadcasted_iota(jnp.int32, (r, r), 1)
    sh = int(math.log2(c))
    same = jnp.right_shift(rr, sh) == jnp.right_shift(cc, sh)
    incl = (jnp.logical_and(same, rr >= cc), jnp.logical_and(same, rr <= cc))
    strict = (jnp.logical_and(same, rr > cc), jnp.logical_and(same, rr < cc))

    gamma, egc, kdec, pw, sol = [], [], [], [], []
    for d in range(2):
        bis = [d * nh + h for h in range(nh)]
        gis = [2 * nh + b for b in bis]
        beta = jnp.concatenate([beta_all[:, b:b + 1] for b in bis], axis=0)
        gcol = jnp.concatenate([gc_all[d][:, g:g + 1] for g in gis], axis=0)
        grow = jnp.concatenate([gct_all[d][g:g + 1, :c] for g in gis], axis=1)
        gtot = jnp.concatenate([jnp.broadcast_to(tot_all[:, g:g + 1], (c, 1)) for g in gis], axis=0)
        gamma.append(jnp.where(incl[d], jnp.exp(jnp.where(incl[d], gcol - grow, 0.0)), 0.0))
        egc.append(jnp.exp(gcol))
        kdec.append(jnp.exp(gtot - gcol))
        kb = k_rows * beta
        pw.append(-jnp.where(strict[d], _dotp(kb, k_rows, NT) * gamma[d], 0.0))
        sol.append(jnp.concatenate([v_rows * beta, kb * egc[d]], axis=1))
    steps = int(math.log2(c))
    for t in range(steps):
        for d in range(2):
            sol[d] = sol[d] + _dotp(pw[d], sol[d], passes=DN_CHAIN_PASSES)
        if t + 1 < steps:
            for d in range(2):
                pw[d] = _dotp(pw[d], pw[d], passes=DN_CHAIN_PASSES)
    for d in range(2):
        qdec = q_rows * egc[d]
        kd = k_rows * kdec[d]
        qk = qkt * gamma[d]
        for h in range(nh):
            hs = slice(h * DN_HEAD, (h + 1) * DN_HEAD)
            rs = slice(h * c, (h + 1) * c)
            gi = 2 * nh + d * nh + h
            u_ref[d, :, hs] = sol[d][rs, :DN_HEAD]
            w_ref[d, :, hs] = sol[d][rs, DN_HEAD:]
            qd_ref[d, :, hs] = qdec[rs]
            kd_ref[d, :, hs] = kd[rs]
            qk_ref[d, :, rs] = qk[rs, rs]
            last_ref[d, 0, :, hs] = jnp.broadcast_to(jnp.exp(tot_all[:, gi:gi + 1]), (8, DN_HEAD))


def _dn_intra(p, cst):
    s_tot = p.shape[0]
    c = DN_CHUNK
    w = 512
    nch = s_tot // c
    big = jax.ShapeDtypeStruct((2, s_tot, w), F32)
    bspec = pl.BlockSpec((2, c, w), lambda j: (0, j, 0))
    return pl.pallas_call(
        _dn_intra_kernel,
        name="dn_intra",
        out_shape=(big, big, big, big, jax.ShapeDtypeStruct((2, s_tot, w // 2), F32),
                   jax.ShapeDtypeStruct((2, nch, 8, w), F32)),
        grid=(nch,),
        in_specs=[pl.BlockSpec((c, w), lambda j: (j, 3)),
                  pl.BlockSpec((c, w), lambda j: (j, 4)),
                  pl.BlockSpec((c, w), lambda j: (j, 5)),
                  pl.BlockSpec((c, 128), lambda j: (j, 46)),
                  pl.BlockSpec((8, 128), lambda j: (0, 0))],
        out_specs=(bspec, bspec, bspec, bspec, pl.BlockSpec((2, c, w // 2), lambda j: (0, j, 0)),
                   pl.BlockSpec((2, 1, 8, w), lambda j: (0, j, 0, 0))),
        compiler_params=_cp("parallel"),
    )(p, p, p, p, cst)


def _dn_rec_kernel(*refs, nh):
    ins, (of_ref, ob_ref, s_ref) = refs[:12], refs[12:]
    o_refs = (of_ref, ob_ref)

    @pl.when(pl.program_id(0) == 0)
    def _():
        s_ref[...] = jnp.zeros_like(s_ref)

    c = DN_CHUNK
    chains = [(d, h) for d in range(2) for h in range(nh)]
    hs = lambda h: slice(h * DN_HEAD, (h + 1) * DN_HEAD)
    s = [s_ref[d * nh + h] for d, h in chains]
    v_new = [ins[6 * d][0, :, hs(h)] - _dotp(ins[6 * d + 1][0, :, hs(h)], s[i]) for i, (d, h) in enumerate(chains)]
    qs = [_dotp(ins[6 * d + 2][0, :, hs(h)], s[i]) for i, (d, h) in enumerate(chains)]
    for i, (d, h) in enumerate(chains):
        o_refs[d][:, hs(h)] = qs[i] + _dotp(ins[6 * d + 4][0, :, h * c:(h + 1) * c], v_new[i])
    for i, (d, h) in enumerate(chains):
        s_ref[d * nh + h] = s[i] * ins[6 * d + 5][0, 0, 0:1, hs(h)] + _dotp(ins[6 * d + 3][0, :, hs(h)], v_new[i], TN)


def _dn_recurrence(u, w, qd, kd, qk, last, *, ctx):
    _, s_tot, width = u.shape
    c = DN_CHUNK
    nch = s_tot // c
    nctx = ctx // c
    nh = width // DN_HEAD
    fwd = lambda j: j
    bwd = lambda j: jnp.where(j < nctx, nctx - 1 - j, nch - 1 - (j - nctx))
    specs = []
    for d, cm in ((0, fwd), (1, bwd)):
        blk = lambda ww, cm=cm, d=d: pl.BlockSpec((1, c, ww), lambda j: (d, cm(j), 0))
        specs += [blk(width), blk(width), blk(width), blk(width), blk(width // 2),
                  pl.BlockSpec((1, 1, 8, width), lambda j, cm=cm, d=d: (d, cm(j), 0, 0))]
    out = jax.ShapeDtypeStruct((s_tot, width), F32)
    return pl.pallas_call(
        functools.partial(_dn_rec_kernel, nh=nh),
        name="dn_scan",
        out_shape=(out, out),
        grid=(nch,),
        in_specs=specs,
        out_specs=(pl.BlockSpec((c, width), lambda j: (fwd(j), 0)), pl.BlockSpec((c, width), lambda j: (bwd(j), 0))),
        scratch_shapes=[pltpu.VMEM((2 * nh, DN_HEAD, DN_HEAD), F32)],
        compiler_params=_cp("arbitrary"),
    )(u, w, qd, kd, qk, last, u, w, qd, kd, qk, last)


def _dn_finish_kernel(of_ref, ob_ref, gate_ref, gain_ref, o_ref):
    nh = of_ref.shape[1] // DN_HEAD
    for h in range(nh):
        hs = slice(h * DN_HEAD, (h + 1) * DN_HEAD)
        o = of_ref[:, hs] + ob_ref[:, hs]
        y = o * lax.rsqrt(jnp.mean(o * o, axis=-1, keepdims=True) + EPS) * gain_ref[...]
        o_ref[:, hs] = (y * _silu(gate_ref[:, hs])).astype(o_ref.dtype)


def _dn_finish(o_f, o_b, p, gain):
    s_tot, width = o_f.shape
    tm = _row_tile(s_tot, (528, 640, 256, 128))
    blk = pl.BlockSpec((tm, width), lambda i: (i, 0))
    return pl.pallas_call(
        _dn_finish_kernel,
        name="dn_finish",
        out_shape=jax.ShapeDtypeStruct((s_tot, width), BF16),
        grid=(s_tot // tm,),
        in_specs=[blk, blk, pl.BlockSpec((tm, width), lambda i: (i, 6)), pl.BlockSpec((1, DN_HEAD), lambda i: (0, 0))],
        out_specs=blk,
        compiler_params=_cp("parallel"),
    )(o_f, o_b, p, gain)


def _rope_tables(ctx, s_lat):
    rows = s_lat // GRID_W
    n_freq = HEAD_D // 4
    row = np.repeat(np.arange(rows, dtype=np.float32), GRID_W)
    col = np.tile(np.arange(GRID_W, dtype=np.float32), rows)
    inv = (ROPE_THETA ** (-np.arange(n_freq, dtype=np.float32) / n_freq)).astype(np.float32)
    ang = np.concatenate([row[:, None] * inv, col[:, None] * inv], axis=-1).astype(np.float32)
    cos, sin = np.cos(ang), np.sin(ang)
    cos = np.concatenate([np.ones((ctx, HEAD_D // 2), np.float32), cos], axis=0)
    sin = np.concatenate([np.zeros((ctx, HEAD_D // 2), np.float32), sin], axis=0)
    cos_t = np.tile(np.concatenate([cos, cos], axis=1), (1, 2))
    sin_t = np.tile(np.concatenate([-sin, sin], axis=1), (1, 2))
    return jnp.asarray(cos_t, F32), jnp.asarray(sin_t, F32)


def _pad_rows(a, rows):
    return jnp.concatenate([a, jnp.zeros((rows - a.shape[0],) + a.shape[1:], a.dtype)], axis=0)


def kernel(x, c, ctx, c_ctx, w_ada, b_ada, norm_mix_pre, norm_mix_post, norm_ffn_pre, norm_ffn_post, w_in, w_out, attn_q_norm, attn_k_norm, hy_short, hy_w1, hy_b1, hy_w2, hy_b2, hy_w3, hy_b3, hy_w4, hy_freq, hy_skip, dn_short, dn_a_log, dn_dt_bias, dn_norm, df_lambda, df_norm, ffn_up, ffn_conv, ffn_down):
    batch, s_lat, d = x.shape
    n_ctx = ctx.shape[1]
    depth = w_in.shape[0]
    gw = d // 4
    assert batch == 1 and gw == 512 and n_ctx % 256 == 0 and s_lat % 512 == 0
    s_tot = n_ctx + s_lat

    xx = jnp.concatenate([ctx[0], x[0]], axis=0)
    c_rows = _pad_rows(jnp.concatenate([c, c_ctx[None, :]], axis=0), 8)
    mods = _mod_vectors(c_rows, w_ada, b_ada)

    cos_t, sin_t = _rope_tables(n_ctx, s_lat)
    lane = np.arange(128)
    gmat = jnp.asarray((lane[:, None] // HEAD_D == lane[None, :] // HEAD_D) / HEAD_D, F32)
    ident = jnp.asarray([[0.0], [1.0], [0.0]], F32)

    a_cols = gw + 2 * 2 * HEAD_D
    o_hy, o_dn, o_df = a_cols, a_cols + 3 * gw, a_cols + 3 * gw + 4 * gw + 16
    order = [(o_hy, o_hy + 3 * gw), (o_dn, o_dn + 4 * gw), (o_df, o_df + 3 * gw), (0, a_cols),
             (o_dn + 4 * gw, o_dn + 4 * gw + 16)]
    n_proj = 12 * gw
    used = sum(b - a for a, b in order)

    for i in range(depth):
        lam_init = 0.8 - 0.6 * math.exp(-0.3 * i)
        mod = mods[i]
        sh1, sc1, gt1, sh2, sc2, gt2 = (mod[:, k * d:(k + 1) * d] for k in range(6))
        mod_mix = _pad_rows(jnp.concatenate([sh1[0:1], sc1[0:1], sh1[1:2], sc1[1:2]], axis=0), 8)
        mod_ffn = _pad_rows(jnp.concatenate([sh2[0:1], sc2[0:1], sh2[1:2], sc2[1:2]], axis=0), 8)

        w_in_p = jnp.concatenate([w_in[i][:, a:b] for a, b in order]
                                 + [jnp.zeros((d, n_proj - used), F32)], axis=1).astype(BF16)
        cw_in = jnp.concatenate([hy_short[i], dn_short[i], jnp.broadcast_to(ident, (3, n_proj - 6 * gw))], axis=1)
        p = _norm_mod_matmul_conv(xx, norm_mix_pre[i][None, :], mod_mix, w_in_p, _pad_rows(cw_in, 8),
                                  ctx=n_ctx, ffn=False, conv_cols=6 * gw)

        qg, kg, vgt, qd, kd, vdt = _attn_prep(p, jnp.tile(attn_q_norm[i], 2)[None, :],
                                              jnp.tile(attn_k_norm[i], 2)[None, :], cos_t, sin_t, gmat)
        ya = _gqa_attention(qg, kg, vgt, ctx=n_ctx)
        yd = _diff_attention(qd, kd, vdt, df_lambda[i], df_norm[i][None, :], ctx=n_ctx, lam_init=lam_init)

        filt = (_pad_rows(hy_w1[i], 128), hy_b1[i][None, :], hy_w2[i], hy_b2[i][None, :], hy_w3[i],
                hy_b3[i][None, :], hy_w4[i], hy_freq[i][None, :])
        skip = hy_skip[i][None, :]
        yb_lat = _hyena_latent(p, _hy_filter(filt, s_lat, 2 * s_lat), skip, ctx=n_ctx)
        if i < depth - 1:
            yb_ctx = _hy_ctx(p, _hy_filter(filt, n_ctx, 2 * n_ctx), skip, ctx=n_ctx)
        else:
            yb_ctx = jnp.zeros((n_ctx, gw), BF16)
        yb = jnp.concatenate([yb_ctx, yb_lat], axis=0)

        zeros8 = jnp.zeros((8,), F32)
        cst = _pad_rows(jnp.stack([jnp.concatenate([zeros8, dn_a_log[i].reshape(-1), jnp.zeros((112,), F32)]),
                                   jnp.concatenate([zeros8, dn_dt_bias[i].reshape(-1), jnp.zeros((112,), F32)])]), 8)
        o_f, o_b = _dn_recurrence(*_dn_intra(p, cst), ctx=n_ctx)
        yc = _dn_finish(o_f, o_b, p, dn_norm[i][None, :])

        y_cat = jnp.concatenate([ya, yb, yc, yd], axis=1)
        xx = _matmul_norm_residual(y_cat, w_out[i].astype(BF16), xx, gt1, norm_mix_post[i][None, :], ctx=n_ctx)

        g = _norm_mod_matmul_conv(xx, norm_ffn_pre[i][None, :], mod_ffn, ffn_up[i].astype(BF16),
                                  _pad_rows(ffn_conv[i], 8), ctx=n_ctx, ffn=True, conv_cols=ffn_down.shape[1])
        xx = _matmul_norm_residual(g, ffn_down[i].astype(BF16), xx, gt2, norm_ffn_post[i][None, :], ctx=n_ctx)

    return xx[n_ctx:][None]
```

```python
import functools
import math

import jax
import jax.numpy as jnp
import numpy as np
from jax import lax
from jax.experimental import pallas as pl
from jax.experimental.pallas import tpu as pltpu

F32 = jnp.float32
BF16 = jnp.bfloat16
HI = lax.Precision.HIGHEST
EPS = 1e-6

GRID_W = 64
ROPE_THETA = 10000.0
HEAD_D = 64
GQA_GROUP = 4
DN_HEAD = 128
DN_CHUNK = 64
DN_INTRA_CHUNKS = 2
DN_CHAIN_PASSES = 3
HY_BANDS = 16
HY_FAST_DECAY, HY_SLOW_DECAY, HY_TARGET = 0.3, 1.5, 1e-2
FFT_N2 = 128
ATT_TK = 256
DF_HPS = 2
LOG2E = 1.4426950408889634
VMEM_LIMIT = 56 * 1024 * 1024
NT = (((1,), (1,)), ((), ()))
TN = (((0,), (0,)), ((), ()))


def _cp(*sem):
    return pltpu.CompilerParams(dimension_semantics=sem, vmem_limit_bytes=VMEM_LIMIT)


def _dot(a, b, precision=None):
    return jnp.dot(a, b, precision=precision, preferred_element_type=F32)


def _split_bf16(a):
    hi = a.astype(BF16)
    return hi, (a - hi.astype(F32)).astype(BF16)


def _mm3(a, b):
    return _dot(a[0], b[0]) + (_dot(a[1], b[0]) + _dot(a[0], b[1]))


def _silu(x):
    return x * jax.nn.sigmoid(x)


def _row_tile(n, cands):
    for t in cands:
        if n % t == 0:
            return t
    raise ValueError(f"no row tile for {n}")


def _mod_kernel(c_ref, w_ref, b_ref, o_ref):
    o_ref[0] = _mm3(_split_bf16(_silu(c_ref[...])), _split_bf16(w_ref[0])) + b_ref[0]


def _mod_vectors(c_rows, w_ada, b_ada):
    depth, d, n = w_ada.shape
    tn = 512
    return pl.pallas_call(
        _mod_kernel,
        name="mod_vectors",
        out_shape=jax.ShapeDtypeStruct((depth, 8, n), F32),
        grid=(depth, n // tn),
        in_specs=[pl.BlockSpec((8, d), lambda l, j: (0, 0)),
                  pl.BlockSpec((1, d, tn), lambda l, j: (l, 0, j)),
                  pl.BlockSpec((1, 1, tn), lambda l, j: (l, 0, j))],
        out_specs=pl.BlockSpec((1, 8, tn), lambda l, j: (l, 0, j)),
        compiler_params=_cp("parallel", "parallel"),
    )(c_rows, w_ada, b_ada.reshape(depth, 1, n))


HALO = 16


def _k1_kernel(xp_ref, xm_ref, xn_ref, g_ref, mod_ref, *rest, tm, ctx, s_tot, ffn, n_conv):
    if ffn:
        wa_ref, wb_ref, cwa_ref, cwb_ref, o_ref, h_ref, acc_ref = rest
    else:
        wa_ref, cwa_ref, o_ref, h_ref, acc_ref = rest
    i = pl.program_id(0)
    j = pl.program_id(1)

    @pl.when(j == 0)
    def _():
        def nm(xv, row0):
            ms = jnp.mean(xv * xv, axis=-1, keepdims=True)
            y = xv * lax.rsqrt(ms + EPS) * g_ref[...]
            rows = row0 + lax.broadcasted_iota(jnp.int32, (xv.shape[0], 1), 0)
            is_ctx = rows < ctx
            sh = jnp.where(is_ctx, mod_ref[2:3, :], mod_ref[0:1, :])
            sc = jnp.where(is_ctx, mod_ref[3:4, :], mod_ref[1:2, :])
            return (y * (1.0 + sc) + sh).astype(BF16)

        h_ref[0:HALO] = nm(xp_ref[...], i * tm - HALO)
        n_sub = 3 if (tm > 640 and tm % 48 == 0) else 1
        rc = tm // n_sub
        for r0 in range(0, tm, rc):
            h_ref[HALO + r0:HALO + r0 + rc] = nm(xm_ref[r0:r0 + rc], i * tm + r0)
        h_ref[HALO + tm:] = nm(xn_ref[...], i * tm + tm)

    def conv(w_ref, cw_ref, edge):
        acc_ref[...] = _dot(h_ref[...], w_ref[...])
        prev = acc_ref[HALO - 1:HALO - 1 + tm]
        nxt = acc_ref[HALO + 1:HALO + 1 + tm]
        if edge:
            rows = i * tm + lax.broadcasted_iota(jnp.int32, (tm, 1), 0)
            prev = jnp.where(jnp.logical_and(rows != 0, rows != ctx), prev, 0.0)
            nxt = jnp.where(jnp.logical_and(rows != ctx - 1, rows != s_tot - 1), nxt, 0.0)
        cw = cw_ref[...]
        return acc_ref[HALO:HALO + tm] * cw[1:2] + prev * cw[0:1] + nxt * cw[2:3]

    def emit(edge):
        if ffn:
            a = conv(wa_ref, cwa_ref, edge)
            b = conv(wb_ref, cwb_ref, edge)
            o_ref[...] = (_silu(a) * b).astype(o_ref.dtype)
        else:
            o_ref[...] = conv(wa_ref, cwa_ref, edge).astype(o_ref.dtype)

    edge_tiles = sorted({r // tm for r in (0, ctx - 1, ctx, s_tot - 1)})
    is_edge = functools.reduce(jnp.logical_or, [i == e for e in edge_tiles])
    is_conv = j < n_conv
    pl.when(jnp.logical_and(is_conv, is_edge))(lambda: emit(True))
    pl.when(jnp.logical_and(is_conv, jnp.logical_not(is_edge)))(lambda: emit(False))

    if not ffn:
        @pl.when(jnp.logical_not(is_conv))
        def _():
            o_ref[...] = _dot(h_ref[HALO:HALO + tm], wa_ref[...]).astype(o_ref.dtype)


def _norm_mod_matmul_conv(xx, gain, mod4, w, cw, *, ctx, ffn, conv_cols):
    s_tot, d = xx.shape
    n = w.shape[1]
    tm = _row_tile(s_tot, (1056, 640, 256, 128))
    tn = 512
    nb = s_tot // HALO
    n_out = n // 2 if ffn else n
    x_specs = [
        pl.BlockSpec((HALO, d), lambda i, j: (jnp.maximum(i * (tm // HALO) - 1, 0), 0)),
        pl.BlockSpec((tm, d), lambda i, j: (i, 0)),
        pl.BlockSpec((HALO, d), lambda i, j: (jnp.minimum((i + 1) * (tm // HALO), nb - 1), 0)),
        pl.BlockSpec((1, d), lambda i, j: (0, 0)),
        pl.BlockSpec((8, d), lambda i, j: (0, 0)),
    ]
    if ffn:
        off = n_out // tn
        w_specs = [pl.BlockSpec((d, tn), lambda i, j: (0, j)),
                   pl.BlockSpec((d, tn), lambda i, j: (0, j + off)),
                   pl.BlockSpec((8, tn), lambda i, j: (0, j)),
                   pl.BlockSpec((8, tn), lambda i, j: (0, j + off))]
        args = (w, w, cw, cw)
        out_dtype = BF16
    else:
        w_specs = [pl.BlockSpec((d, tn), lambda i, j: (0, j)),
                   pl.BlockSpec((8, tn), lambda i, j: (0, j))]
        args = (w, cw)
        out_dtype = F32
    return pl.pallas_call(
        functools.partial(_k1_kernel, tm=tm, ctx=ctx, s_tot=s_tot, ffn=ffn, n_conv=conv_cols // tn),
        name="ffn_up_conv_gate" if ffn else "in_proj_conv",
        out_shape=jax.ShapeDtypeStruct((s_tot, n_out), out_dtype),
        grid=(s_tot // tm, n_out // tn),
        in_specs=x_specs + w_specs,
        out_specs=pl.BlockSpec((tm, tn), lambda i, j: (i, j)),
        scratch_shapes=[pltpu.VMEM((tm + 2 * HALO, d), BF16), pltpu.VMEM((tm + 2 * HALO, tn), F32)],
        compiler_params=_cp("parallel", "arbitrary"),
    )(xx, xx, xx, gain, mod4, *args)


def _k2_kernel(a_ref, w_ref, x_ref, gate_ref, gain_ref, o_ref, *, tm, ctx, nk):
    i = pl.program_id(0)
    k = pl.program_id(1)

    part = _dot(a_ref[...], w_ref[...])

    def finish(y):
        r = y * lax.rsqrt(jnp.mean(y * y, axis=-1, keepdims=True) + EPS) * gain_ref[...]
        rows = i * tm + lax.broadcasted_iota(jnp.int32, (tm, 1), 0)
        gt = jnp.where(rows < ctx, gate_ref[1:2, :], gate_ref[0:1, :])
        o_ref[...] = x_ref[...] + gt * r

    if nk == 1:
        finish(part)
        return

    @pl.when(k == 0)
    def _():
        o_ref[...] = part

    @pl.when(jnp.logical_and(k > 0, k < nk - 1))
    def _():
        o_ref[...] += part

    @pl.when(k == nk - 1)
    def _():
        finish(o_ref[...] + part)


def _matmul_norm_residual(a, w, xx, gate2, gain, *, ctx):
    s_tot, kdim = a.shape
    d = w.shape[1]
    tm = _row_tile(s_tot, (704, 640, 256, 128))
    nk = 1 if kdim <= 2048 else 4
    tk = kdim // nk
    assert tk % 128 == 0
    return pl.pallas_call(
        functools.partial(_k2_kernel, tm=tm, ctx=ctx, nk=nk),
        name="matmul_norm_residual",
        out_shape=jax.ShapeDtypeStruct((s_tot, d), F32),
        grid=(s_tot // tm, nk),
        in_specs=[pl.BlockSpec((tm, tk), lambda i, k: (i, k)),
                  pl.BlockSpec((tk, d), lambda i, k: (k, 0)),
                  pl.BlockSpec((tm, d), lambda i, k: (i, 0)),
                  pl.BlockSpec((8, d), lambda i, k: (0, 0)),
                  pl.BlockSpec((1, d), lambda i, k: (0, 0))],
        out_specs=pl.BlockSpec((tm, d), lambda i, k: (i, 0)),
        compiler_params=_cp("parallel", "arbitrary"),
    )(a, w, xx, gate2, gain)


def _out_proj_kernel(ya_ref, ybc_ref, ybl_ref, yc_ref, yd_ref, w_ref, x_ref, gate_ref, gain_ref, o_ref, *, n_ctx_tiles):
    is_ctx = pl.program_id(0) < n_ctx_tiles
    gw = ya_ref.shape[1]
    yb = jnp.where(is_ctx, ybc_ref[...], ybl_ref[...])
    y = sum(_dot(a, w_ref[g * gw:(g + 1) * gw, :])
            for g, a in enumerate((ya_ref[...], yb, yc_ref[...], yd_ref[...])))
    r = y * lax.rsqrt(jnp.mean(y * y, axis=-1, keepdims=True) + EPS) * gain_ref[...]
    gt = jnp.where(is_ctx, gate_ref[1:2, :], gate_ref[0:1, :])
    o_ref[...] = x_ref[...] + gt * r


def _out_proj(ya, yb_ctx, yb_lat, yc, yd, w, xx, gate2, gain, *, ctx):
    s_tot, d = xx.shape
    gw = ya.shape[1]
    tm = ATT_TK
    nct = ctx // tm
    blk = pl.BlockSpec((tm, gw), lambda i: (i, 0))
    return pl.pallas_call(
        functools.partial(_out_proj_kernel, n_ctx_tiles=nct),
        name="out_proj",
        out_shape=jax.ShapeDtypeStruct((s_tot, d), F32),
        grid=(s_tot // tm,),
        in_specs=[blk,
                  pl.BlockSpec((tm, gw), lambda i: (jnp.minimum(i, nct - 1), 0)),
                  pl.BlockSpec((tm, gw), lambda i: (jnp.maximum(i - nct, 0), 0)),
                  blk, blk,
                  pl.BlockSpec(w.shape, lambda i: (0, 0)),
                  pl.BlockSpec((tm, d), lambda i: (i, 0)),
                  pl.BlockSpec((8, d), lambda i: (0, 0)),
                  pl.BlockSpec((1, d), lambda i: (0, 0))],
        out_specs=pl.BlockSpec((tm, d), lambda i: (i, 0)),
        compiler_params=_cp("parallel"),
    )(ya, yb_ctx, yb_lat, yc, yd, w, xx, gate2, gain)


def _attn_prep_kernel(aq_ref, akv_ref, dq_ref, dk_ref, dv_ref, qgain_ref, kgain_ref, cos_ref, sin_ref,
                      gmat_ref, qg_ref, kg_ref, vgt_ref, qd_ref, kd_ref, vdt_ref):
    cosf = cos_ref[...]
    sins = sin_ref[...]
    lane = lax.broadcasted_iota(jnp.int32, (1, 128), 1)
    first_half = (lane & (HEAD_D - 1)) < (HEAD_D // 2)
    scale = HEAD_D ** -0.5 * LOG2E

    def rope(x):
        partner = jnp.where(first_half, pltpu.roll(x, 128 - HEAD_D // 2, 1), pltpu.roll(x, HEAD_D // 2, 1))
        return x * cosf + partner * sins

    def headnorm(x, gain):
        ms = _dot(x * x, gmat_ref[...], HI)
        return x * lax.rsqrt(ms + EPS) * gain

    def put(ref, pair, val):
        vb = val.astype(ref.dtype)
        ref[2 * pair] = vb[:, :HEAD_D]
        ref[2 * pair + 1] = vb[:, HEAD_D:]

    for ci in range(4):
        sl = slice(ci * 128, (ci + 1) * 128)
        put(qg_ref, ci, rope(headnorm(aq_ref[:, sl], qgain_ref[...])) * scale)
        put(qd_ref, ci, rope(dq_ref[:, sl]) * scale)
        put(kd_ref, ci, rope(dk_ref[:, sl]))
        vdt_ref[ci, 0] = dv_ref[:, sl].T.astype(vdt_ref.dtype)
    put(kg_ref, 0, rope(headnorm(akv_ref[:, 0:128], kgain_ref[...])))
    vt = akv_ref[:, 128:256].T.astype(vgt_ref.dtype)
    vgt_ref[0, 0] = vt[:HEAD_D]
    vgt_ref[1, 0] = vt[HEAD_D:]


def _attn_prep(p, qgain, kgain, cos_t, sin_t, gmat):
    s_tot = p.shape[0]
    tm = ATT_TK
    hm = lambda nh, dd: jax.ShapeDtypeStruct((nh, s_tot, dd), BF16)
    hspec = lambda nh, dd: pl.BlockSpec((nh, tm, dd), lambda i: (0, i, 0))
    vt = lambda nh, dd: jax.ShapeDtypeStruct((nh, s_tot // tm, dd, tm), BF16)
    vtspec = lambda nh, dd: pl.BlockSpec((nh, 1, dd, tm), lambda i: (0, i, 0, 0))
    return pl.pallas_call(
        _attn_prep_kernel,
        name="attn_prep",
        out_shape=(hm(8, 64), hm(2, 64), vt(2, 64), hm(8, 64), hm(8, 64), vt(4, 128)),
        grid=(s_tot // tm,),
        in_specs=[pl.BlockSpec((tm, 512), lambda i: (i, 10)),
                  pl.BlockSpec((tm, 256), lambda i: (i, 22)),
                  pl.BlockSpec((tm, 512), lambda i: (i, 7)),
                  pl.BlockSpec((tm, 512), lambda i: (i, 8)),
                  pl.BlockSpec((tm, 512), lambda i: (i, 9)),
                  pl.BlockSpec((1, 128), lambda i: (0, 0)),
                  pl.BlockSpec((1, 128), lambda i: (0, 0)),
                  pl.BlockSpec((tm, 128), lambda i: (i, 0)),
                  pl.BlockSpec((tm, 128), lambda i: (i, 0)),
                  pl.BlockSpec((128, 128), lambda i: (0, 0))],
        out_specs=(hspec(8, 64), hspec(2, 64), vtspec(2, 64), hspec(8, 64), hspec(8, 64), vtspec(4, 128)),
        compiler_params=_cp("parallel"),
    )(p, p, p, p, p, qgain, kgain, cos_t, sin_t, gmat)


def _flash_t(pairs, vt_ref, n_val, is_ctx_tile, n_ctx_chunks, n_chunks, m_ref, l_ref, acc_ref, st_ref):
    tk = ATT_TK
    m_ref[...] = jnp.full_like(m_ref, -jnp.inf)
    l_ref[...] = jnp.zeros_like(l_ref)
    acc_ref[...] = jnp.zeros_like(acc_ref)

    n = jnp.where(is_ctx_tile, n_ctx_chunks, n_chunks)

    def scores(c, slot):
        off = pl.multiple_of(c * tk, tk)
        parts = [lax.dot_general(k_ref[kidx, pl.ds(off, tk), :], q, NT, preferred_element_type=F32)
                 for q, k_ref, kidx in pairs]
        st_ref[slot] = parts[0] if len(parts) == 1 else jnp.concatenate(parts, axis=1)

    def update(c, slot):
        st = st_ref[slot]
        m_prev = m_ref[...]
        m_new = jnp.maximum(m_prev, jnp.max(st, axis=0, keepdims=True))
        alpha = jnp.exp2(m_prev - m_new)
        pt = jnp.exp2(st - m_new)
        l_ref[...] = alpha * l_ref[...] + jnp.sum(pt, axis=0, keepdims=True)
        pb = pt.astype(BF16)
        w = pb.shape[1] // n_val
        for g in range(n_val):
            sl = slice(g * w, (g + 1) * w)
            acc_ref[:, sl] = alpha[:, sl] * acc_ref[:, sl] + _dot(vt_ref[g, c], pb[:, sl])
        m_ref[...] = m_new

    scores(0, 0)

    def body(it, carry):
        c = 2 * it
        scores(c + 1, 1)
        update(c, 0)
        scores(jnp.minimum(c + 2, n - 1), 0)
        update(c + 1, 1)
        return carry

    lax.fori_loop(0, n // 2, body, 0)

    @pl.when(n % 2 == 1)
    def _():
        update(n - 1, 0)

    return acc_ref[...] / l_ref[...]


def _gqa_kernel(q_ref, k_ref, vt_ref, o_ref, m_ref, l_ref, acc_ref, st_ref, *, tq, ctx, s_tot):
    q = q_ref[...].reshape(GQA_GROUP * tq, HEAD_D)
    ot = _flash_t([(q, k_ref, 0)], vt_ref, 1, pl.program_id(1) < ctx // tq, ctx // ATT_TK, s_tot // ATT_TK,
                  m_ref, l_ref, acc_ref, st_ref)
    o2 = jnp.concatenate([ot[:, h * tq:(h + 1) * tq] for h in range(GQA_GROUP)], axis=0)
    o_ref[...] = o2.T.astype(o_ref.dtype)


def _gqa_attention(qg, kg, vgt, *, ctx):
    nh, s_tot, _ = qg.shape
    tq = ATT_TK
    m = GQA_GROUP * tq
    return pl.pallas_call(
        functools.partial(_gqa_kernel, tq=tq, ctx=ctx, s_tot=s_tot),
        name="gqa_attention",
        out_shape=jax.ShapeDtypeStruct((s_tot, nh * HEAD_D), BF16),
        grid=(nh // GQA_GROUP, s_tot // tq),
        in_specs=[pl.BlockSpec((GQA_GROUP, tq, HEAD_D), lambda g, i: (g, i, 0)),
                  pl.BlockSpec((1, s_tot, HEAD_D), lambda g, i: (g, 0, 0)),
                  pl.BlockSpec((1, s_tot // ATT_TK, HEAD_D, ATT_TK), lambda g, i: (g, 0, 0, 0))],
        out_specs=pl.BlockSpec((tq, GQA_GROUP * HEAD_D), lambda g, i: (i, g)),
        scratch_shapes=[pltpu.VMEM((1, m), F32), pltpu.VMEM((1, m), F32), pltpu.VMEM((HEAD_D, m), F32),
                        pltpu.VMEM((2, ATT_TK, m), F32)],
        compiler_params=_cp("parallel", "parallel"),
    )(qg, kg, vgt)


def _diff_kernel(q_ref, k_ref, vt_ref, lam_ref, gain_ref, o_ref, m_ref, l_ref, acc_ref, st_ref, *, tq, ctx, s_tot,
                 lam_init):
    pairs = [(q_ref[j], k_ref, j) for j in range(2 * DF_HPS)]
    ot = _flash_t(pairs, vt_ref, DF_HPS, pl.program_id(1) < ctx // tq, ctx // ATT_TK, s_tot // ATT_TK,
                  m_ref, l_ref, acc_ref, st_ref)
    lv = lam_ref[...]
    lam = (jnp.exp(jnp.sum(lv[0:1] * lv[1:2], axis=1, keepdims=True))
           - jnp.exp(jnp.sum(lv[2:3] * lv[3:4], axis=1, keepdims=True)) + lam_init)
    dv = ot.shape[0]
    for g in range(DF_HPS):
        o = (ot[:, 2 * g * tq:(2 * g + 1) * tq] - lam * ot[:, (2 * g + 1) * tq:(2 * g + 2) * tq]).T
        y = o * lax.rsqrt(jnp.mean(o * o, axis=-1, keepdims=True) + EPS) * gain_ref[...]
        o_ref[:, g * dv:(g + 1) * dv] = (y * (1.0 - lam_init)).astype(o_ref.dtype)


def _diff_attention(qd, kd, vdt, lam_vecs, sub_gain, *, ctx, lam_init):
    nh, n_chunks, dv, _ = vdt.shape
    s_tot = qd.shape[1]
    tq = ATT_TK
    g = DF_HPS
    m = 2 * g * tq
    return pl.pallas_call(
        functools.partial(_diff_kernel, tq=tq, ctx=ctx, s_tot=s_tot, lam_init=lam_init),
        name="diff_attention",
        out_shape=jax.ShapeDtypeStruct((s_tot, nh * dv), BF16),
        grid=(nh // g, s_tot // tq),
        in_specs=[pl.BlockSpec((2 * g, tq, HEAD_D), lambda h, i: (h, i, 0)),
                  pl.BlockSpec((2 * g, s_tot, HEAD_D), lambda h, i: (h, 0, 0)),
                  pl.BlockSpec((g, n_chunks, dv, ATT_TK), lambda h, i: (h, 0, 0, 0)),
                  pl.BlockSpec(lam_vecs.shape, lambda h, i: (0, 0)),
                  pl.BlockSpec((1, dv), lambda h, i: (0, 0))],
        out_specs=pl.BlockSpec((tq, g * dv), lambda h, i: (i, h)),
        scratch_shapes=[pltpu.VMEM((1, m), F32), pltpu.VMEM((1, m), F32), pltpu.VMEM((dv, m), F32),
                        pltpu.VMEM((2, ATT_TK, m), F32)],
        compiler_params=_cp("parallel", "parallel"),
    )(qd, kd, vdt, lam_vecs, sub_gain)


def _hy_filter_kernel(w1_ref, b1_ref, w2_ref, b2_ref, w3_ref, b3_ref, w4_ref, fr_ref, o_ref, *, tp, l, n, width):
    i = pl.program_id(0)
    def position(idx):
        pos = jnp.where(idx < l, idx, n - idx).astype(F32)
        return pos / (l - 1.0), (2.0 * math.pi) * pos / l

    t_l, w_l = position(i * tp + lax.broadcasted_iota(jnp.int32, (1, tp), 1))
    nf = w1_ref.shape[1]
    r = lax.broadcasted_iota(jnp.int32, (nf, 1), 0)
    band = jnp.where(r <= HY_BANDS, r - 1, r - 1 - HY_BANDS).astype(F32)
    f = 1e-4 + band * ((HY_BANDS - 1 - 1e-4) / (HY_BANDS - 1))
    phase = jnp.where(r <= HY_BANDS, 0.5 * math.pi, math.pi)
    z = jnp.where(r == 0, t_l, jnp.where(r <= 2 * HY_BANDS, jnp.sin(f * w_l + phase), 0.0))
    fr = fr_ref[...]
    h = jnp.sin(fr * (_dot(w1_ref[...], z, HI) + b1_ref[...]))
    h = jnp.sin(fr * (_dot(w2_ref[...], h, HI) + b2_ref[...]))
    h = jnp.sin(fr * (_dot(w3_ref[...], h, HI) + b3_ref[...]))
    h = lax.dot_general(h, w4_ref[...], TN, precision=HI, preferred_element_type=F32)
    idx = i * tp + lax.broadcasted_iota(jnp.int32, (tp, 1), 0)
    t, _ = position(idx)
    ch = lax.broadcasted_iota(jnp.int32, (1, width), 1).astype(F32)
    min_decay = math.log(HY_TARGET) / HY_SLOW_DECAY
    max_decay = math.log(HY_TARGET) / HY_FAST_DECAY
    deltas = min_decay + ch * ((max_decay - min_decay) / (width - 1))
    window = jnp.exp(-t * jnp.abs(deltas))
    valid = jnp.logical_or(idx < l, idx > n - l)
    o_ref[...] = jnp.where(valid, h * window, 0.0)


def _hy_filter(filt, l, n):
    w1p, b1, w2, b2, w3, b3, w4, fr = filt
    width = w4.shape[1] // 2
    tp = min(512, l)
    full = lambda a: pl.BlockSpec(a.shape, lambda i: (0, 0))
    return pl.pallas_call(
        functools.partial(_hy_filter_kernel, tp=tp, l=l, n=n, width=width),
        name="hy_filter",
        out_shape=jax.ShapeDtypeStruct((n, width), F32),
        grid=(n // tp,),
        in_specs=[full(w1p), full(b1), full(w2), full(b2), full(w3), full(b3),
                  pl.BlockSpec((w4.shape[0], width), lambda i: (0, (i * tp >= l).astype(jnp.int32))),
                  full(fr)],
        out_specs=pl.BlockSpec((tp, width), lambda i: (i, 0)),
        compiler_params=_cp("parallel"),
    )(w1p, b1, w2, b2, w3, b3, w4, fr)


def _hy_ctx_kernel(x0_ref, x1_ref, v_ref, kern_ref, skip_ref, o_ref, ks_ref, u_ref, *, l):
    u = x1_ref[...] * v_ref[...]
    u_ref[...] = u
    base = jnp.concatenate([kern_ref[l:2 * l, :], kern_ref[0:l, :]], axis=0)
    ks_ref[0] = base
    for b in range(1, 8):
        ks_ref[b] = pltpu.roll(base, b, 0)

    def body(a, acc):
        start = pl.multiple_of(l - 8 * a, 8)
        ublk = u_ref[pl.ds(pl.multiple_of(8 * a, 8), 8), :]
        for b in range(8):
            acc = acc + ks_ref[b, pl.ds(start, l), :] * ublk[b:b + 1, :]
        return acc

    y = lax.fori_loop(0, l // 8, body, jnp.zeros((l, 128), F32))
    o_ref[...] = (x0_ref[...] * (y + u * skip_ref[...])).astype(o_ref.dtype)


def _hy_ctx(p, kern, skip, *, ctx):
    width = skip.shape[1]
    nb = width // 128
    return pl.pallas_call(
        functools.partial(_hy_ctx_kernel, l=ctx),
        name="hy_ctx_conv",
        out_shape=jax.ShapeDtypeStruct((ctx, width), BF16),
        grid=(nb,),
        in_specs=[pl.BlockSpec((ctx, 128), lambda c: (0, c)),
                  pl.BlockSpec((ctx, 128), lambda c: (0, nb + c)),
                  pl.BlockSpec((ctx, 128), lambda c: (0, 2 * nb + c)),
                  pl.BlockSpec((2 * ctx, 128), lambda c: (0, c)),
                  pl.BlockSpec((1, 128), lambda c: (0, c))],
        out_specs=pl.BlockSpec((ctx, 128), lambda c: (0, c)),
        scratch_shapes=[pltpu.VMEM((8, 2 * ctx, 128), F32), pltpu.VMEM((ctx, 128), F32)],
        compiler_params=_cp("parallel"),
    )(p, p, p, kern, skip)


def _hy_prep_kernel(x0_ref, x1_ref, v_ref, u_ref, x0o_ref):
    u_ref[...] = x1_ref[...] * v_ref[...]
    x0o_ref[...] = x0_ref[...]


def _hy_prep(p, *, ctx, width):
    s_lat = p.shape[0] - ctx
    tm = 256
    off = ctx // tm
    return pl.pallas_call(
        _hy_prep_kernel,
        name="hy_prep",
        out_shape=(jax.ShapeDtypeStruct((s_lat, width), F32), jax.ShapeDtypeStruct((s_lat, width), F32)),
        grid=(s_lat // tm,),
        in_specs=[pl.BlockSpec((tm, width), lambda i: (i + off, 0)),
                  pl.BlockSpec((tm, width), lambda i: (i + off, 1)),
                  pl.BlockSpec((tm, width), lambda i: (i + off, 2))],
        out_specs=(pl.BlockSpec((tm, width), lambda i: (i, 0)), pl.BlockSpec((tm, width), lambda i: (i, 0))),
        compiler_params=_cp("parallel"),
    )(p, p, p)


def _dft_consts(n1):
    n2 = FFT_N2
    n = n1 * n2
    f1n = n1 // 2 + 1
    f1p = -(-f1n // 8) * 8
    f1 = np.arange(f1p)[:, None].astype(np.float64)
    live = (f1 < f1n).astype(np.float64)
    t1 = np.arange(n1)[None, :]
    ang = 2 * np.pi * f1 * t1 / n1
    w_re, w_im = np.cos(ang) * live, -np.sin(ang) * live
    t0 = np.arange(n2)
    ang = 2 * np.pi * f1[None, :, :] * t0[:, None, None] / n
    tw_fwd = np.concatenate([np.cos(ang), -np.sin(ang)], axis=-1)
    ang = 2 * np.pi * f1[:, :, None] * t0[None, :, None] / n
    tw_inv = np.concatenate([np.cos(ang), np.sin(ang)], axis=-1)
    ang = 2 * np.pi * np.outer(t0, t0) / n2
    d_re, d_im = np.cos(ang), -np.sin(ang)
    wgt = np.where((f1 == 0) | (f1 == n1 // 2), 1.0, 2.0) * live
    t1o = np.arange(n1 // 2)[:, None]
    ang = 2 * np.pi * t1o * f1.T / n1
    c_m, s_m = np.cos(ang) * wgt.T / n, -np.sin(ang) * wgt.T / n
    f = lambda a: jnp.asarray(a, F32)
    return dict(f1p=f1p, w_re=f(w_re), w_im=f(w_im), tw_fwd=f(tw_fwd), tw_inv=f(tw_inv),
                d_re=f(d_re), d_im=f(d_im), c_m=f(c_m), s_m=f(s_m))


def _fft1_kernel(u_ref, wre_ref, wim_ref, tw_ref, bre_ref, bim_ref, *, tb, c):
    wre = _split_bf16(wre_ref[...])
    wim = _split_bf16(wim_ref[...])
    for b in range(tb):
        sl = slice(b * c, (b + 1) * c)
        u = _split_bf16(u_ref[:, sl])
        are = _mm3(wre, u)
        aim = _mm3(wim, u)
        tr = tw_ref[b, :, 0:1]
        ti = tw_ref[b, :, 1:2]
        bre_ref[:, sl] = are * tr - aim * ti
        bim_ref[:, sl] = are * ti + aim * tr


def _fft1(u2, w_re, w_im, tw_fwd, *, c):
    t1_in = u2.shape[0]
    f1p = w_re.shape[0]
    tb = 4
    out = jax.ShapeDtypeStruct((f1p, FFT_N2 * c), F32)
    return pl.pallas_call(
        functools.partial(_fft1_kernel, tb=tb, c=c),
        name="fft_outer",
        out_shape=(out, out),
        grid=(FFT_N2 // tb,),
        in_specs=[pl.BlockSpec((t1_in, tb * c), lambda i: (0, i)),
                  pl.BlockSpec((f1p, t1_in), lambda i: (0, 0)),
                  pl.BlockSpec((f1p, t1_in), lambda i: (0, 0)),
                  pl.BlockSpec((tb, f1p, 2), lambda i: (i, 0, 0))],
        out_specs=(pl.BlockSpec((f1p, tb * c), lambda i: (0, i)), pl.BlockSpec((f1p, tb * c), lambda i: (0, i))),
        compiler_params=_cp("parallel"),
    )(u2, w_re[:, :t1_in], w_im[:, :t1_in], tw_fwd)


def _fft_spec_kernel(bre_ref, bim_ref, dre_ref, dim_ref, xre_ref, xim_ref):
    br, bi = _split_bf16(bre_ref[0]), _split_bf16(bim_ref[0])
    dr, di = _split_bf16(dre_ref[...]), _split_bf16(dim_ref[...])
    xre_ref[0] = _mm3(dr, br) - _mm3(di, bi)
    xim_ref[0] = _mm3(dr, bi) + _mm3(di, br)


def _fft_spectrum(b_re, b_im, d_re, d_im):
    f1p, n2, c = b_re.shape
    blk = pl.BlockSpec((1, n2, c), lambda i: (i, 0, 0))
    mat = pl.BlockSpec((n2, n2), lambda i: (0, 0))
    out = jax.ShapeDtypeStruct((f1p, n2, c), F32)
    return pl.pallas_call(
        _fft_spec_kernel, name="fft_spectrum", out_shape=(out, out), grid=(f1p,),
        in_specs=[blk, blk, mat, mat], out_specs=(blk, blk),
        compiler_params=_cp("parallel"),
    )(b_re, b_im, d_re, d_im)


def _fft_mid_kernel(bre_ref, bim_ref, kre_ref, kim_ref, dre_ref, dim_ref, tw_ref, ore_ref, oim_ref):
    br, bi = _split_bf16(bre_ref[0]), _split_bf16(bim_ref[0])
    dr, di = _split_bf16(dre_ref[...]), _split_bf16(dim_ref[...])
    xr = _mm3(dr, br) - _mm3(di, bi)
    xi = _mm3(dr, bi) + _mm3(di, br)
    kr, ki = kre_ref[0], kim_ref[0]
    yr = _split_bf16(xr * kr - xi * ki)
    yi = _split_bf16(xr * ki + xi * kr)
    zr = _mm3(dr, yr) + _mm3(di, yi)
    zi = _mm3(dr, yi) - _mm3(di, yr)
    tr = tw_ref[0, :, 0:1]
    ti = tw_ref[0, :, 1:2]
    ore_ref[0] = zr * tr - zi * ti
    oim_ref[0] = zr * ti + zi * tr


def _fft_mid(b_re, b_im, k_re, k_im, d_re, d_im, tw_inv):
    f1p, n2, c = b_re.shape
    blk = pl.BlockSpec((1, n2, c), lambda i: (i, 0, 0))
    mat = pl.BlockSpec((n2, n2), lambda i: (0, 0))
    out = jax.ShapeDtypeStruct((f1p, n2, c), F32)
    return pl.pallas_call(
        _fft_mid_kernel, name="fft_mid", out_shape=(out, out), grid=(f1p,),
        in_specs=[blk, blk, blk, blk, mat, mat, pl.BlockSpec((1, n2, 2), lambda i: (i, 0, 0))],
        out_specs=(blk, blk),
        compiler_params=_cp("parallel"),
    )(b_re, b_im, k_re, k_im, d_re, d_im, tw_inv)


def _fft_fin_kernel(bre_ref, bim_ref, cm_ref, sm_ref, x0_ref, u_ref, skip_ref, o_ref, *, tb, c):
    y = (_mm3(_split_bf16(cm_ref[...]), _split_bf16(bre_ref[...]))
         + _mm3(_split_bf16(sm_ref[...]), _split_bf16(bim_ref[...])))
    skip = skip_ref[...]
    for b in range(tb):
        sl = slice(b * c, (b + 1) * c)
        o_ref[:, sl] = (x0_ref[:, sl] * (y[:, sl] + u_ref[:, sl] * skip)).astype(o_ref.dtype)


def _fft_fin(z_re, z_im, c_m, s_m, x0_2, u2, skip, *, c):
    f1p = z_re.shape[0]
    t1o = c_m.shape[0]
    tb = 4
    big = pl.BlockSpec((f1p, tb * c), lambda i: (0, i))
    sig = pl.BlockSpec((t1o, tb * c), lambda i: (0, i))
    mat = pl.BlockSpec((t1o, f1p), lambda i: (0, 0))
    return pl.pallas_call(
        functools.partial(_fft_fin_kernel, tb=tb, c=c),
        name="fft_final",
        out_shape=jax.ShapeDtypeStruct((t1o, FFT_N2 * c), BF16),
        grid=(FFT_N2 // tb,),
        in_specs=[big, big, mat, mat, sig, sig, pl.BlockSpec((1, c), lambda i: (0, 0))],
        out_specs=sig,
        compiler_params=_cp("parallel"),
    )(z_re, z_im, c_m, s_m, x0_2, u2, skip)


def _hyena_latent(p, kern, skip, *, ctx):
    c = skip.shape[1]
    s_lat = p.shape[0] - ctx
    n1 = 2 * s_lat // FFT_N2
    k = _dft_consts(n1)
    f1p = k["f1p"]
    u, x0 = _hy_prep(p, ctx=ctx, width=c)
    u2 = u.reshape(n1 // 2, FFT_N2 * c)
    x0_2 = x0.reshape(n1 // 2, FFT_N2 * c)
    kb_re, kb_im = _fft1(kern.reshape(n1, FFT_N2 * c), k["w_re"], k["w_im"], k["tw_fwd"], c=c)
    kf_re, kf_im = _fft_spectrum(kb_re.reshape(f1p, FFT_N2, c), kb_im.reshape(f1p, FFT_N2, c), k["d_re"], k["d_im"])
    b_re, b_im = _fft1(u2, k["w_re"], k["w_im"], k["tw_fwd"], c=c)
    z_re, z_im = _fft_mid(b_re.reshape(f1p, FFT_N2, c), b_im.reshape(f1p, FFT_N2, c), kf_re, kf_im,
                          k["d_re"], k["d_im"], k["tw_inv"])
    y2 = _fft_fin(z_re.reshape(f1p, FFT_N2 * c), z_im.reshape(f1p, FFT_N2 * c), k["c_m"], k["s_m"],
                  x0_2, u2, skip, c=c)
    return y2.reshape(s_lat, c)


def _softplus(x):
    return jnp.maximum(x, 0.0) + jnp.log(1.0 + jnp.exp(-jnp.abs(x)))


def _dotp(a, b, dims=(((1,), (0,)), ((), ())), passes=1):
    f = lambda x, y: lax.dot_general(x, y, dims, preferred_element_type=F32)
    if passes == 1:
        return f(a.astype(BF16), b.astype(BF16))
    ah, al = _split_bf16(a)
    bh, bl = _split_bf16(b)
    return f(ah, bh) + (f(al, bh) + f(ah, bl))


DN_U, DN_W, DN_QD, DN_KD, DN_QK = range(5)


def _dn_intra_kernel(q_ref, k_ref, v_ref, sm_ref, cst_ref, pk_ref, last_ref):
    c = DN_CHUNK
    nh = q_ref.shape[1] // DN_HEAD
    n_cb = q_ref.shape[0] // c
    r = nh * c
    row = lax.broadcasted_iota(jnp.int32, (c, c), 0)
    col = lax.broadcasted_iota(jnp.int32, (c, c), 1)
    tri = ((row >= col).astype(F32), (row <= col).astype(F32))
    zpad = jnp.zeros((128 - c, 128), F32)
    rr = lax.broadcasted_iota(jnp.int32, (r, r), 0)
    cc = lax.broadcasted_iota(jnp.int32, (r, r), 1)
    sh = int(math.log2(c))
    same = jnp.right_shift(rr, sh) == jnp.right_shift(cc, sh)
    incl = (jnp.logical_and(same, rr >= cc), jnp.logical_and(same, rr <= cc))
    strict = (jnp.logical_and(same, rr > cc), jnp.logical_and(same, rr < cc))

    units = []
    for cb in range(n_cb):
        ts = slice(cb * c, (cb + 1) * c)
        sm = sm_ref[ts, :]
        beta_all = jax.nn.sigmoid(sm)
        g_all = -jnp.exp(cst_ref[0:1, :]) * _softplus(sm + cst_ref[1:2, :])
        tot_all = jnp.sum(g_all, axis=0, keepdims=True)
        qs, ks, vs = [], [], []
        for h in range(nh):
            hs = slice(h * DN_HEAD, (h + 1) * DN_HEAD)
            qh = _silu(q_ref[ts, hs])
            kh = _silu(k_ref[ts, hs])
            qs.append(qh * lax.rsqrt(jnp.sum(qh * qh, axis=-1, keepdims=True) + EPS) * (DN_HEAD ** -0.5))
            ks.append(kh * lax.rsqrt(jnp.sum(kh * kh, axis=-1, keepdims=True) + EPS))
            vs.append(_silu(v_ref[ts, hs]))
        q_rows = jnp.concatenate(qs, axis=0)
        k_rows = jnp.concatenate(ks, axis=0)
        v_rows = jnp.concatenate(vs, axis=0)
        qkt = _dotp(q_rows, k_rows, NT)
        for d in range(2):
            gc = _dot(tri[d], g_all, HI)
            gct = jnp.concatenate([gc, zpad], axis=0).T
            bis = [d * nh + h for h in range(nh)]
            gis = [2 * nh + b for b in bis]
            beta = jnp.concatenate([beta_all[:, b:b + 1] for b in bis], axis=0)
            gcol = jnp.concatenate([gc[:, g:g + 1] for g in gis], axis=0)
            grow = jnp.concatenate([gct[g:g + 1, :c] for g in gis], axis=1)
            gtot = jnp.concatenate([jnp.broadcast_to(tot_all[:, g:g + 1], (c, 1)) for g in gis], axis=0)
            gamma = jnp.where(incl[d], jnp.exp(jnp.where(incl[d], gcol - grow, 0.0)), 0.0)
            egc = jnp.exp(gcol)
            kb = k_rows * beta
            units.append(dict(
                cb=cb, d=d, tot=tot_all, qdec=q_rows * egc, kd=k_rows * jnp.exp(gtot - gcol), qk=qkt * gamma,
                pw=-jnp.where(strict[d], _dotp(kb, k_rows, NT) * gamma, 0.0),
                sol=jnp.concatenate([v_rows * beta, kb * egc], axis=1)))
    steps = int(math.log2(c))
    for t in range(steps):
        for un in units:
            un["sol"] = un["sol"] + _dotp(un["pw"], un["sol"], passes=DN_CHAIN_PASSES)
        if t + 1 < steps:
            for un in units:
                un["pw"] = _dotp(un["pw"], un["pw"], passes=DN_CHAIN_PASSES)
    for un in units:
        d, cb = un["d"], un["cb"]
        ts = slice(cb * c, (cb + 1) * c)
        for h in range(nh):
            hs = slice(h * DN_HEAD, (h + 1) * DN_HEAD)
            rs = slice(h * c, (h + 1) * c)
            gi = 2 * nh + d * nh + h
            wd = nh * DN_HEAD

            def put(k, val):
                pk_ref[d, ts, k * wd + h * DN_HEAD:k * wd + (h + 1) * DN_HEAD] = val

            put(DN_U, un["sol"][rs, :DN_HEAD])
            put(DN_W, un["sol"][rs, DN_HEAD:])
            put(DN_QD, un["qdec"][rs])
            put(DN_KD, un["kd"][rs])
            pk_ref[d, ts, DN_QK * wd + h * c:DN_QK * wd + (h + 1) * c] = un["qk"][rs, rs]
            last_ref[d, cb, :, hs] = jnp.broadcast_to(jnp.exp(un["tot"][:, gi:gi + 1]), (8, DN_HEAD))


def _dn_intra(p, cst):
    s_tot = p.shape[0]
    c = DN_CHUNK
    w = 512
    nch = s_tot // c
    n_cb = DN_INTRA_CHUNKS
    rows = n_cb * c
    assert nch % n_cb == 0
    pw = 4 * w + w // 2
    return pl.pallas_call(
        _dn_intra_kernel,
        name="dn_intra",
        out_shape=(jax.ShapeDtypeStruct((2, s_tot, pw), F32), jax.ShapeDtypeStruct((2, nch, 8, w), F32)),
        grid=(nch // n_cb,),
        in_specs=[pl.BlockSpec((rows, w), lambda j: (j, 3)),
                  pl.BlockSpec((rows, w), lambda j: (j, 4)),
                  pl.BlockSpec((rows, w), lambda j: (j, 5)),
                  pl.BlockSpec((rows, 128), lambda j: (j, 46)),
                  pl.BlockSpec((8, 128), lambda j: (0, 0))],
        out_specs=(pl.BlockSpec((2, rows, pw), lambda j: (0, j, 0)),
                   pl.BlockSpec((2, n_cb, 8, w), lambda j: (0, j, 0, 0))),
        compiler_params=_cp("parallel"),
    )(p, p, p, p, cst)


def _dn_rec_kernel(pkf_ref, lastf_ref, pkb_ref, lastb_ref, of_ref, ob_ref, s_ref, *, nh):
    pk_refs, last_refs, o_refs = (pkf_ref, pkb_ref), (lastf_ref, lastb_ref), (of_ref, ob_ref)

    @pl.when(pl.program_id(0) == 0)
    def _():
        s_ref[...] = jnp.zeros_like(s_ref)

    c = DN_CHUNK
    width = nh * DN_HEAD
    chains = [(d, h) for d in range(2) for h in range(nh)]
    part = lambda d, k, h: pk_refs[d][0, :, k * width + h * DN_HEAD:k * width + (h + 1) * DN_HEAD]
    s = [s_ref[d * nh + h] for d, h in chains]
    v_new = [part(d, DN_U, h) - _dotp(part(d, DN_W, h), s[i]) for i, (d, h) in enumerate(chains)]
    qs = [_dotp(part(d, DN_QD, h), s[i]) for i, (d, h) in enumerate(chains)]
    for i, (d, h) in enumerate(chains):
        qk = pk_refs[d][0, :, DN_QK * width + h * c:DN_QK * width + (h + 1) * c]
        o_refs[d][:, h * DN_HEAD:(h + 1) * DN_HEAD] = qs[i] + _dotp(qk, v_new[i])
    for i, (d, h) in enumerate(chains):
        decay = last_refs[d][0, 0, 0:1, h * DN_HEAD:(h + 1) * DN_HEAD]
        s_ref[d * nh + h] = s[i] * decay + _dotp(part(d, DN_KD, h), v_new[i], TN)


def _dn_recurrence(pk, last, *, ctx):
    _, s_tot, pw = pk.shape
    width = last.shape[-1]
    c = DN_CHUNK
    nch = s_tot // c
    nctx = ctx // c
    nh = width // DN_HEAD
    fwd = lambda j: j
    bwd = lambda j: jnp.where(j < nctx, nctx - 1 - j, nch - 1 - (j - nctx))
    specs = []
    for d, cm in ((0, fwd), (1, bwd)):
        specs += [pl.BlockSpec((1, c, pw), lambda j, cm=cm, d=d: (d, cm(j), 0)),
                  pl.BlockSpec((1, 1, 8, width), lambda j, cm=cm, d=d: (d, cm(j), 0, 0))]
    out = jax.ShapeDtypeStruct((s_tot, width), F32)
    return pl.pallas_call(
        functools.partial(_dn_rec_kernel, nh=nh),
        name="dn_scan",
        out_shape=(out, out),
        grid=(nch,),
        in_specs=specs,
        out_specs=(pl.BlockSpec((c, width), lambda j: (fwd(j), 0)), pl.BlockSpec((c, width), lambda j: (bwd(j), 0))),
        scratch_shapes=[pltpu.VMEM((2 * nh, DN_HEAD, DN_HEAD), F32)],
        compiler_params=_cp("arbitrary"),
    )(pk, last, pk, last)


def _dn_finish_kernel(of_ref, ob_ref, gate_ref, gain_ref, o_ref):
    nh = of_ref.shape[1] // DN_HEAD
    for h in range(nh):
        hs = slice(h * DN_HEAD, (h + 1) * DN_HEAD)
        o = of_ref[:, hs] + ob_ref[:, hs]
        y = o * lax.rsqrt(jnp.mean(o * o, axis=-1, keepdims=True) + EPS) * gain_ref[...]
        o_ref[:, hs] = (y * _silu(gate_ref[:, hs])).astype(o_ref.dtype)


def _dn_finish(o_f, o_b, p, gain):
    s_tot, width = o_f.shape
    tm = _row_tile(s_tot, (528, 640, 256, 128))
    blk = pl.BlockSpec((tm, width), lambda i: (i, 0))
    return pl.pallas_call(
        _dn_finish_kernel,
        name="dn_finish",
        out_shape=jax.ShapeDtypeStruct((s_tot, width), BF16),
        grid=(s_tot // tm,),
        in_specs=[blk, blk, pl.BlockSpec((tm, width), lambda i: (i, 6)), pl.BlockSpec((1, DN_HEAD), lambda i: (0, 0))],
        out_specs=blk,
        compiler_params=_cp("parallel"),
    )(o_f, o_b, p, gain)


def _rope_tables(ctx, s_lat):
    rows = s_lat // GRID_W
    n_freq = HEAD_D // 4
    row = np.repeat(np.arange(rows, dtype=np.float32), GRID_W)
    col = np.tile(np.arange(GRID_W, dtype=np.float32), rows)
    inv = (ROPE_THETA ** (-np.arange(n_freq, dtype=np.float32) / n_freq)).astype(np.float32)
    ang = np.concatenate([row[:, None] * inv, col[:, None] * inv], axis=-1).astype(np.float32)
    cos, sin = np.cos(ang), np.sin(ang)
    cos = np.concatenate([np.ones((ctx, HEAD_D // 2), np.float32), cos], axis=0)
    sin = np.concatenate([np.zeros((ctx, HEAD_D // 2), np.float32), sin], axis=0)
    cos_t = np.tile(np.concatenate([cos, cos], axis=1), (1, 2))
    sin_t = np.tile(np.concatenate([-sin, sin], axis=1), (1, 2))
    return jnp.asarray(cos_t, F32), jnp.asarray(sin_t, F32)


def _pad_rows(a, rows):
    return jnp.concatenate([a, jnp.zeros((rows - a.shape[0],) + a.shape[1:], a.dtype)], axis=0)


def kernel(x, c, ctx, c_ctx, w_ada, b_ada, norm_mix_pre, norm_mix_post, norm_ffn_pre, norm_ffn_post, w_in, w_out, attn_q_norm, attn_k_norm, hy_short, hy_w1, hy_b1, hy_w2, hy_b2, hy_w3, hy_b3, hy_w4, hy_freq, hy_skip, dn_short, dn_a_log, dn_dt_bias, dn_norm, df_lambda, df_norm, ffn_up, ffn_conv, ffn_down):
    batch, s_lat, d = x.shape
    n_ctx = ctx.shape[1]
    depth = w_in.shape[0]
    gw = d // 4
    assert batch == 1 and gw == 512 and n_ctx % 256 == 0 and s_lat % 512 == 0
    s_tot = n_ctx + s_lat

    xx = jnp.concatenate([ctx[0], x[0]], axis=0)
    c_rows = _pad_rows(jnp.concatenate([c, c_ctx[None, :]], axis=0), 8)
    mods = _mod_vectors(c_rows, w_ada, b_ada)

    cos_t, sin_t = _rope_tables(n_ctx, s_lat)
    lane = np.arange(128)
    gmat = jnp.asarray((lane[:, None] // HEAD_D == lane[None, :] // HEAD_D) / HEAD_D, F32)
    ident = jnp.asarray([[0.0], [1.0], [0.0]], F32)

    a_cols = gw + 2 * 2 * HEAD_D
    o_hy, o_dn, o_df = a_cols, a_cols + 3 * gw, a_cols + 3 * gw + 4 * gw + 16
    order = [(o_hy, o_hy + 3 * gw), (o_dn, o_dn + 4 * gw), (o_df, o_df + 3 * gw), (0, a_cols),
             (o_dn + 4 * gw, o_dn + 4 * gw + 16)]
    n_proj = 12 * gw
    used = sum(b - a for a, b in order)

    for i in range(depth):
        lam_init = 0.8 - 0.6 * math.exp(-0.3 * i)
        mod = mods[i]
        sh1, sc1, gt1, sh2, sc2, gt2 = (mod[:, k * d:(k + 1) * d] for k in range(6))
        mod_mix = _pad_rows(jnp.concatenate([sh1[0:1], sc1[0:1], sh1[1:2], sc1[1:2]], axis=0), 8)
        mod_ffn = _pad_rows(jnp.concatenate([sh2[0:1], sc2[0:1], sh2[1:2], sc2[1:2]], axis=0), 8)

        w_in_p = jnp.concatenate([w_in[i][:, a:b] for a, b in order]
                                 + [jnp.zeros((d, n_proj - used), F32)], axis=1).astype(BF16)
        cw_in = jnp.concatenate([hy_short[i], dn_short[i], jnp.broadcast_to(ident, (3, n_proj - 6 * gw))], axis=1)
        p = _norm_mod_matmul_conv(xx, norm_mix_pre[i][None, :], mod_mix, w_in_p, _pad_rows(cw_in, 8),
                                  ctx=n_ctx, ffn=False, conv_cols=6 * gw)

        qg, kg, vgt, qd, kd, vdt = _attn_prep(p, jnp.tile(attn_q_norm[i], 2)[None, :],
                                              jnp.tile(attn_k_norm[i], 2)[None, :], cos_t, sin_t, gmat)
        ya = _gqa_attention(qg, kg, vgt, ctx=n_ctx)
        yd = _diff_attention(qd, kd, vdt, df_lambda[i], df_norm[i][None, :], ctx=n_ctx, lam_init=lam_init)

        filt = (_pad_rows(hy_w1[i], 40).T, hy_b1[i][:, None], hy_w2[i].T, hy_b2[i][:, None], hy_w3[i].T,
                hy_b3[i][:, None], hy_w4[i], hy_freq[i][:, None])
        skip = hy_skip[i][None, :]
        yb_lat = _hyena_latent(p, _hy_filter(filt, s_lat, 2 * s_lat), skip, ctx=n_ctx)
        if i < depth - 1:
            yb_ctx = _hy_ctx(p, _hy_filter(filt, n_ctx, 2 * n_ctx), skip, ctx=n_ctx)
        else:
            yb_ctx = jnp.zeros((n_ctx, gw), BF16)

        zeros8 = jnp.zeros((8,), F32)
        cst = _pad_rows(jnp.stack([jnp.concatenate([zeros8, dn_a_log[i].reshape(-1), jnp.zeros((112,), F32)]),
                                   jnp.concatenate([zeros8, dn_dt_bias[i].reshape(-1), jnp.zeros((112,), F32)])]), 8)
        o_f, o_b = _dn_recurrence(*_dn_intra(p, cst), ctx=n_ctx)
        yc = _dn_finish(o_f, o_b, p, dn_norm[i][None, :])

        xx = _out_proj(ya, yb_ctx, yb_lat, yc, yd, w_out[i].astype(BF16), xx, gt1, norm_mix_post[i][None, :],
                       ctx=n_ctx)

        g = _norm_mod_matmul_conv(xx, norm_ffn_pre[i][None, :], mod_ffn, ffn_up[i].astype(BF16),
                                  _pad_rows(ffn_conv[i], 8), ctx=n_ctx, ffn=True, conv_cols=ffn_down.shape[1])
        xx = _matmul_norm_residual(g, ffn_down[i].astype(BF16), xx, gt2, norm_ffn_post[i][None, :], ctx=n_ctx)

    return xx[n_ctx:][None]
```

```python
import functools
import math

import jax
import jax.numpy as jnp
import numpy as np
from jax import lax
from jax.experimental import pallas as pl
from jax.experimental.pallas import tpu as pltpu

F32 = jnp.float32
BF16 = jnp.bfloat16
HI = lax.Precision.HIGHEST
EPS = 1e-6

GRID_W = 64
ROPE_THETA = 10000.0
HEAD_D = 64
GQA_GROUP = 4
DN_HEAD = 128
DN_CHUNK = 64
DN_SCAN_CHUNKS = 2
DN_INTRA_CHUNKS = 2
DN_CHAIN_PASSES = 3
HY_BANDS = 16
HY_FAST_DECAY, HY_SLOW_DECAY, HY_TARGET = 0.3, 1.5, 1e-2
FFT_N2 = 128
ATT_TK = 256
DF_HPS = 2
LOG2E = 1.4426950408889634
VMEM_LIMIT = 56 * 1024 * 1024
NT = (((1,), (1,)), ((), ()))
TN = (((0,), (0,)), ((), ()))


def _cp(*sem):
    return pltpu.CompilerParams(dimension_semantics=sem, vmem_limit_bytes=VMEM_LIMIT)


def _dot(a, b, precision=None):
    return jnp.dot(a, b, precision=precision, preferred_element_type=F32)


def _split_bf16(a):
    hi = a.astype(BF16)
    return hi, (a - hi.astype(F32)).astype(BF16)


def _mm3(a, b):
    return _dot(a[0], b[0]) + (_dot(a[1], b[0]) + _dot(a[0], b[1]))


def _silu(x):
    return x * jax.nn.sigmoid(x)


def _row_tile(n, cands):
    for t in cands:
        if n % t == 0:
            return t
    raise ValueError(f"no row tile for {n}")


def _mod_kernel(c_ref, w_ref, b_ref, o_ref):
    o_ref[0] = _mm3(_split_bf16(_silu(c_ref[...])), _split_bf16(w_ref[0])) + b_ref[0]


def _mod_vectors(c_rows, w_ada, b_ada):
    depth, d, n = w_ada.shape
    tn = 512
    return pl.pallas_call(
        _mod_kernel,
        name="mod_vectors",
        out_shape=jax.ShapeDtypeStruct((depth, 8, n), F32),
        grid=(depth, n // tn),
        in_specs=[pl.BlockSpec((8, d), lambda l, j: (0, 0)),
                  pl.BlockSpec((1, d, tn), lambda l, j: (l, 0, j)),
                  pl.BlockSpec((1, 1, tn), lambda l, j: (l, 0, j))],
        out_specs=pl.BlockSpec((1, 8, tn), lambda l, j: (l, 0, j)),
        compiler_params=_cp("parallel", "parallel"),
    )(c_rows, w_ada, b_ada.reshape(depth, 1, n))


HALO = 16


def _k1_kernel(xp_ref, xm_ref, xn_ref, g_ref, mod_ref, *rest, tm, ctx, s_tot, ffn, n_conv):
    if ffn:
        wa_ref, wb_ref, cwa_ref, cwb_ref, o_ref, h_ref, *acc_refs = rest
    else:
        wa_ref, cwa_ref, o_ref, h_ref, *acc_refs = rest
    i = pl.program_id(0)
    j = pl.program_id(1)

    @pl.when(j == 0)
    def _():
        def nm(xv, row0):
            ms = jnp.mean(xv * xv, axis=-1, keepdims=True)
            y = xv * lax.rsqrt(ms + EPS) * g_ref[...]
            rows = row0 + lax.broadcasted_iota(jnp.int32, (xv.shape[0], 1), 0)
            is_ctx = rows < ctx
            sh = jnp.where(is_ctx, mod_ref[2:3, :], mod_ref[0:1, :])
            sc = jnp.where(is_ctx, mod_ref[3:4, :], mod_ref[1:2, :])
            return (y * (1.0 + sc) + sh).astype(BF16)

        h_ref[0:HALO] = nm(xp_ref[...], i * tm - HALO)
        n_sub = 3 if (tm > 640 and tm % 48 == 0) else 1
        rc = tm // n_sub
        for r0 in range(0, tm, rc):
            h_ref[HALO + r0:HALO + r0 + rc] = nm(xm_ref[r0:r0 + rc], i * tm + r0)
        h_ref[HALO + tm:] = nm(xn_ref[...], i * tm + tm)

    def conv(w_ref, cw_ref, edge, acc_ref):
        acc_ref[...] = _dot(h_ref[...], w_ref[...])
        prev = acc_ref[HALO - 1:HALO - 1 + tm]
        nxt = acc_ref[HALO + 1:HALO + 1 + tm]
        if edge:
            rows = i * tm + lax.broadcasted_iota(jnp.int32, (tm, 1), 0)
            prev = jnp.where(jnp.logical_and(rows != 0, rows != ctx), prev, 0.0)
            nxt = jnp.where(jnp.logical_and(rows != ctx - 1, rows != s_tot - 1), nxt, 0.0)
        cw = cw_ref[...]
        return acc_ref[HALO:HALO + tm] * cw[1:2] + prev * cw[0:1] + nxt * cw[2:3]

    def emit(edge):
        if ffn:
            a = conv(wa_ref, cwa_ref, edge, acc_refs[0])
            b = conv(wb_ref, cwb_ref, edge, acc_refs[1])
            o_ref[...] = (_silu(a) * b).astype(o_ref.dtype)
        else:
            o_ref[...] = conv(wa_ref, cwa_ref, edge, acc_refs[0]).astype(o_ref.dtype)

    edge_tiles = sorted({r // tm for r in (0, ctx - 1, ctx, s_tot - 1)})
    is_edge = functools.reduce(jnp.logical_or, [i == e for e in edge_tiles])
    is_conv = j < n_conv
    pl.when(jnp.logical_and(is_conv, is_edge))(lambda: emit(True))
    pl.when(jnp.logical_and(is_conv, jnp.logical_not(is_edge)))(lambda: emit(False))

    if not ffn:
        @pl.when(jnp.logical_not(is_conv))
        def _():
            o_ref[...] = _dot(h_ref[HALO:HALO + tm], wa_ref[...]).astype(o_ref.dtype)


def _norm_mod_matmul_conv(xx, gain, mod4, w, cw, *, ctx, ffn, conv_cols):
    s_tot, d = xx.shape
    n = w.shape[1]
    tm = _row_tile(s_tot, (1056, 640, 256, 128))
    tn = 512
    nb = s_tot // HALO
    n_out = n // 2 if ffn else n
    x_specs = [
        pl.BlockSpec((HALO, d), lambda i, j: (jnp.maximum(i * (tm // HALO) - 1, 0), 0)),
        pl.BlockSpec((tm, d), lambda i, j: (i, 0)),
        pl.BlockSpec((HALO, d), lambda i, j: (jnp.minimum((i + 1) * (tm // HALO), nb - 1), 0)),
        pl.BlockSpec((1, d), lambda i, j: (0, 0)),
        pl.BlockSpec((8, d), lambda i, j: (0, 0)),
    ]
    if ffn:
        off = n_out // tn
        w_specs = [pl.BlockSpec((d, tn), lambda i, j: (0, j)),
                   pl.BlockSpec((d, tn), lambda i, j: (0, j + off)),
                   pl.BlockSpec((8, tn), lambda i, j: (0, j)),
                   pl.BlockSpec((8, tn), lambda i, j: (0, j + off))]
        args = (w, w, cw, cw)
        out_dtype = BF16
    else:
        w_specs = [pl.BlockSpec((d, tn), lambda i, j: (0, j)),
                   pl.BlockSpec((8, tn), lambda i, j: (0, j))]
        args = (w, cw)
        out_dtype = F32
    return pl.pallas_call(
        functools.partial(_k1_kernel, tm=tm, ctx=ctx, s_tot=s_tot, ffn=ffn, n_conv=conv_cols // tn),
        name="ffn_up_conv_gate" if ffn else "in_proj_conv",
        out_shape=jax.ShapeDtypeStruct((s_tot, n_out), out_dtype),
        grid=(s_tot // tm, n_out // tn),
        in_specs=x_specs + w_specs,
        out_specs=pl.BlockSpec((tm, tn), lambda i, j: (i, j)),
        scratch_shapes=[pltpu.VMEM((tm + 2 * HALO, d), BF16)]
        + [pltpu.VMEM((tm + 2 * HALO, tn), F32)] * (2 if ffn else 1),
        compiler_params=_cp("parallel", "arbitrary"),
    )(xx, xx, xx, gain, mod4, *args)


def _k2_kernel(a_ref, w_ref, x_ref, gate_ref, gain_ref, o_ref, *, tm, ctx, nk):
    i = pl.program_id(0)
    k = pl.program_id(1)

    part = _dot(a_ref[...], w_ref[...])

    def finish(y):
        r = y * lax.rsqrt(jnp.mean(y * y, axis=-1, keepdims=True) + EPS) * gain_ref[...]
        rows = i * tm + lax.broadcasted_iota(jnp.int32, (tm, 1), 0)
        gt = jnp.where(rows < ctx, gate_ref[1:2, :], gate_ref[0:1, :])
        o_ref[...] = x_ref[...] + gt * r

    if nk == 1:
        finish(part)
        return

    @pl.when(k == 0)
    def _():
        o_ref[...] = part

    @pl.when(jnp.logical_and(k > 0, k < nk - 1))
    def _():
        o_ref[...] += part

    @pl.when(k == nk - 1)
    def _():
        finish(o_ref[...] + part)


def _matmul_norm_residual(a, w, xx, gate2, gain, *, ctx):
    s_tot, kdim = a.shape
    d = w.shape[1]
    tm = _row_tile(s_tot, (704, 640, 256, 128))
    nk = 1 if kdim <= 2048 else 4
    tk = kdim // nk
    assert tk % 128 == 0
    return pl.pallas_call(
        functools.partial(_k2_kernel, tm=tm, ctx=ctx, nk=nk),
        name="matmul_norm_residual",
        out_shape=jax.ShapeDtypeStruct((s_tot, d), F32),
        grid=(s_tot // tm, nk),
        in_specs=[pl.BlockSpec((tm, tk), lambda i, k: (i, k)),
                  pl.BlockSpec((tk, d), lambda i, k: (k, 0)),
                  pl.BlockSpec((tm, d), lambda i, k: (i, 0)),
                  pl.BlockSpec((8, d), lambda i, k: (0, 0)),
                  pl.BlockSpec((1, d), lambda i, k: (0, 0))],
        out_specs=pl.BlockSpec((tm, d), lambda i, k: (i, 0)),
        compiler_params=_cp("parallel", "arbitrary"),
    )(a, w, xx, gate2, gain)


def _out_proj_kernel(ya_ref, ybc_ref, ybl_ref, yc_ref, yd_ref, w_ref, x_ref, gate_ref, gain_ref, o_ref, *, n_ctx_tiles):
    is_ctx = pl.program_id(0) < n_ctx_tiles
    gw = ya_ref.shape[1]
    yb = jnp.where(is_ctx, ybc_ref[...], ybl_ref[...])
    y = sum(_dot(a, w_ref[g * gw:(g + 1) * gw, :])
            for g, a in enumerate((ya_ref[...], yb, yc_ref[...], yd_ref[...])))
    r = y * lax.rsqrt(jnp.mean(y * y, axis=-1, keepdims=True) + EPS) * gain_ref[...]
    gt = jnp.where(is_ctx, gate_ref[1:2, :], gate_ref[0:1, :])
    o_ref[...] = x_ref[...] + gt * r


def _out_proj(ya, yb_ctx, yb_lat, yc, yd, w, xx, gate2, gain, *, ctx):
    s_tot, d = xx.shape
    gw = ya.shape[1]
    tm = ATT_TK
    nct = ctx // tm
    blk = pl.BlockSpec((tm, gw), lambda i: (i, 0))
    return pl.pallas_call(
        functools.partial(_out_proj_kernel, n_ctx_tiles=nct),
        name="out_proj",
        out_shape=jax.ShapeDtypeStruct((s_tot, d), F32),
        grid=(s_tot // tm,),
        in_specs=[blk,
                  pl.BlockSpec((tm, gw), lambda i: (jnp.minimum(i, nct - 1), 0)),
                  pl.BlockSpec((tm, gw), lambda i: (jnp.maximum(i - nct, 0), 0)),
                  blk, blk,
                  pl.BlockSpec(w.shape, lambda i: (0, 0)),
                  pl.BlockSpec((tm, d), lambda i: (i, 0)),
                  pl.BlockSpec((8, d), lambda i: (0, 0)),
                  pl.BlockSpec((1, d), lambda i: (0, 0))],
        out_specs=pl.BlockSpec((tm, d), lambda i: (i, 0)),
        compiler_params=_cp("parallel"),
    )(ya, yb_ctx, yb_lat, yc, yd, w, xx, gate2, gain)


def _attn_prep_kernel(aq_ref, akv_ref, dq_ref, dk_ref, dv_ref, qgain_ref, kgain_ref, cos_ref, sin_ref,
                      gmat_ref, qg_ref, kg_ref, vgt_ref, qd_ref, kd_ref, vdt_ref):
    cosf = cos_ref[...]
    sins = sin_ref[...]
    lane = lax.broadcasted_iota(jnp.int32, (1, 128), 1)
    first_half = (lane & (HEAD_D - 1)) < (HEAD_D // 2)
    scale = HEAD_D ** -0.5 * LOG2E

    def rope(x):
        partner = jnp.where(first_half, pltpu.roll(x, 128 - HEAD_D // 2, 1), pltpu.roll(x, HEAD_D // 2, 1))
        return x * cosf + partner * sins

    def headnorm(x, gain):
        g = gmat_ref[...].astype(BF16)
        hi, lo = _split_bf16(x * x)
        ms = _dot(hi, g) + _dot(lo, g)
        return x * lax.rsqrt(ms + EPS) * gain

    def put(ref, pair, val):
        vb = val.astype(ref.dtype)
        ref[2 * pair] = vb[:, :HEAD_D]
        ref[2 * pair + 1] = vb[:, HEAD_D:]

    for ci in range(4):
        sl = slice(ci * 128, (ci + 1) * 128)
        put(qg_ref, ci, rope(headnorm(aq_ref[:, sl], qgain_ref[...])) * scale)
        put(qd_ref, ci, rope(dq_ref[:, sl]) * scale)
        put(kd_ref, ci, rope(dk_ref[:, sl]))
        vdt_ref[ci, 0] = dv_ref[:, sl].T.astype(vdt_ref.dtype)
    put(kg_ref, 0, rope(headnorm(akv_ref[:, 0:128], kgain_ref[...])))
    vt = akv_ref[:, 128:256].T.astype(vgt_ref.dtype)
    vgt_ref[0, 0] = vt[:HEAD_D]
    vgt_ref[1, 0] = vt[HEAD_D:]


def _attn_prep(p, qgain, kgain, cos_t, sin_t, gmat):
    s_tot = p.shape[0]
    tm = ATT_TK
    hm = lambda nh, dd: jax.ShapeDtypeStruct((nh, s_tot, dd), BF16)
    hspec = lambda nh, dd: pl.BlockSpec((nh, tm, dd), lambda i: (0, i, 0))
    vt = lambda nh, dd: jax.ShapeDtypeStruct((nh, s_tot // tm, dd, tm), BF16)
    vtspec = lambda nh, dd: pl.BlockSpec((nh, 1, dd, tm), lambda i: (0, i, 0, 0))
    return pl.pallas_call(
        _attn_prep_kernel,
        name="attn_prep",
        out_shape=(hm(8, 64), hm(2, 64), vt(2, 64), hm(8, 64), hm(8, 64), vt(4, 128)),
        grid=(s_tot // tm,),
        in_specs=[pl.BlockSpec((tm, 512), lambda i: (i, 10)),
                  pl.BlockSpec((tm, 256), lambda i: (i, 22)),
                  pl.BlockSpec((tm, 512), lambda i: (i, 7)),
                  pl.BlockSpec((tm, 512), lambda i: (i, 8)),
                  pl.BlockSpec((tm, 512), lambda i: (i, 9)),
                  pl.BlockSpec((1, 128), lambda i: (0, 0)),
                  pl.BlockSpec((1, 128), lambda i: (0, 0)),
                  pl.BlockSpec((tm, 128), lambda i: (i, 0)),
                  pl.BlockSpec((tm, 128), lambda i: (i, 0)),
                  pl.BlockSpec((128, 128), lambda i: (0, 0))],
        out_specs=(hspec(8, 64), hspec(2, 64), vtspec(2, 64), hspec(8, 64), hspec(8, 64), vtspec(4, 128)),
        compiler_params=_cp("parallel"),
    )(p, p, p, p, p, qgain, kgain, cos_t, sin_t, gmat)


def _flash_t(pairs, vt_ref, n_val, is_ctx_tile, n_ctx_chunks, n_chunks, m_ref, l_ref, acc_ref, st_ref):
    tk = ATT_TK
    m_ref[...] = jnp.full_like(m_ref, -jnp.inf)
    l_ref[...] = jnp.zeros_like(l_ref)
    acc_ref[...] = jnp.zeros_like(acc_ref)

    n = jnp.where(is_ctx_tile, n_ctx_chunks, n_chunks)

    def scores(c, slot):
        off = pl.multiple_of(c * tk, tk)
        parts = [lax.dot_general(k_ref[kidx, pl.ds(off, tk), :], q, NT, preferred_element_type=F32)
                 for q, k_ref, kidx in pairs]
        st_ref[slot] = parts[0] if len(parts) == 1 else jnp.concatenate(parts, axis=1)

    def update(c, slot):
        st = st_ref[slot]
        m_prev = m_ref[...]
        m_new = jnp.maximum(m_prev, jnp.max(st, axis=0, keepdims=True))
        alpha = jnp.exp2(m_prev - m_new)
        pt = jnp.exp2(st - m_new)
        l_ref[...] = alpha * l_ref[...] + jnp.sum(pt, axis=0, keepdims=True)
        pb = pt.astype(BF16)
        w = pb.shape[1] // n_val
        for g in range(n_val):
            sl = slice(g * w, (g + 1) * w)
            acc_ref[:, sl] = alpha[:, sl] * acc_ref[:, sl] + _dot(vt_ref[g, c], pb[:, sl])
        m_ref[...] = m_new

    scores(0, 0)

    def body(it, carry):
        c = 2 * it
        scores(c + 1, 1)
        update(c, 0)
        scores(jnp.minimum(c + 2, n - 1), 0)
        update(c + 1, 1)
        return carry

    lax.fori_loop(0, n // 2, body, 0)

    @pl.when(n % 2 == 1)
    def _():
        update(n - 1, 0)

    return acc_ref[...] / l_ref[...]


def _gqa_kernel(q_ref, k_ref, vt_ref, o_ref, m_ref, l_ref, acc_ref, st_ref, *, tq, ctx, s_tot):
    q = q_ref[...].reshape(GQA_GROUP * tq, HEAD_D)
    ot = _flash_t([(q, k_ref, 0)], vt_ref, 1, pl.program_id(1) < ctx // tq, ctx // ATT_TK, s_tot // ATT_TK,
                  m_ref, l_ref, acc_ref, st_ref)
    o2 = jnp.concatenate([ot[:, h * tq:(h + 1) * tq] for h in range(GQA_GROUP)], axis=0)
    o_ref[...] = o2.T.astype(o_ref.dtype)


def _gqa_attention(qg, kg, vgt, *, ctx):
    nh, s_tot, _ = qg.shape
    tq = ATT_TK
    m = GQA_GROUP * tq
    return pl.pallas_call(
        functools.partial(_gqa_kernel, tq=tq, ctx=ctx, s_tot=s_tot),
        name="gqa_attention",
        out_shape=jax.ShapeDtypeStruct((s_tot, nh * HEAD_D), BF16),
        grid=(nh // GQA_GROUP, s_tot // tq),
        in_specs=[pl.BlockSpec((GQA_GROUP, tq, HEAD_D), lambda g, i: (g, i, 0)),
                  pl.BlockSpec((1, s_tot, HEAD_D), lambda g, i: (g, 0, 0)),
                  pl.BlockSpec((1, s_tot // ATT_TK, HEAD_D, ATT_TK), lambda g, i: (g, 0, 0, 0))],
        out_specs=pl.BlockSpec((tq, GQA_GROUP * HEAD_D), lambda g, i: (i, g)),
        scratch_shapes=[pltpu.VMEM((1, m), F32), pltpu.VMEM((1, m), F32), pltpu.VMEM((HEAD_D, m), F32),
                        pltpu.VMEM((2, ATT_TK, m), F32)],
        compiler_params=_cp("parallel", "parallel"),
    )(qg, kg, vgt)


def _diff_kernel(q_ref, k_ref, vt_ref, lam_ref, gain_ref, o_ref, m_ref, l_ref, acc_ref, st_ref, *, tq, ctx, s_tot,
                 lam_init):
    pairs = [(q_ref[j], k_ref, j) for j in range(2 * DF_HPS)]
    ot = _flash_t(pairs, vt_ref, DF_HPS, pl.program_id(1) < ctx // tq, ctx // ATT_TK, s_tot // ATT_TK,
                  m_ref, l_ref, acc_ref, st_ref)
    lv = lam_ref[...]
    lam = (jnp.exp(jnp.sum(lv[0:1] * lv[1:2], axis=1, keepdims=True))
           - jnp.exp(jnp.sum(lv[2:3] * lv[3:4], axis=1, keepdims=True)) + lam_init)
    dv = ot.shape[0]
    for g in range(DF_HPS):
        o = (ot[:, 2 * g * tq:(2 * g + 1) * tq] - lam * ot[:, (2 * g + 1) * tq:(2 * g + 2) * tq]).T
        y = o * lax.rsqrt(jnp.mean(o * o, axis=-1, keepdims=True) + EPS) * gain_ref[...]
        o_ref[:, g * dv:(g + 1) * dv] = (y * (1.0 - lam_init)).astype(o_ref.dtype)


def _diff_attention(qd, kd, vdt, lam_vecs, sub_gain, *, ctx, lam_init):
    nh, n_chunks, dv, _ = vdt.shape
    s_tot = qd.shape[1]
    tq = ATT_TK
    g = DF_HPS
    m = 2 * g * tq
    return pl.pallas_call(
        functools.partial(_diff_kernel, tq=tq, ctx=ctx, s_tot=s_tot, lam_init=lam_init),
        name="diff_attention",
        out_shape=jax.ShapeDtypeStruct((s_tot, nh * dv), BF16),
        grid=(nh // g, s_tot // tq),
        in_specs=[pl.BlockSpec((2 * g, tq, HEAD_D), lambda h, i: (h, i, 0)),
                  pl.BlockSpec((2 * g, s_tot, HEAD_D), lambda h, i: (h, 0, 0)),
                  pl.BlockSpec((g, n_chunks, dv, ATT_TK), lambda h, i: (h, 0, 0, 0)),
                  pl.BlockSpec(lam_vecs.shape, lambda h, i: (0, 0)),
                  pl.BlockSpec((1, dv), lambda h, i: (0, 0))],
        out_specs=pl.BlockSpec((tq, g * dv), lambda h, i: (i, h)),
        scratch_shapes=[pltpu.VMEM((1, m), F32), pltpu.VMEM((1, m), F32), pltpu.VMEM((dv, m), F32),
                        pltpu.VMEM((2, ATT_TK, m), F32)],
        compiler_params=_cp("parallel", "parallel"),
    )(qd, kd, vdt, lam_vecs, sub_gain)


def _hy_filter_kernel(w1_ref, b1_ref, w2_ref, b2_ref, w3_ref, b3_ref, w4_ref, fr_ref, o_ref, *, tp, l, n, width):
    i = pl.program_id(0)
    def position(idx):
        pos = jnp.where(idx < l, idx, n - idx).astype(F32)
        return pos / (l - 1.0), (2.0 * math.pi) * pos / l

    t_l, w_l = position(i * tp + lax.broadcasted_iota(jnp.int32, (1, tp), 1))
    nf = w1_ref.shape[1]
    r = lax.broadcasted_iota(jnp.int32, (nf, 1), 0)
    band = jnp.where(r <= HY_BANDS, r - 1, r - 1 - HY_BANDS).astype(F32)
    f = 1e-4 + band * ((HY_BANDS - 1 - 1e-4) / (HY_BANDS - 1))
    phase = jnp.where(r <= HY_BANDS, 0.5 * math.pi, math.pi)
    z = jnp.where(r == 0, t_l, jnp.where(r <= 2 * HY_BANDS, jnp.sin(f * w_l + phase), 0.0))
    fr = fr_ref[...]
    h = jnp.sin(fr * (_dot(w1_ref[...], z, HI) + b1_ref[...]))
    h = jnp.sin(fr * (_dot(w2_ref[...], h, HI) + b2_ref[...]))
    h = jnp.sin(fr * (_dot(w3_ref[...], h, HI) + b3_ref[...]))
    h = lax.dot_general(h, w4_ref[...], TN, precision=HI, preferred_element_type=F32)
    idx = i * tp + lax.broadcasted_iota(jnp.int32, (tp, 1), 0)
    t, _ = position(idx)
    ch = lax.broadcasted_iota(jnp.int32, (1, width), 1).astype(F32)
    min_decay = math.log(HY_TARGET) / HY_SLOW_DECAY
    max_decay = math.log(HY_TARGET) / HY_FAST_DECAY
    deltas = min_decay + ch * ((max_decay - min_decay) / (width - 1))
    window = jnp.exp(-t * jnp.abs(deltas))
    valid = jnp.logical_or(idx < l, idx > n - l)
    o_ref[...] = jnp.where(valid, h * window, 0.0)


def _hy_filter(filt, l, n):
    w1p, b1, w2, b2, w3, b3, w4, fr = filt
    width = w4.shape[1] // 2
    tp = min(512, l)
    full = lambda a: pl.BlockSpec(a.shape, lambda i: (0, 0))
    return pl.pallas_call(
        functools.partial(_hy_filter_kernel, tp=tp, l=l, n=n, width=width),
        name="hy_filter",
        out_shape=jax.ShapeDtypeStruct((n, width), F32),
        grid=(n // tp,),
        in_specs=[full(w1p), full(b1), full(w2), full(b2), full(w3), full(b3),
                  pl.BlockSpec((w4.shape[0], width), lambda i: (0, (i * tp >= l).astype(jnp.int32))),
                  full(fr)],
        out_specs=pl.BlockSpec((tp, width), lambda i: (i, 0)),
        compiler_params=_cp("parallel"),
    )(w1p, b1, w2, b2, w3, b3, w4, fr)


def _hy_ctx_kernel(x0_ref, x1_ref, v_ref, kern_ref, skip_ref, o_ref, ks_ref, u_ref, *, l):
    u = x1_ref[...] * v_ref[...]
    u_ref[...] = u
    base = jnp.concatenate([kern_ref[l:2 * l, :], kern_ref[0:l, :]], axis=0)
    ks_ref[0] = base
    for b in range(1, 8):
        ks_ref[b] = pltpu.roll(base, b, 0)

    def body(a, acc):
        start = pl.multiple_of(l - 8 * a, 8)
        ublk = u_ref[pl.ds(pl.multiple_of(8 * a, 8), 8), :]
        for b in range(8):
            acc = acc + ks_ref[b, pl.ds(start, l), :] * ublk[b:b + 1, :]
        return acc

    y = lax.fori_loop(0, l // 8, body, jnp.zeros((l, 128), F32))
    o_ref[...] = (x0_ref[...] * (y + u * skip_ref[...])).astype(o_ref.dtype)


def _hy_ctx(p, kern, skip, *, ctx):
    width = skip.shape[1]
    nb = width // 128
    return pl.pallas_call(
        functools.partial(_hy_ctx_kernel, l=ctx),
        name="hy_ctx_conv",
        out_shape=jax.ShapeDtypeStruct((ctx, width), BF16),
        grid=(nb,),
        in_specs=[pl.BlockSpec((ctx, 128), lambda c: (0, c)),
                  pl.BlockSpec((ctx, 128), lambda c: (0, nb + c)),
                  pl.BlockSpec((ctx, 128), lambda c: (0, 2 * nb + c)),
                  pl.BlockSpec((2 * ctx, 128), lambda c: (0, c)),
                  pl.BlockSpec((1, 128), lambda c: (0, c))],
        out_specs=pl.BlockSpec((ctx, 128), lambda c: (0, c)),
        scratch_shapes=[pltpu.VMEM((8, 2 * ctx, 128), F32), pltpu.VMEM((ctx, 128), F32)],
        compiler_params=_cp("parallel"),
    )(p, p, p, kern, skip)


def _hy_prep_kernel(x0_ref, x1_ref, v_ref, u_ref, x0o_ref):
    u_ref[...] = x1_ref[...] * v_ref[...]
    x0o_ref[...] = x0_ref[...]


def _hy_prep(p, *, ctx, width):
    s_lat = p.shape[0] - ctx
    tm = 256
    off = ctx // tm
    return pl.pallas_call(
        _hy_prep_kernel,
        name="hy_prep",
        out_shape=(jax.ShapeDtypeStruct((s_lat, width), F32), jax.ShapeDtypeStruct((s_lat, width), F32)),
        grid=(s_lat // tm,),
        in_specs=[pl.BlockSpec((tm, width), lambda i: (i + off, 0)),
                  pl.BlockSpec((tm, width), lambda i: (i + off, 1)),
                  pl.BlockSpec((tm, width), lambda i: (i + off, 2))],
        out_specs=(pl.BlockSpec((tm, width), lambda i: (i, 0)), pl.BlockSpec((tm, width), lambda i: (i, 0))),
        compiler_params=_cp("parallel"),
    )(p, p, p)


def _dft_consts(n1):
    n2 = FFT_N2
    n = n1 * n2
    f1n = n1 // 2 + 1
    f1p = -(-f1n // 8) * 8
    f1 = np.arange(f1p)[:, None].astype(np.float64)
    live = (f1 < f1n).astype(np.float64)
    t1 = np.arange(n1)[None, :]
    ang = 2 * np.pi * f1 * t1 / n1
    w_re, w_im = np.cos(ang) * live, -np.sin(ang) * live
    t0 = np.arange(n2)
    ang = 2 * np.pi * f1[None, :, :] * t0[:, None, None] / n
    tw_fwd = np.concatenate([np.cos(ang), -np.sin(ang)], axis=-1)
    ang = 2 * np.pi * f1[:, :, None] * t0[None, :, None] / n
    tw_inv = np.concatenate([np.cos(ang), np.sin(ang)], axis=-1)
    ang = 2 * np.pi * np.outer(t0, t0) / n2
    d_re, d_im = np.cos(ang), -np.sin(ang)
    wgt = np.where((f1 == 0) | (f1 == n1 // 2), 1.0, 2.0) * live
    t1o = np.arange(n1 // 2)[:, None]
    ang = 2 * np.pi * t1o * f1.T / n1
    c_m, s_m = np.cos(ang) * wgt.T / n, -np.sin(ang) * wgt.T / n
    f = lambda a: jnp.asarray(a, F32)
    return dict(f1p=f1p, w_re=f(w_re), w_im=f(w_im), tw_fwd=f(tw_fwd), tw_inv=f(tw_inv),
                d_re=f(d_re), d_im=f(d_im), c_m=f(c_m), s_m=f(s_m))


def _fft1_kernel(u_ref, wre_ref, wim_ref, tw_ref, bre_ref, bim_ref, *, tb, c):
    wre = _split_bf16(wre_ref[...])
    wim = _split_bf16(wim_ref[...])
    for b in range(tb):
        sl = slice(b * c, (b + 1) * c)
        u = _split_bf16(u_ref[:, sl])
        are = _mm3(wre, u)
        aim = _mm3(wim, u)
        tr = tw_ref[b, :, 0:1]
        ti = tw_ref[b, :, 1:2]
        bre_ref[:, sl] = are * tr - aim * ti
        bim_ref[:, sl] = are * ti + aim * tr


def _fft1(u2, w_re, w_im, tw_fwd, *, c):
    t1_in = u2.shape[0]
    f1p = w_re.shape[0]
    tb = 4
    out = jax.ShapeDtypeStruct((f1p, FFT_N2 * c), F32)
    return pl.pallas_call(
        functools.partial(_fft1_kernel, tb=tb, c=c),
        name="fft_outer",
        out_shape=(out, out),
        grid=(FFT_N2 // tb,),
        in_specs=[pl.BlockSpec((t1_in, tb * c), lambda i: (0, i)),
                  pl.BlockSpec((f1p, t1_in), lambda i: (0, 0)),
                  pl.BlockSpec((f1p, t1_in), lambda i: (0, 0)),
                  pl.BlockSpec((tb, f1p, 2), lambda i: (i, 0, 0))],
        out_specs=(pl.BlockSpec((f1p, tb * c), lambda i: (0, i)), pl.BlockSpec((f1p, tb * c), lambda i: (0, i))),
        compiler_params=_cp("parallel"),
    )(u2, w_re[:, :t1_in], w_im[:, :t1_in], tw_fwd)


def _fft_spec_kernel(bre_ref, bim_ref, dre_ref, dim_ref, xre_ref, xim_ref):
    br, bi = _split_bf16(bre_ref[0]), _split_bf16(bim_ref[0])
    dr, di = _split_bf16(dre_ref[...]), _split_bf16(dim_ref[...])
    xre_ref[0] = _mm3(dr, br) - _mm3(di, bi)
    xim_ref[0] = _mm3(dr, bi) + _mm3(di, br)


def _fft_spectrum(b_re, b_im, d_re, d_im):
    f1p, n2, c = b_re.shape
    blk = pl.BlockSpec((1, n2, c), lambda i: (i, 0, 0))
    mat = pl.BlockSpec((n2, n2), lambda i: (0, 0))
    out = jax.ShapeDtypeStruct((f1p, n2, c), F32)
    return pl.pallas_call(
        _fft_spec_kernel, name="fft_spectrum", out_shape=(out, out), grid=(f1p,),
        in_specs=[blk, blk, mat, mat], out_specs=(blk, blk),
        compiler_params=_cp("parallel"),
    )(b_re, b_im, d_re, d_im)


def _fft_mid_kernel(bre_ref, bim_ref, kre_ref, kim_ref, dre_ref, dim_ref, tw_ref, ore_ref, oim_ref):
    br, bi = _split_bf16(bre_ref[0]), _split_bf16(bim_ref[0])
    dr, di = _split_bf16(dre_ref[...]), _split_bf16(dim_ref[...])
    xr = _mm3(dr, br) - _mm3(di, bi)
    xi = _mm3(dr, bi) + _mm3(di, br)
    kr, ki = kre_ref[0], kim_ref[0]
    yr = _split_bf16(xr * kr - xi * ki)
    yi = _split_bf16(xr * ki + xi * kr)
    zr = _mm3(dr, yr) + _mm3(di, yi)
    zi = _mm3(dr, yi) - _mm3(di, yr)
    tr = tw_ref[0, :, 0:1]
    ti = tw_ref[0, :, 1:2]
    ore_ref[0] = zr * tr - zi * ti
    oim_ref[0] = zr * ti + zi * tr


def _fft_mid(b_re, b_im, k_re, k_im, d_re, d_im, tw_inv):
    f1p, n2, c = b_re.shape
    blk = pl.BlockSpec((1, n2, c), lambda i: (i, 0, 0))
    mat = pl.BlockSpec((n2, n2), lambda i: (0, 0))
    out = jax.ShapeDtypeStruct((f1p, n2, c), F32)
    return pl.pallas_call(
        _fft_mid_kernel, name="fft_mid", out_shape=(out, out), grid=(f1p,),
        in_specs=[blk, blk, blk, blk, mat, mat, pl.BlockSpec((1, n2, 2), lambda i: (i, 0, 0))],
        out_specs=(blk, blk),
        compiler_params=_cp("parallel"),
    )(b_re, b_im, k_re, k_im, d_re, d_im, tw_inv)


def _fft_fin_kernel(bre_ref, bim_ref, cm_ref, sm_ref, x0_ref, u_ref, skip_ref, o_ref, *, tb, c):
    y = (_mm3(_split_bf16(cm_ref[...]), _split_bf16(bre_ref[...]))
         + _mm3(_split_bf16(sm_ref[...]), _split_bf16(bim_ref[...])))
    skip = skip_ref[...]
    for b in range(tb):
        sl = slice(b * c, (b + 1) * c)
        o_ref[:, sl] = (x0_ref[:, sl] * (y[:, sl] + u_ref[:, sl] * skip)).astype(o_ref.dtype)


def _fft_fin(z_re, z_im, c_m, s_m, x0_2, u2, skip, *, c):
    f1p = z_re.shape[0]
    t1o = c_m.shape[0]
    tb = 4
    big = pl.BlockSpec((f1p, tb * c), lambda i: (0, i))
    sig = pl.BlockSpec((t1o, tb * c), lambda i: (0, i))
    mat = pl.BlockSpec((t1o, f1p), lambda i: (0, 0))
    return pl.pallas_call(
        functools.partial(_fft_fin_kernel, tb=tb, c=c),
        name="fft_final",
        out_shape=jax.ShapeDtypeStruct((t1o, FFT_N2 * c), BF16),
        grid=(FFT_N2 // tb,),
        in_specs=[big, big, mat, mat, sig, sig, pl.BlockSpec((1, c), lambda i: (0, 0))],
        out_specs=sig,
        compiler_params=_cp("parallel"),
    )(z_re, z_im, c_m, s_m, x0_2, u2, skip)


def _hyena_latent(p, kern, skip, *, ctx):
    c = skip.shape[1]
    s_lat = p.shape[0] - ctx
    n1 = 2 * s_lat // FFT_N2
    k = _dft_consts(n1)
    f1p = k["f1p"]
    u, x0 = _hy_prep(p, ctx=ctx, width=c)
    u2 = u.reshape(n1 // 2, FFT_N2 * c)
    x0_2 = x0.reshape(n1 // 2, FFT_N2 * c)
    kb_re, kb_im = _fft1(kern.reshape(n1, FFT_N2 * c), k["w_re"], k["w_im"], k["tw_fwd"], c=c)
    kf_re, kf_im = _fft_spectrum(kb_re.reshape(f1p, FFT_N2, c), kb_im.reshape(f1p, FFT_N2, c), k["d_re"], k["d_im"])
    b_re, b_im = _fft1(u2, k["w_re"], k["w_im"], k["tw_fwd"], c=c)
    z_re, z_im = _fft_mid(b_re.reshape(f1p, FFT_N2, c), b_im.reshape(f1p, FFT_N2, c), kf_re, kf_im,
                          k["d_re"], k["d_im"], k["tw_inv"])
    y2 = _fft_fin(z_re.reshape(f1p, FFT_N2 * c), z_im.reshape(f1p, FFT_N2 * c), k["c_m"], k["s_m"],
                  x0_2, u2, skip, c=c)
    return y2.reshape(s_lat, c)


def _softplus(x):
    return jnp.maximum(x, 0.0) + jnp.log(1.0 + jnp.exp(-jnp.abs(x)))


def _dotp(a, b, dims=(((1,), (0,)), ((), ())), passes=1):
    f = lambda x, y: lax.dot_general(x, y, dims, preferred_element_type=F32)
    if passes == 1:
        return f(a.astype(BF16), b.astype(BF16))
    ah, al = _split_bf16(a)
    bh, bl = _split_bf16(b)
    return f(ah, bh) + (f(al, bh) + f(ah, bl))


DN_U, DN_W, DN_QD, DN_KD, DN_QK = range(5)


def _dn_intra_kernel(q_ref, k_ref, v_ref, sm_ref, cst_ref, pk_ref, last_ref):
    c = DN_CHUNK
    nh = q_ref.shape[1] // DN_HEAD
    n_cb = q_ref.shape[0] // c
    r = nh * c
    row = lax.broadcasted_iota(jnp.int32, (c, c), 0)
    col = lax.broadcasted_iota(jnp.int32, (c, c), 1)
    tri = ((row >= col).astype(F32), (row <= col).astype(F32))
    zpad = jnp.zeros((128 - c, 128), F32)
    rr = lax.broadcasted_iota(jnp.int32, (r, r), 0)
    cc = lax.broadcasted_iota(jnp.int32, (r, r), 1)
    sh = int(math.log2(c))
    same = jnp.right_shift(rr, sh) == jnp.right_shift(cc, sh)
    incl = (jnp.logical_and(same, rr >= cc), jnp.logical_and(same, rr <= cc))
    strict = (jnp.logical_and(same, rr > cc), jnp.logical_and(same, rr < cc))

    units = []
    for cb in range(n_cb):
        ts = slice(cb * c, (cb + 1) * c)
        sm = sm_ref[ts, :]
        beta_all = jax.nn.sigmoid(sm)
        g_all = -jnp.exp(cst_ref[0:1, :]) * _softplus(sm + cst_ref[1:2, :])
        tot_all = jnp.sum(g_all, axis=0, keepdims=True)
        qs, ks, vs = [], [], []
        for h in range(nh):
            hs = slice(h * DN_HEAD, (h + 1) * DN_HEAD)
            qh = _silu(q_ref[ts, hs])
            kh = _silu(k_ref[ts, hs])
            qs.append(qh * lax.rsqrt(jnp.sum(qh * qh, axis=-1, keepdims=True) + EPS) * (DN_HEAD ** -0.5))
            ks.append(kh * lax.rsqrt(jnp.sum(kh * kh, axis=-1, keepdims=True) + EPS))
            vs.append(_silu(v_ref[ts, hs]))
        q_rows = jnp.concatenate(qs, axis=0)
        k_rows = jnp.concatenate(ks, axis=0)
        v_rows = jnp.concatenate(vs, axis=0)
        qkt = _dotp(q_rows, k_rows, NT)
        for d in range(2):
            gc = _dot(tri[d], g_all, HI)
            gct = jnp.concatenate([gc, zpad], axis=0).T
            bis = [d * nh + h for h in range(nh)]
            gis = [2 * nh + b for b in bis]
            beta = jnp.concatenate([beta_all[:, b:b + 1] for b in bis], axis=0)
            gcol = jnp.concatenate([gc[:, g:g + 1] for g in gis], axis=0)
            grow = jnp.concatenate([gct[g:g + 1, :c] for g in gis], axis=1)
            gtot = jnp.concatenate([jnp.broadcast_to(tot_all[:, g:g + 1], (c, 1)) for g in gis], axis=0)
            gamma = jnp.where(incl[d], jnp.exp(jnp.where(incl[d], gcol - grow, 0.0)), 0.0)
            egc = jnp.exp(gcol)
            kb = k_rows * beta
            units.append(dict(
                cb=cb, d=d, tot=tot_all, qdec=q_rows * egc, kd=k_rows * jnp.exp(gtot - gcol), qk=qkt * gamma,
                pw=-jnp.where(strict[d], _dotp(kb, k_rows, NT) * gamma, 0.0),
                sol=jnp.concatenate([v_rows * beta, kb * egc], axis=1)))
    steps = int(math.log2(c))
    for t in range(steps):
        for un in units:
            un["sol"] = un["sol"] + _dotp(un["pw"], un["sol"], passes=DN_CHAIN_PASSES)
        if t + 1 < steps:
            for un in units:
                un["pw"] = _dotp(un["pw"], un["pw"], passes=DN_CHAIN_PASSES)
    for un in units:
        d, cb = un["d"], un["cb"]
        ts = slice(cb * c, (cb + 1) * c)
        for h in range(nh):
            hs = slice(h * DN_HEAD, (h + 1) * DN_HEAD)
            rs = slice(h * c, (h + 1) * c)
            gi = 2 * nh + d * nh + h
            wd = nh * DN_HEAD

            def put(k, val):
                pk_ref[d, ts, k * wd + h * DN_HEAD:k * wd + (h + 1) * DN_HEAD] = val

            put(DN_U, un["sol"][rs, :DN_HEAD])
            put(DN_W, un["sol"][rs, DN_HEAD:])
            put(DN_QD, un["qdec"][rs])
            put(DN_KD, un["kd"][rs])
            pk_ref[d, ts, DN_QK * wd + h * c:DN_QK * wd + (h + 1) * c] = un["qk"][rs, rs]
            last_ref[d, cb, :, hs] = jnp.broadcast_to(jnp.exp(un["tot"][:, gi:gi + 1]), (8, DN_HEAD))


def _dn_intra(p, cst):
    s_tot = p.shape[0]
    c = DN_CHUNK
    w = 512
    nch = s_tot // c
    n_cb = DN_INTRA_CHUNKS
    rows = n_cb * c
    assert nch % n_cb == 0
    pw = 4 * w + w // 2
    return pl.pallas_call(
        _dn_intra_kernel,
        name="dn_intra",
        out_shape=(jax.ShapeDtypeStruct((2, s_tot, pw), F32), jax.ShapeDtypeStruct((2, nch, 8, w), F32)),
        grid=(nch // n_cb,),
        in_specs=[pl.BlockSpec((rows, w), lambda j: (j, 3)),
                  pl.BlockSpec((rows, w), lambda j: (j, 4)),
                  pl.BlockSpec((rows, w), lambda j: (j, 5)),
                  pl.BlockSpec((rows, 128), lambda j: (j, 46)),
                  pl.BlockSpec((8, 128), lambda j: (0, 0))],
        out_specs=(pl.BlockSpec((2, rows, pw), lambda j: (0, j, 0)),
                   pl.BlockSpec((2, n_cb, 8, w), lambda j: (0, j, 0, 0))),
        compiler_params=_cp("parallel"),
    )(p, p, p, p, cst)


def _dn_rec_kernel(pkf_ref, lastf_ref, pkb_ref, lastb_ref, of_ref, ob_ref, s_ref, *, nh):
    pk_refs, last_refs, o_refs = (pkf_ref, pkb_ref), (lastf_ref, lastb_ref), (of_ref, ob_ref)

    @pl.when(pl.program_id(0) == 0)
    def _():
        s_ref[...] = jnp.zeros_like(s_ref)

    c = DN_CHUNK
    width = nh * DN_HEAD
    n_sub = pkf_ref.shape[1] // c
    chains = [(d, h) for d in range(2) for h in range(nh)]
    hcols = lambda h: slice(h * DN_HEAD, (h + 1) * DN_HEAD)
    s = [s_ref[d * nh + h] for d, h in chains]
    for sub in range(n_sub):
        cidx = (sub, n_sub - 1 - sub)
        rows = [slice(ci * c, (ci + 1) * c) for ci in cidx]
        part = lambda d, k, h: pk_refs[d][0, rows[d], k * width + h * DN_HEAD:k * width + (h + 1) * DN_HEAD]
        v_new = [part(d, DN_U, h) - _dotp(part(d, DN_W, h), s[i]) for i, (d, h) in enumerate(chains)]
        qs = [_dotp(part(d, DN_QD, h), s[i]) for i, (d, h) in enumerate(chains)]
        for i, (d, h) in enumerate(chains):
            qk = pk_refs[d][0, rows[d], DN_QK * width + h * c:DN_QK * width + (h + 1) * c]
            o_refs[d][rows[d], hcols(h)] = qs[i] + _dotp(qk, v_new[i])
        s = [s[i] * last_refs[d][0, cidx[d], 0:1, hcols(h)] + _dotp(part(d, DN_KD, h), v_new[i], TN)
             for i, (d, h) in enumerate(chains)]
    for i, (d, h) in enumerate(chains):
        s_ref[d * nh + h] = s[i]


def _dn_recurrence(pk, last, *, ctx):
    _, s_tot, pw = pk.shape
    width = last.shape[-1]
    c = DN_CHUNK
    n_sub = DN_SCAN_CHUNKS
    nblk = s_tot // (c * n_sub)
    nctx = ctx // (c * n_sub)
    assert s_tot % (c * n_sub) == 0 and ctx % (c * n_sub) == 0
    nh = width // DN_HEAD
    fwd = lambda j: j
    bwd = lambda j: jnp.where(j < nctx, nctx - 1 - j, nblk - 1 - (j - nctx))
    specs = []
    for d, cm in ((0, fwd), (1, bwd)):
        specs += [pl.BlockSpec((1, c * n_sub, pw), lambda j, cm=cm, d=d: (d, cm(j), 0)),
                  pl.BlockSpec((1, n_sub, 8, width), lambda j, cm=cm, d=d: (d, cm(j), 0, 0))]
    out = jax.ShapeDtypeStruct((s_tot, width), F32)
    return pl.pallas_call(
        functools.partial(_dn_rec_kernel, nh=nh),
        name="dn_scan",
        out_shape=(out, out),
        grid=(nblk,),
        in_specs=specs,
        out_specs=(pl.BlockSpec((c * n_sub, width), lambda j: (fwd(j), 0)),
                   pl.BlockSpec((c * n_sub, width), lambda j: (bwd(j), 0))),
        scratch_shapes=[pltpu.VMEM((2 * nh, DN_HEAD, DN_HEAD), F32)],
        compiler_params=_cp("arbitrary"),
    )(pk, last, pk, last)


def _dn_finish_kernel(of_ref, ob_ref, gate_ref, gain_ref, o_ref):
    nh = of_ref.shape[1] // DN_HEAD
    for h in range(nh):
        hs = slice(h * DN_HEAD, (h + 1) * DN_HEAD)
        o = of_ref[:, hs] + ob_ref[:, hs]
        y = o * lax.rsqrt(jnp.mean(o * o, axis=-1, keepdims=True) + EPS) * gain_ref[...]
        o_ref[:, hs] = (y * _silu(gate_ref[:, hs])).astype(o_ref.dtype)


def _dn_finish(o_f, o_b, p, gain):
    s_tot, width = o_f.shape
    tm = _row_tile(s_tot, (528, 640, 256, 128))
    blk = pl.BlockSpec((tm, width), lambda i: (i, 0))
    return pl.pallas_call(
        _dn_finish_kernel,
        name="dn_finish",
        out_shape=jax.ShapeDtypeStruct((s_tot, width), BF16),
        grid=(s_tot // tm,),
        in_specs=[blk, blk, pl.BlockSpec((tm, width), lambda i: (i, 6)), pl.BlockSpec((1, DN_HEAD), lambda i: (0, 0))],
        out_specs=blk,
        compiler_params=_cp("parallel"),
    )(o_f, o_b, p, gain)


def _rope_tables(ctx, s_lat):
    rows = s_lat // GRID_W
    n_freq = HEAD_D // 4
    row = np.repeat(np.arange(rows, dtype=np.float32), GRID_W)
    col = np.tile(np.arange(GRID_W, dtype=np.float32), rows)
    inv = (ROPE_THETA ** (-np.arange(n_freq, dtype=np.float32) / n_freq)).astype(np.float32)
    ang = np.concatenate([row[:, None] * inv, col[:, None] * inv], axis=-1).astype(np.float32)
    cos, sin = np.cos(ang), np.sin(ang)
    cos = np.concatenate([np.ones((ctx, HEAD_D // 2), np.float32), cos], axis=0)
    sin = np.concatenate([np.zeros((ctx, HEAD_D // 2), np.float32), sin], axis=0)
    cos_t = np.tile(np.concatenate([cos, cos], axis=1), (1, 2))
    sin_t = np.tile(np.concatenate([-sin, sin], axis=1), (1, 2))
    return jnp.asarray(cos_t, F32), jnp.asarray(sin_t, F32)


def _pad_rows(a, rows):
    return jnp.concatenate([a, jnp.zeros((rows - a.shape[0],) + a.shape[1:], a.dtype)], axis=0)


def kernel(x, c, ctx, c_ctx, w_ada, b_ada, norm_mix_pre, norm_mix_post, norm_ffn_pre, norm_ffn_post, w_in, w_out, attn_q_norm, attn_k_norm, hy_short, hy_w1, hy_b1, hy_w2, hy_b2, hy_w3, hy_b3, hy_w4, hy_freq, hy_skip, dn_short, dn_a_log, dn_dt_bias, dn_norm, df_lambda, df_norm, ffn_up, ffn_conv, ffn_down):
    batch, s_lat, d = x.shape
    n_ctx = ctx.shape[1]
    depth = w_in.shape[0]
    gw = d // 4
    assert batch == 1 and gw == 512 and n_ctx % 256 == 0 and s_lat % 512 == 0
    s_tot = n_ctx + s_lat

    xx = jnp.concatenate([ctx[0], x[0]], axis=0)
    c_rows = _pad_rows(jnp.concatenate([c, c_ctx[None, :]], axis=0), 8)
    mods = _mod_vectors(c_rows, w_ada, b_ada)

    cos_t, sin_t = _rope_tables(n_ctx, s_lat)
    lane = np.arange(128)
    gmat = jnp.asarray((lane[:, None] // HEAD_D == lane[None, :] // HEAD_D) / HEAD_D, F32)
    ident = jnp.asarray([[0.0], [1.0], [0.0]], F32)

    a_cols = gw + 2 * 2 * HEAD_D
    o_hy, o_dn, o_df = a_cols, a_cols + 3 * gw, a_cols + 3 * gw + 4 * gw + 16
    order = [(o_hy, o_hy + 3 * gw), (o_dn, o_dn + 4 * gw), (o_df, o_df + 3 * gw), (0, a_cols),
             (o_dn + 4 * gw, o_dn + 4 * gw + 16)]
    n_proj = 12 * gw
    used = sum(b - a for a, b in order)

    for i in range(depth):
        lam_init = 0.8 - 0.6 * math.exp(-0.3 * i)
        mod = mods[i]
        sh1, sc1, gt1, sh2, sc2, gt2 = (mod[:, k * d:(k + 1) * d] for k in range(6))
        mod_mix = _pad_rows(jnp.concatenate([sh1[0:1], sc1[0:1], sh1[1:2], sc1[1:2]], axis=0), 8)
        mod_ffn = _pad_rows(jnp.concatenate([sh2[0:1], sc2[0:1], sh2[1:2], sc2[1:2]], axis=0), 8)

        w_in_p = jnp.concatenate([w_in[i][:, a:b] for a, b in order]
                                 + [jnp.zeros((d, n_proj - used), F32)], axis=1).astype(BF16)
        cw_in = jnp.concatenate([hy_short[i], dn_short[i], jnp.broadcast_to(ident, (3, n_proj - 6 * gw))], axis=1)
        p = _norm_mod_matmul_conv(xx, norm_mix_pre[i][None, :], mod_mix, w_in_p, _pad_rows(cw_in, 8),
                                  ctx=n_ctx, ffn=False, conv_cols=6 * gw)

        qg, kg, vgt, qd, kd, vdt = _attn_prep(p, jnp.tile(attn_q_norm[i], 2)[None, :],
                                              jnp.tile(attn_k_norm[i], 2)[None, :], cos_t, sin_t, gmat)
        ya = _gqa_attention(qg, kg, vgt, ctx=n_ctx)
        yd = _diff_attention(qd, kd, vdt, df_lambda[i], df_norm[i][None, :], ctx=n_ctx, lam_init=lam_init)

        filt = (_pad_rows(hy_w1[i], 40).T, hy_b1[i][:, None], hy_w2[i].T, hy_b2[i][:, None], hy_w3[i].T,
                hy_b3[i][:, None], hy_w4[i], hy_freq[i][:, None])
        skip = hy_skip[i][None, :]
        yb_lat = _hyena_latent(p, _hy_filter(filt, s_lat, 2 * s_lat), skip, ctx=n_ctx)
        if i < depth - 1:
            yb_ctx = _hy_ctx(p, _hy_filter(filt, n_ctx, 2 * n_ctx), skip, ctx=n_ctx)
        else:
            yb_ctx = jnp.zeros((n_ctx, gw), BF16)

        zeros8 = jnp.zeros((8,), F32)
        cst = _pad_rows(jnp.stack([jnp.concatenate([zeros8, dn_a_log[i].reshape(-1), jnp.zeros((112,), F32)]),
                                   jnp.concatenate([zeros8, dn_dt_bias[i].reshape(-1), jnp.zeros((112,), F32)])]), 8)
        o_f, o_b = _dn_recurrence(*_dn_intra(p, cst), ctx=n_ctx)
        yc = _dn_finish(o_f, o_b, p, dn_norm[i][None, :])

        xx = _out_proj(ya, yb_ctx, yb_lat, yc, yd, w_out[i].astype(BF16), xx, gt1, norm_mix_post[i][None, :],
                       ctx=n_ctx)

        g = _norm_mod_matmul_conv(xx, norm_ffn_pre[i][None, :], mod_ffn, ffn_up[i].astype(BF16),
                                  _pad_rows(ffn_conv[i], 8), ctx=n_ctx, ffn=True, conv_cols=ffn_down.shape[1])
        xx = _matmul_norm_residual(g, ffn_down[i].astype(BF16), xx, gt2, norm_ffn_post[i][None, :], ctx=n_ctx)

    return xx[n_ctx:][None]
```

```python
import functools
import math

import jax
import jax.numpy as jnp
import numpy as np
from jax import lax
from jax.experimental import pallas as pl
from jax.experimental.pallas import tpu as pltpu

F32 = jnp.float32
BF16 = jnp.bfloat16
HI = lax.Precision.HIGHEST
EPS = 1e-6

GRID_W = 64
ROPE_THETA = 10000.0
HEAD_D = 64
GQA_GROUP = 4
DN_HEAD = 128
DN_CHUNK = 64
DN_SCAN_CHUNKS = 4
DN_INTRA_CHUNKS = 4
DN_CHAIN_PASSES = 3
HY_BANDS = 16
HY_FAST_DECAY, HY_SLOW_DECAY, HY_TARGET = 0.3, 1.5, 1e-2
FFT_N2 = 128
ATT_TK = 256
DF_HPS = 2
LOG2E = 1.4426950408889634
VMEM_LIMIT = 56 * 1024 * 1024
NT = (((1,), (1,)), ((), ()))
TN = (((0,), (0,)), ((), ()))


def _cp(*sem):
    return pltpu.CompilerParams(dimension_semantics=sem, vmem_limit_bytes=VMEM_LIMIT)


def _dot(a, b, precision=None):
    return jnp.dot(a, b, precision=precision, preferred_element_type=F32)


def _split_bf16(a):
    hi = a.astype(BF16)
    return hi, (a - hi.astype(F32)).astype(BF16)


def _mm3(a, b):
    return _dot(a[0], b[0]) + (_dot(a[1], b[0]) + _dot(a[0], b[1]))


def _silu(x):
    return x * jax.nn.sigmoid(x)


def _row_tile(n, cands):
    for t in cands:
        if n % t == 0:
            return t
    raise ValueError(f"no row tile for {n}")


def _mod_kernel(c_ref, w_ref, b_ref, o_ref):
    o_ref[0] = _mm3(_split_bf16(_silu(c_ref[...])), _split_bf16(w_ref[0])) + b_ref[0]


def _mod_vectors(c_rows, w_ada, b_ada):
    depth, d, n = w_ada.shape
    tn = 512
    return pl.pallas_call(
        _mod_kernel,
        name="mod_vectors",
        out_shape=jax.ShapeDtypeStruct((depth, 8, n), F32),
        grid=(depth, n // tn),
        in_specs=[pl.BlockSpec((8, d), lambda l, j: (0, 0)),
                  pl.BlockSpec((1, d, tn), lambda l, j: (l, 0, j)),
                  pl.BlockSpec((1, 1, tn), lambda l, j: (l, 0, j))],
        out_specs=pl.BlockSpec((1, 8, tn), lambda l, j: (l, 0, j)),
        compiler_params=_cp("parallel", "parallel"),
    )(c_rows, w_ada, b_ada.reshape(depth, 1, n))


HALO = 16


def _k1_kernel(xp_ref, xm_ref, xn_ref, g_ref, mod_ref, *rest, tm, ctx, s_tot, ffn, n_conv):
    if ffn:
        wa_ref, wb_ref, cwa_ref, cwb_ref, o_ref, h_ref, *acc_refs = rest
    else:
        wa_ref, cwa_ref, o_ref, h_ref, *acc_refs = rest
    i = pl.program_id(0)
    j = pl.program_id(1)

    @pl.when(j == 0)
    def _():
        def nm(xv, row0):
            ms = jnp.mean(xv * xv, axis=-1, keepdims=True)
            y = xv * lax.rsqrt(ms + EPS) * g_ref[...]
            rows = row0 + lax.broadcasted_iota(jnp.int32, (xv.shape[0], 1), 0)
            is_ctx = rows < ctx
            sh = jnp.where(is_ctx, mod_ref[2:3, :], mod_ref[0:1, :])
            sc = jnp.where(is_ctx, mod_ref[3:4, :], mod_ref[1:2, :])
            return (y * (1.0 + sc) + sh).astype(BF16)

        h_ref[0:HALO] = nm(xp_ref[...], i * tm - HALO)
        n_sub = 3 if (tm > 640 and tm % 48 == 0) else 1
        rc = tm // n_sub
        for r0 in range(0, tm, rc):
            h_ref[HALO + r0:HALO + r0 + rc] = nm(xm_ref[r0:r0 + rc], i * tm + r0)
        h_ref[HALO + tm:] = nm(xn_ref[...], i * tm + tm)

    def conv(w_ref, cw_ref, edge, acc_ref):
        acc_ref[...] = _dot(h_ref[...], w_ref[...])
        prev = acc_ref[HALO - 1:HALO - 1 + tm]
        nxt = acc_ref[HALO + 1:HALO + 1 + tm]
        if edge:
            rows = i * tm + lax.broadcasted_iota(jnp.int32, (tm, 1), 0)
            prev = jnp.where(jnp.logical_and(rows != 0, rows != ctx), prev, 0.0)
            nxt = jnp.where(jnp.logical_and(rows != ctx - 1, rows != s_tot - 1), nxt, 0.0)
        cw = cw_ref[...]
        return acc_ref[HALO:HALO + tm] * cw[1:2] + prev * cw[0:1] + nxt * cw[2:3]

    def emit(edge):
        if ffn:
            a = conv(wa_ref, cwa_ref, edge, acc_refs[0])
            b = conv(wb_ref, cwb_ref, edge, acc_refs[0])
            o_ref[...] = (_silu(a) * b).astype(o_ref.dtype)
        else:
            o_ref[...] = conv(wa_ref, cwa_ref, edge, acc_refs[0]).astype(o_ref.dtype)

    edge_tiles = sorted({r // tm for r in (0, ctx - 1, ctx, s_tot - 1)})
    is_edge = functools.reduce(jnp.logical_or, [i == e for e in edge_tiles])
    is_conv = j < n_conv
    pl.when(jnp.logical_and(is_conv, is_edge))(lambda: emit(True))
    pl.when(jnp.logical_and(is_conv, jnp.logical_not(is_edge)))(lambda: emit(False))

    if not ffn:
        @pl.when(jnp.logical_not(is_conv))
        def _():
            o_ref[...] = _dot(h_ref[HALO:HALO + tm], wa_ref[...]).astype(o_ref.dtype)


def _norm_mod_matmul_conv(xx, gain, mod4, w, cw, *, ctx, ffn, conv_cols):
    s_tot, d = xx.shape
    n = w.shape[1]
    tm = _row_tile(s_tot, (1056, 640, 256, 128))
    tn = 512
    nb = s_tot // HALO
    n_out = n // 2 if ffn else n
    x_specs = [
        pl.BlockSpec((HALO, d), lambda i, j: (jnp.maximum(i * (tm // HALO) - 1, 0), 0)),
        pl.BlockSpec((tm, d), lambda i, j: (i, 0)),
        pl.BlockSpec((HALO, d), lambda i, j: (jnp.minimum((i + 1) * (tm // HALO), nb - 1), 0)),
        pl.BlockSpec((1, d), lambda i, j: (0, 0)),
        pl.BlockSpec((8, d), lambda i, j: (0, 0)),
    ]
    if ffn:
        off = n_out // tn
        w_specs = [pl.BlockSpec((d, tn), lambda i, j: (0, j)),
                   pl.BlockSpec((d, tn), lambda i, j: (0, j + off)),
                   pl.BlockSpec((8, tn), lambda i, j: (0, j)),
                   pl.BlockSpec((8, tn), lambda i, j: (0, j + off))]
        args = (w, w, cw, cw)
        out_dtype = BF16
    else:
        w_specs = [pl.BlockSpec((d, tn), lambda i, j: (0, j)),
                   pl.BlockSpec((8, tn), lambda i, j: (0, j))]
        args = (w, cw)
        out_dtype = F32
    return pl.pallas_call(
        functools.partial(_k1_kernel, tm=tm, ctx=ctx, s_tot=s_tot, ffn=ffn, n_conv=conv_cols // tn),
        name="ffn_up_conv_gate" if ffn else "in_proj_conv",
        out_shape=jax.ShapeDtypeStruct((s_tot, n_out), out_dtype),
        grid=(s_tot // tm, n_out // tn),
        in_specs=x_specs + w_specs,
        out_specs=pl.BlockSpec((tm, tn), lambda i, j: (i, j)),
        scratch_shapes=[pltpu.VMEM((tm + 2 * HALO, d), BF16), pltpu.VMEM((tm + 2 * HALO, tn), F32)],
        compiler_params=_cp("parallel", "arbitrary"),
    )(xx, xx, xx, gain, mod4, *args)


def _k2_kernel(a_ref, w_ref, x_ref, gate_ref, gain_ref, o_ref, *, tm, ctx, nk):
    i = pl.program_id(0)
    k = pl.program_id(1)

    part = _dot(a_ref[...], w_ref[...])

    def finish(y):
        r = y * lax.rsqrt(jnp.mean(y * y, axis=-1, keepdims=True) + EPS) * gain_ref[...]
        rows = i * tm + lax.broadcasted_iota(jnp.int32, (tm, 1), 0)
        gt = jnp.where(rows < ctx, gate_ref[1:2, :], gate_ref[0:1, :])
        o_ref[...] = x_ref[...] + gt * r

    if nk == 1:
        finish(part)
        return

    @pl.when(k == 0)
    def _():
        o_ref[...] = part

    @pl.when(jnp.logical_and(k > 0, k < nk - 1))
    def _():
        o_ref[...] += part

    @pl.when(k == nk - 1)
    def _():
        finish(o_ref[...] + part)


def _matmul_norm_residual(a, w, xx, gate2, gain, *, ctx):
    s_tot, kdim = a.shape
    d = w.shape[1]
    tm = _row_tile(s_tot, (704, 640, 256, 128))
    nk = 1 if kdim <= 2048 else 4
    tk = kdim // nk
    assert tk % 128 == 0
    return pl.pallas_call(
        functools.partial(_k2_kernel, tm=tm, ctx=ctx, nk=nk),
        name="matmul_norm_residual",
        out_shape=jax.ShapeDtypeStruct((s_tot, d), F32),
        grid=(s_tot // tm, nk),
        in_specs=[pl.BlockSpec((tm, tk), lambda i, k: (i, k)),
                  pl.BlockSpec((tk, d), lambda i, k: (k, 0)),
                  pl.BlockSpec((tm, d), lambda i, k: (i, 0)),
                  pl.BlockSpec((8, d), lambda i, k: (0, 0)),
                  pl.BlockSpec((1, d), lambda i, k: (0, 0))],
        out_specs=pl.BlockSpec((tm, d), lambda i, k: (i, 0)),
        compiler_params=_cp("parallel", "arbitrary"),
    )(a, w, xx, gate2, gain)


def _out_proj_kernel(ya_ref, ybc_ref, ybl_ref, yc_ref, yd_ref, w_ref, x_ref, gate_ref, gain_ref, o_ref, *, n_ctx_tiles):
    is_ctx = pl.program_id(0) < n_ctx_tiles
    gw = ya_ref.shape[1]
    yb = jnp.where(is_ctx, ybc_ref[...], ybl_ref[...])
    y = sum(_dot(a, w_ref[g * gw:(g + 1) * gw, :])
            for g, a in enumerate((ya_ref[...], yb, yc_ref[...], yd_ref[...])))
    r = y * lax.rsqrt(jnp.mean(y * y, axis=-1, keepdims=True) + EPS) * gain_ref[...]
    gt = jnp.where(is_ctx, gate_ref[1:2, :], gate_ref[0:1, :])
    o_ref[...] = x_ref[...] + gt * r


def _out_proj(ya, yb_ctx, yb_lat, yc, yd, w, xx, gate2, gain, *, ctx):
    s_tot, d = xx.shape
    gw = ya.shape[1]
    tm = ATT_TK
    nct = ctx // tm
    blk = pl.BlockSpec((tm, gw), lambda i: (i, 0))
    return pl.pallas_call(
        functools.partial(_out_proj_kernel, n_ctx_tiles=nct),
        name="out_proj",
        out_shape=jax.ShapeDtypeStruct((s_tot, d), F32),
        grid=(s_tot // tm,),
        in_specs=[blk,
                  pl.BlockSpec((tm, gw), lambda i: (jnp.minimum(i, nct - 1), 0)),
                  pl.BlockSpec((tm, gw), lambda i: (jnp.maximum(i - nct, 0), 0)),
                  blk, blk,
                  pl.BlockSpec(w.shape, lambda i: (0, 0)),
                  pl.BlockSpec((tm, d), lambda i: (i, 0)),
                  pl.BlockSpec((8, d), lambda i: (0, 0)),
                  pl.BlockSpec((1, d), lambda i: (0, 0))],
        out_specs=pl.BlockSpec((tm, d), lambda i: (i, 0)),
        compiler_params=_cp("parallel"),
    )(ya, yb_ctx, yb_lat, yc, yd, w, xx, gate2, gain)


def _attn_prep_kernel(aq_ref, akv_ref, dq_ref, dk_ref, dv_ref, qgain_ref, kgain_ref, cos_ref, sin_ref,
                      gmat_ref, qg_ref, kg_ref, vgt_ref, qd_ref, kd_ref, vdt_ref):
    cosf = cos_ref[...]
    sins = sin_ref[...]
    lane = lax.broadcasted_iota(jnp.int32, (1, 128), 1)
    first_half = (lane & (HEAD_D - 1)) < (HEAD_D // 2)
    scale = HEAD_D ** -0.5 * LOG2E

    def rope(x):
        partner = jnp.where(first_half, pltpu.roll(x, 128 - HEAD_D // 2, 1), pltpu.roll(x, HEAD_D // 2, 1))
        return x * cosf + partner * sins

    def headnorm(x, gain):
        g = gmat_ref[...].astype(BF16)
        hi, lo = _split_bf16(x * x)
        ms = _dot(hi, g) + _dot(lo, g)
        return x * lax.rsqrt(ms + EPS) * gain

    def put(ref, pair, val):
        vb = val.astype(ref.dtype)
        ref[2 * pair] = vb[:, :HEAD_D]
        ref[2 * pair + 1] = vb[:, HEAD_D:]

    for ci in range(4):
        sl = slice(ci * 128, (ci + 1) * 128)
        put(qg_ref, ci, rope(headnorm(aq_ref[:, sl], qgain_ref[...])) * scale)
        put(qd_ref, ci, rope(dq_ref[:, sl]) * scale)
        put(kd_ref, ci, rope(dk_ref[:, sl]))
        vdt_ref[ci, 0] = dv_ref[:, sl].T.astype(vdt_ref.dtype)
    put(kg_ref, 0, rope(headnorm(akv_ref[:, 0:128], kgain_ref[...])))
    vt = akv_ref[:, 128:256].T.astype(vgt_ref.dtype)
    vgt_ref[0, 0] = vt[:HEAD_D]
    vgt_ref[1, 0] = vt[HEAD_D:]


def _attn_prep(p, qgain, kgain, cos_t, sin_t, gmat):
    s_tot = p.shape[0]
    tm = ATT_TK
    hm = lambda nh, dd: jax.ShapeDtypeStruct((nh, s_tot, dd), BF16)
    hspec = lambda nh, dd: pl.BlockSpec((nh, tm, dd), lambda i: (0, i, 0))
    vt = lambda nh, dd: jax.ShapeDtypeStruct((nh, s_tot // tm, dd, tm), BF16)
    vtspec = lambda nh, dd: pl.BlockSpec((nh, 1, dd, tm), lambda i: (0, i, 0, 0))
    return pl.pallas_call(
        _attn_prep_kernel,
        name="attn_prep",
        out_shape=(hm(8, 64), hm(2, 64), vt(2, 64), hm(8, 64), hm(8, 64), vt(4, 128)),
        grid=(s_tot // tm,),
        in_specs=[pl.BlockSpec((tm, 512), lambda i: (i, 10)),
                  pl.BlockSpec((tm, 256), lambda i: (i, 22)),
                  pl.BlockSpec((tm, 512), lambda i: (i, 7)),
                  pl.BlockSpec((tm, 512), lambda i: (i, 8)),
                  pl.BlockSpec((tm, 512), lambda i: (i, 9)),
                  pl.BlockSpec((1, 128), lambda i: (0, 0)),
                  pl.BlockSpec((1, 128), lambda i: (0, 0)),
                  pl.BlockSpec((tm, 128), lambda i: (i, 0)),
                  pl.BlockSpec((tm, 128), lambda i: (i, 0)),
                  pl.BlockSpec((128, 128), lambda i: (0, 0))],
        out_specs=(hspec(8, 64), hspec(2, 64), vtspec(2, 64), hspec(8, 64), hspec(8, 64), vtspec(4, 128)),
        compiler_params=_cp("parallel"),
    )(p, p, p, p, p, qgain, kgain, cos_t, sin_t, gmat)


def _flash_t(pairs, vt_ref, n_val, is_ctx_tile, n_ctx_chunks, n_chunks, m_ref, l_ref, acc_ref, st_ref):
    tk = ATT_TK
    m_ref[...] = jnp.full_like(m_ref, -jnp.inf)
    l_ref[...] = jnp.zeros_like(l_ref)
    acc_ref[...] = jnp.zeros_like(acc_ref)

    n = jnp.where(is_ctx_tile, n_ctx_chunks, n_chunks)

    def scores(c, slot):
        off = pl.multiple_of(c * tk, tk)
        parts = [lax.dot_general(k_ref[kidx, pl.ds(off, tk), :], q, NT, preferred_element_type=F32)
                 for q, k_ref, kidx in pairs]
        st_ref[slot] = parts[0] if len(parts) == 1 else jnp.concatenate(parts, axis=1)

    def update(c, slot):
        st = st_ref[slot]
        m_prev = m_ref[...]
        m_new = jnp.maximum(m_prev, jnp.max(st, axis=0, keepdims=True))
        alpha = jnp.exp2(m_prev - m_new)
        pt = jnp.exp2(st - m_new)
        l_ref[...] = alpha * l_ref[...] + jnp.sum(pt, axis=0, keepdims=True)
        pb = pt.astype(BF16)
        w = pb.shape[1] // n_val
        for g in range(n_val):
            sl = slice(g * w, (g + 1) * w)
            acc_ref[:, sl] = alpha[:, sl] * acc_ref[:, sl] + _dot(vt_ref[g, c], pb[:, sl])
        m_ref[...] = m_new

    scores(0, 0)

    def body(it, carry):
        c = 2 * it
        scores(c + 1, 1)
        update(c, 0)
        scores(jnp.minimum(c + 2, n - 1), 0)
        update(c + 1, 1)
        return carry

    lax.fori_loop(0, n // 2, body, 0)

    @pl.when(n % 2 == 1)
    def _():
        update(n - 1, 0)

    return acc_ref[...] / l_ref[...]


def _gqa_kernel(q_ref, k_ref, vt_ref, o_ref, m_ref, l_ref, acc_ref, st_ref, *, tq, ctx, s_tot):
    q = q_ref[...].reshape(GQA_GROUP * tq, HEAD_D)
    ot = _flash_t([(q, k_ref, 0)], vt_ref, 1, pl.program_id(1) < ctx // tq, ctx // ATT_TK, s_tot // ATT_TK,
                  m_ref, l_ref, acc_ref, st_ref)
    o2 = jnp.concatenate([ot[:, h * tq:(h + 1) * tq] for h in range(GQA_GROUP)], axis=0)
    o_ref[...] = o2.T.astype(o_ref.dtype)


def _gqa_attention(qg, kg, vgt, *, ctx):
    nh, s_tot, _ = qg.shape
    tq = ATT_TK
    m = GQA_GROUP * tq
    return pl.pallas_call(
        functools.partial(_gqa_kernel, tq=tq, ctx=ctx, s_tot=s_tot),
        name="gqa_attention",
        out_shape=jax.ShapeDtypeStruct((s_tot, nh * HEAD_D), BF16),
        grid=(nh // GQA_GROUP, s_tot // tq),
        in_specs=[pl.BlockSpec((GQA_GROUP, tq, HEAD_D), lambda g, i: (g, i, 0)),
                  pl.BlockSpec((1, s_tot, HEAD_D), lambda g, i: (g, 0, 0)),
                  pl.BlockSpec((1, s_tot // ATT_TK, HEAD_D, ATT_TK), lambda g, i: (g, 0, 0, 0))],
        out_specs=pl.BlockSpec((tq, GQA_GROUP * HEAD_D), lambda g, i: (i, g)),
        scratch_shapes=[pltpu.VMEM((1, m), F32), pltpu.VMEM((1, m), F32), pltpu.VMEM((HEAD_D, m), F32),
                        pltpu.VMEM((2, ATT_TK, m), F32)],
        compiler_params=_cp("parallel", "parallel"),
    )(qg, kg, vgt)


def _diff_kernel(q_ref, k_ref, vt_ref, lam_ref, gain_ref, o_ref, m_ref, l_ref, acc_ref, st_ref, *, tq, ctx, s_tot,
                 lam_init):
    pairs = [(q_ref[j], k_ref, j) for j in range(2 * DF_HPS)]
    ot = _flash_t(pairs, vt_ref, DF_HPS, pl.program_id(1) < ctx // tq, ctx // ATT_TK, s_tot // ATT_TK,
                  m_ref, l_ref, acc_ref, st_ref)
    lv = lam_ref[...]
    lam = (jnp.exp(jnp.sum(lv[0:1] * lv[1:2], axis=1, keepdims=True))
           - jnp.exp(jnp.sum(lv[2:3] * lv[3:4], axis=1, keepdims=True)) + lam_init)
    dv = ot.shape[0]
    for g in range(DF_HPS):
        o = (ot[:, 2 * g * tq:(2 * g + 1) * tq] - lam * ot[:, (2 * g + 1) * tq:(2 * g + 2) * tq]).T
        y = o * lax.rsqrt(jnp.mean(o * o, axis=-1, keepdims=True) + EPS) * gain_ref[...]
        o_ref[:, g * dv:(g + 1) * dv] = (y * (1.0 - lam_init)).astype(o_ref.dtype)


def _diff_attention(qd, kd, vdt, lam_vecs, sub_gain, *, ctx, lam_init):
    nh, n_chunks, dv, _ = vdt.shape
    s_tot = qd.shape[1]
    tq = ATT_TK
    g = DF_HPS
    m = 2 * g * tq
    return pl.pallas_call(
        functools.partial(_diff_kernel, tq=tq, ctx=ctx, s_tot=s_tot, lam_init=lam_init),
        name="diff_attention",
        out_shape=jax.ShapeDtypeStruct((s_tot, nh * dv), BF16),
        grid=(nh // g, s_tot // tq),
        in_specs=[pl.BlockSpec((2 * g, tq, HEAD_D), lambda h, i: (h, i, 0)),
                  pl.BlockSpec((2 * g, s_tot, HEAD_D), lambda h, i: (h, 0, 0)),
                  pl.BlockSpec((g, n_chunks, dv, ATT_TK), lambda h, i: (h, 0, 0, 0)),
                  pl.BlockSpec(lam_vecs.shape, lambda h, i: (0, 0)),
                  pl.BlockSpec((1, dv), lambda h, i: (0, 0))],
        out_specs=pl.BlockSpec((tq, g * dv), lambda h, i: (i, h)),
        scratch_shapes=[pltpu.VMEM((1, m), F32), pltpu.VMEM((1, m), F32), pltpu.VMEM((dv, m), F32),
                        pltpu.VMEM((2, ATT_TK, m), F32)],
        compiler_params=_cp("parallel", "parallel"),
    )(qd, kd, vdt, lam_vecs, sub_gain)


def _hy_filter_kernel(w1_ref, b1_ref, w2_ref, b2_ref, w3_ref, b3_ref, w4_ref, fr_ref, o_ref, *, tp, l, n, width):
    i = pl.program_id(0)
    def position(idx):
        pos = jnp.where(idx < l, idx, n - idx).astype(F32)
        return pos / (l - 1.0), (2.0 * math.pi) * pos / l

    t_l, w_l = position(i * tp + lax.broadcasted_iota(jnp.int32, (1, tp), 1))
    nf = w1_ref.shape[1]
    r = lax.broadcasted_iota(jnp.int32, (nf, 1), 0)
    band = jnp.where(r <= HY_BANDS, r - 1, r - 1 - HY_BANDS).astype(F32)
    f = 1e-4 + band * ((HY_BANDS - 1 - 1e-4) / (HY_BANDS - 1))
    phase = jnp.where(r <= HY_BANDS, 0.5 * math.pi, math.pi)
    z = jnp.where(r == 0, t_l, jnp.where(r <= 2 * HY_BANDS, jnp.sin(f * w_l + phase), 0.0))
    fr = fr_ref[...]
    h = jnp.sin(fr * (_dot(w1_ref[...], z, HI) + b1_ref[...]))
    h = jnp.sin(fr * (_dot(w2_ref[...], h, HI) + b2_ref[...]))
    h = jnp.sin(fr * (_dot(w3_ref[...], h, HI) + b3_ref[...]))
    h = lax.dot_general(h, w4_ref[...], TN, precision=HI, preferred_element_type=F32)
    idx = i * tp + lax.broadcasted_iota(jnp.int32, (tp, 1), 0)
    t, _ = position(idx)
    ch = lax.broadcasted_iota(jnp.int32, (1, width), 1).astype(F32)
    min_decay = math.log(HY_TARGET) / HY_SLOW_DECAY
    max_decay = math.log(HY_TARGET) / HY_FAST_DECAY
    deltas = min_decay + ch * ((max_decay - min_decay) / (width - 1))
    window = jnp.exp(-t * jnp.abs(deltas))
    valid = jnp.logical_or(idx < l, idx > n - l)
    o_ref[...] = jnp.where(valid, h * window, 0.0)


def _hy_filter(filt, l, n):
    w1p, b1, w2, b2, w3, b3, w4, fr = filt
    width = w4.shape[1] // 2
    tp = min(512, l)
    full = lambda a: pl.BlockSpec(a.shape, lambda i: (0, 0))
    return pl.pallas_call(
        functools.partial(_hy_filter_kernel, tp=tp, l=l, n=n, width=width),
        name="hy_filter",
        out_shape=jax.ShapeDtypeStruct((n, width), F32),
        grid=(n // tp,),
        in_specs=[full(w1p), full(b1), full(w2), full(b2), full(w3), full(b3),
                  pl.BlockSpec((w4.shape[0], width), lambda i: (0, (i * tp >= l).astype(jnp.int32))),
                  full(fr)],
        out_specs=pl.BlockSpec((tp, width), lambda i: (i, 0)),
        compiler_params=_cp("parallel"),
    )(w1p, b1, w2, b2, w3, b3, w4, fr)


def _hy_ctx_kernel(x0_ref, x1_ref, v_ref, kern_ref, skip_ref, o_ref, ks_ref, u_ref, *, l):
    u = x1_ref[...] * v_ref[...]
    u_ref[...] = u
    base = jnp.concatenate([kern_ref[l:2 * l, :], kern_ref[0:l, :]], axis=0)
    ks_ref[0] = base
    for b in range(1, 8):
        ks_ref[b] = pltpu.roll(base, b, 0)

    def body(a, acc):
        start = pl.multiple_of(l - 8 * a, 8)
        ublk = u_ref[pl.ds(pl.multiple_of(8 * a, 8), 8), :]
        for b in range(8):
            acc = acc + ks_ref[b, pl.ds(start, l), :] * ublk[b:b + 1, :]
        return acc

    y = lax.fori_loop(0, l // 8, body, jnp.zeros((l, 128), F32))
    o_ref[...] = (x0_ref[...] * (y + u * skip_ref[...])).astype(o_ref.dtype)


def _hy_ctx(p, kern, skip, *, ctx):
    width = skip.shape[1]
    nb = width // 128
    return pl.pallas_call(
        functools.partial(_hy_ctx_kernel, l=ctx),
        name="hy_ctx_conv",
        out_shape=jax.ShapeDtypeStruct((ctx, width), BF16),
        grid=(nb,),
        in_specs=[pl.BlockSpec((ctx, 128), lambda c: (0, c)),
                  pl.BlockSpec((ctx, 128), lambda c: (0, nb + c)),
                  pl.BlockSpec((ctx, 128), lambda c: (0, 2 * nb + c)),
                  pl.BlockSpec((2 * ctx, 128), lambda c: (0, c)),
                  pl.BlockSpec((1, 128), lambda c: (0, c))],
        out_specs=pl.BlockSpec((ctx, 128), lambda c: (0, c)),
        scratch_shapes=[pltpu.VMEM((8, 2 * ctx, 128), F32), pltpu.VMEM((ctx, 128), F32)],
        compiler_params=_cp("parallel"),
    )(p, p, p, kern, skip)


def _hy_prep_kernel(x0_ref, x1_ref, v_ref, u_ref, x0o_ref):
    u_ref[...] = x1_ref[...] * v_ref[...]
    x0o_ref[...] = x0_ref[...]


def _hy_prep(p, *, ctx, width):
    s_lat = p.shape[0] - ctx
    tm = 256
    off = ctx // tm
    return pl.pallas_call(
        _hy_prep_kernel,
        name="hy_prep",
        out_shape=(jax.ShapeDtypeStruct((s_lat, width), F32), jax.ShapeDtypeStruct((s_lat, width), F32)),
        grid=(s_lat // tm,),
        in_specs=[pl.BlockSpec((tm, width), lambda i: (i + off, 0)),
                  pl.BlockSpec((tm, width), lambda i: (i + off, 1)),
                  pl.BlockSpec((tm, width), lambda i: (i + off, 2))],
        out_specs=(pl.BlockSpec((tm, width), lambda i: (i, 0)), pl.BlockSpec((tm, width), lambda i: (i, 0))),
        compiler_params=_cp("parallel"),
    )(p, p, p)


def _dft_consts(n1):
    n2 = FFT_N2
    n = n1 * n2
    f1n = n1 // 2 + 1
    f1p = -(-f1n // 8) * 8
    f1 = np.arange(f1p)[:, None].astype(np.float64)
    live = (f1 < f1n).astype(np.float64)
    t1 = np.arange(n1)[None, :]
    ang = 2 * np.pi * f1 * t1 / n1
    w_re, w_im = np.cos(ang) * live, -np.sin(ang) * live
    t0 = np.arange(n2)
    ang = 2 * np.pi * f1[None, :, :] * t0[:, None, None] / n
    tw_fwd = np.concatenate([np.cos(ang), -np.sin(ang)], axis=-1)
    ang = 2 * np.pi * f1[:, :, None] * t0[None, :, None] / n
    tw_inv = np.concatenate([np.cos(ang), np.sin(ang)], axis=-1)
    ang = 2 * np.pi * np.outer(t0, t0) / n2
    d_re, d_im = np.cos(ang), -np.sin(ang)
    wgt = np.where((f1 == 0) | (f1 == n1 // 2), 1.0, 2.0) * live
    t1o = np.arange(n1 // 2)[:, None]
    ang = 2 * np.pi * t1o * f1.T / n1
    c_m, s_m = np.cos(ang) * wgt.T / n, -np.sin(ang) * wgt.T / n
    f = lambda a: jnp.asarray(a, F32)
    return dict(f1p=f1p, w_re=f(w_re), w_im=f(w_im), tw_fwd=f(tw_fwd), tw_inv=f(tw_inv),
                d_re=f(d_re), d_im=f(d_im), c_m=f(c_m), s_m=f(s_m))


def _fft1_kernel(u_ref, wre_ref, wim_ref, tw_ref, bre_ref, bim_ref, *, tb, c):
    wre = _split_bf16(wre_ref[...])
    wim = _split_bf16(wim_ref[...])
    for b in range(tb):
        sl = slice(b * c, (b + 1) * c)
        u = _split_bf16(u_ref[:, sl])
        are = _mm3(wre, u)
        aim = _mm3(wim, u)
        tr = tw_ref[b, :, 0:1]
        ti = tw_ref[b, :, 1:2]
        bre_ref[:, sl] = are * tr - aim * ti
        bim_ref[:, sl] = are * ti + aim * tr


def _fft1(u2, w_re, w_im, tw_fwd, *, c):
    t1_in = u2.shape[0]
    f1p = w_re.shape[0]
    tb = 4
    out = jax.ShapeDtypeStruct((f1p, FFT_N2 * c), F32)
    return pl.pallas_call(
        functools.partial(_fft1_kernel, tb=tb, c=c),
        name="fft_outer",
        out_shape=(out, out),
        grid=(FFT_N2 // tb,),
        in_specs=[pl.BlockSpec((t1_in, tb * c), lambda i: (0, i)),
                  pl.BlockSpec((f1p, t1_in), lambda i: (0, 0)),
                  pl.BlockSpec((f1p, t1_in), lambda i: (0, 0)),
                  pl.BlockSpec((tb, f1p, 2), lambda i: (i, 0, 0))],
        out_specs=(pl.BlockSpec((f1p, tb * c), lambda i: (0, i)), pl.BlockSpec((f1p, tb * c), lambda i: (0, i))),
        compiler_params=_cp("parallel"),
    )(u2, w_re[:, :t1_in], w_im[:, :t1_in], tw_fwd)


def _fft_spec_kernel(bre_ref, bim_ref, dre_ref, dim_ref, xre_ref, xim_ref):
    br, bi = _split_bf16(bre_ref[0]), _split_bf16(bim_ref[0])
    dr, di = _split_bf16(dre_ref[...]), _split_bf16(dim_ref[...])
    xre_ref[0] = _mm3(dr, br) - _mm3(di, bi)
    xim_ref[0] = _mm3(dr, bi) + _mm3(di, br)


def _fft_spectrum(b_re, b_im, d_re, d_im):
    f1p, n2, c = b_re.shape
    blk = pl.BlockSpec((1, n2, c), lambda i: (i, 0, 0))
    mat = pl.BlockSpec((n2, n2), lambda i: (0, 0))
    out = jax.ShapeDtypeStruct((f1p, n2, c), F32)
    return pl.pallas_call(
        _fft_spec_kernel, name="fft_spectrum", out_shape=(out, out), grid=(f1p,),
        in_specs=[blk, blk, mat, mat], out_specs=(blk, blk),
        compiler_params=_cp("parallel"),
    )(b_re, b_im, d_re, d_im)


def _fft_mid_kernel(bre_ref, bim_ref, kre_ref, kim_ref, dre_ref, dim_ref, tw_ref, ore_ref, oim_ref):
    br, bi = _split_bf16(bre_ref[0]), _split_bf16(bim_ref[0])
    dr, di = _split_bf16(dre_ref[...]), _split_bf16(dim_ref[...])
    xr = _mm3(dr, br) - _mm3(di, bi)
    xi = _mm3(dr, bi) + _mm3(di, br)
    kr, ki = kre_ref[0], kim_ref[0]
    yr = _split_bf16(xr * kr - xi * ki)
    yi = _split_bf16(xr * ki + xi * kr)
    zr = _mm3(dr, yr) + _mm3(di, yi)
    zi = _mm3(dr, yi) - _mm3(di, yr)
    tr = tw_ref[0, :, 0:1]
    ti = tw_ref[0, :, 1:2]
    ore_ref[0] = zr * tr - zi * ti
    oim_ref[0] = zr * ti + zi * tr


def _fft_mid(b_re, b_im, k_re, k_im, d_re, d_im, tw_inv):
    f1p, n2, c = b_re.shape
    blk = pl.BlockSpec((1, n2, c), lambda i: (i, 0, 0))
    mat = pl.BlockSpec((n2, n2), lambda i: (0, 0))
    out = jax.ShapeDtypeStruct((f1p, n2, c), F32)
    return pl.pallas_call(
        _fft_mid_kernel, name="fft_mid", out_shape=(out, out), grid=(f1p,),
        in_specs=[blk, blk, blk, blk, mat, mat, pl.BlockSpec((1, n2, 2), lambda i: (i, 0, 0))],
        out_specs=(blk, blk),
        compiler_params=_cp("parallel"),
    )(b_re, b_im, k_re, k_im, d_re, d_im, tw_inv)


def _fft_fin_kernel(bre_ref, bim_ref, cm_ref, sm_ref, x0_ref, u_ref, skip_ref, o_ref, *, tb, c):
    y = (_mm3(_split_bf16(cm_ref[...]), _split_bf16(bre_ref[...]))
         + _mm3(_split_bf16(sm_ref[...]), _split_bf16(bim_ref[...])))
    skip = skip_ref[...]
    for b in range(tb):
        sl = slice(b * c, (b + 1) * c)
        o_ref[:, sl] = (x0_ref[:, sl] * (y[:, sl] + u_ref[:, sl] * skip)).astype(o_ref.dtype)


def _fft_fin(z_re, z_im, c_m, s_m, x0_2, u2, skip, *, c):
    f1p = z_re.shape[0]
    t1o = c_m.shape[0]
    tb = 4
    big = pl.BlockSpec((f1p, tb * c), lambda i: (0, i))
    sig = pl.BlockSpec((t1o, tb * c), lambda i: (0, i))
    mat = pl.BlockSpec((t1o, f1p), lambda i: (0, 0))
    return pl.pallas_call(
        functools.partial(_fft_fin_kernel, tb=tb, c=c),
        name="fft_final",
        out_shape=jax.ShapeDtypeStruct((t1o, FFT_N2 * c), BF16),
        grid=(FFT_N2 // tb,),
        in_specs=[big, big, mat, mat, sig, sig, pl.BlockSpec((1, c), lambda i: (0, 0))],
        out_specs=sig,
        compiler_params=_cp("parallel"),
    )(z_re, z_im, c_m, s_m, x0_2, u2, skip)


def _hyena_latent(p, kern, skip, *, ctx):
    c = skip.shape[1]
    s_lat = p.shape[0] - ctx
    n1 = 2 * s_lat // FFT_N2
    k = _dft_consts(n1)
    f1p = k["f1p"]
    u, x0 = _hy_prep(p, ctx=ctx, width=c)
    u2 = u.reshape(n1 // 2, FFT_N2 * c)
    x0_2 = x0.reshape(n1 // 2, FFT_N2 * c)
    kb_re, kb_im = _fft1(kern.reshape(n1, FFT_N2 * c), k["w_re"], k["w_im"], k["tw_fwd"], c=c)
    kf_re, kf_im = _fft_spectrum(kb_re.reshape(f1p, FFT_N2, c), kb_im.reshape(f1p, FFT_N2, c), k["d_re"], k["d_im"])
    b_re, b_im = _fft1(u2, k["w_re"], k["w_im"], k["tw_fwd"], c=c)
    z_re, z_im = _fft_mid(b_re.reshape(f1p, FFT_N2, c), b_im.reshape(f1p, FFT_N2, c), kf_re, kf_im,
                          k["d_re"], k["d_im"], k["tw_inv"])
    y2 = _fft_fin(z_re.reshape(f1p, FFT_N2 * c), z_im.reshape(f1p, FFT_N2 * c), k["c_m"], k["s_m"],
                  x0_2, u2, skip, c=c)
    return y2.reshape(s_lat, c)


def _softplus(x):
    return jnp.maximum(x, 0.0) + jnp.log(1.0 + jnp.exp(-jnp.abs(x)))


def _dotp(a, b, dims=(((1,), (0,)), ((), ())), passes=1):
    f = lambda x, y: lax.dot_general(x, y, dims, preferred_element_type=F32)
    if passes == 1:
        return f(a.astype(BF16), b.astype(BF16))
    ah, al = _split_bf16(a)
    bh, bl = _split_bf16(b)
    return f(ah, bh) + (f(al, bh) + f(ah, bl))


DN_U, DN_W, DN_QD, DN_KD, DN_QK = range(5)


def _dn_intra_kernel(q_ref, k_ref, v_ref, sm_ref, cst_ref, pk_ref, last_ref):
    c = DN_CHUNK
    nh = q_ref.shape[1] // DN_HEAD
    n_cb = q_ref.shape[0] // c
    r = nh * c
    row = lax.broadcasted_iota(jnp.int32, (c, c), 0)
    col = lax.broadcasted_iota(jnp.int32, (c, c), 1)
    tri = ((row >= col).astype(F32), (row <= col).astype(F32))
    zpad = jnp.zeros((128 - c, 128), F32)
    rr = lax.broadcasted_iota(jnp.int32, (r, r), 0)
    cc = lax.broadcasted_iota(jnp.int32, (r, r), 1)
    sh = int(math.log2(c))
    same = jnp.right_shift(rr, sh) == jnp.right_shift(cc, sh)
    incl = (jnp.logical_and(same, rr >= cc), jnp.logical_and(same, rr <= cc))
    strict = (jnp.logical_and(same, rr > cc), jnp.logical_and(same, rr < cc))

    units = []
    for cb in range(n_cb):
        ts = slice(cb * c, (cb + 1) * c)
        sm = sm_ref[ts, :]
        beta_all = jax.nn.sigmoid(sm)
        g_all = -jnp.exp(cst_ref[0:1, :]) * _softplus(sm + cst_ref[1:2, :])
        tot_all = jnp.sum(g_all, axis=0, keepdims=True)
        qs, ks, vs = [], [], []
        for h in range(nh):
            hs = slice(h * DN_HEAD, (h + 1) * DN_HEAD)
            qh = _silu(q_ref[ts, hs])
            kh = _silu(k_ref[ts, hs])
            qs.append(qh * lax.rsqrt(jnp.sum(qh * qh, axis=-1, keepdims=True) + EPS) * (DN_HEAD ** -0.5))
            ks.append(kh * lax.rsqrt(jnp.sum(kh * kh, axis=-1, keepdims=True) + EPS))
            vs.append(_silu(v_ref[ts, hs]))
        q_rows = jnp.concatenate(qs, axis=0)
        k_rows = jnp.concatenate(ks, axis=0)
        v_rows = jnp.concatenate(vs, axis=0)
        qkt = _dotp(q_rows, k_rows, NT)
        for d in range(2):
            gc = _dot(tri[d], g_all, HI)
            gct = jnp.concatenate([gc, zpad], axis=0).T
            bis = [d * nh + h for h in range(nh)]
            gis = [2 * nh + b for b in bis]
            beta = jnp.concatenate([beta_all[:, b:b + 1] for b in bis], axis=0)
            gcol = jnp.concatenate([gc[:, g:g + 1] for g in gis], axis=0)
            grow = jnp.concatenate([gct[g:g + 1, :c] for g in gis], axis=1)
            gtot = jnp.concatenate([jnp.broadcast_to(tot_all[:, g:g + 1], (c, 1)) for g in gis], axis=0)
            gamma = jnp.where(incl[d], jnp.exp(jnp.where(incl[d], gcol - grow, 0.0)), 0.0)
            egc = jnp.exp(gcol)
            kb = k_rows * beta
            units.append(dict(
                cb=cb, d=d, tot=tot_all, qdec=q_rows * egc, kd=k_rows * jnp.exp(gtot - gcol), qk=qkt * gamma,
                pw=-jnp.where(strict[d], _dotp(kb, k_rows, NT) * gamma, 0.0),
                sol=jnp.concatenate([v_rows * beta, kb * egc], axis=1)))
    steps = int(math.log2(c))
    for t in range(steps):
        for un in units:
            un["sol"] = un["sol"] + _dotp(un["pw"], un["sol"], passes=DN_CHAIN_PASSES)
        if t + 1 < steps:
            for un in units:
                un["pw"] = _dotp(un["pw"], un["pw"], passes=DN_CHAIN_PASSES)
    for un in units:
        d, cb = un["d"], un["cb"]
        ts = slice(cb * c, (cb + 1) * c)
        for h in range(nh):
            hs = slice(h * DN_HEAD, (h + 1) * DN_HEAD)
            rs = slice(h * c, (h + 1) * c)
            gi = 2 * nh + d * nh + h
            wd = nh * DN_HEAD

            def put(k, val):
                pk_ref[d, ts, k * wd + h * DN_HEAD:k * wd + (h + 1) * DN_HEAD] = val

            put(DN_U, un["sol"][rs, :DN_HEAD])
            put(DN_W, un["sol"][rs, DN_HEAD:])
            put(DN_QD, un["qdec"][rs])
            put(DN_KD, un["kd"][rs])
            pk_ref[d, ts, DN_QK * wd + h * c:DN_QK * wd + (h + 1) * c] = un["qk"][rs, rs]
            last_ref[d, cb, :, hs] = jnp.broadcast_to(jnp.exp(un["tot"][:, gi:gi + 1]), (8, DN_HEAD))


def _dn_intra(p, cst):
    s_tot = p.shape[0]
    c = DN_CHUNK
    w = 512
    nch = s_tot // c
    n_cb = DN_INTRA_CHUNKS
    rows = n_cb * c
    assert nch % n_cb == 0
    pw = 4 * w + w // 2
    return pl.pallas_call(
        _dn_intra_kernel,
        name="dn_intra",
        out_shape=(jax.ShapeDtypeStruct((2, s_tot, pw), F32), jax.ShapeDtypeStruct((2, nch, 8, w), F32)),
        grid=(nch // n_cb,),
        in_specs=[pl.BlockSpec((rows, w), lambda j: (j, 3)),
                  pl.BlockSpec((rows, w), lambda j: (j, 4)),
                  pl.BlockSpec((rows, w), lambda j: (j, 5)),
                  pl.BlockSpec((rows, 128), lambda j: (j, 46)),
                  pl.BlockSpec((8, 128), lambda j: (0, 0))],
        out_specs=(pl.BlockSpec((2, rows, pw), lambda j: (0, j, 0)),
                   pl.BlockSpec((2, n_cb, 8, w), lambda j: (0, j, 0, 0))),
        compiler_params=_cp("parallel"),
    )(p, p, p, p, cst)


def _dn_rec_kernel(pkf_ref, lastf_ref, pkb_ref, lastb_ref, of_ref, ob_ref, s_ref, *, nh):
    pk_refs, last_refs, o_refs = (pkf_ref, pkb_ref), (lastf_ref, lastb_ref), (of_ref, ob_ref)

    @pl.when(pl.program_id(0) == 0)
    def _():
        s_ref[...] = jnp.zeros_like(s_ref)

    c = DN_CHUNK
    width = nh * DN_HEAD
    n_sub = pkf_ref.shape[1] // c
    chains = [(d, h) for d in range(2) for h in range(nh)]
    hcols = lambda h: slice(h * DN_HEAD, (h + 1) * DN_HEAD)
    s = [s_ref[d * nh + h] for d, h in chains]
    for sub in range(n_sub):
        cidx = (sub, n_sub - 1 - sub)
        rows = [slice(ci * c, (ci + 1) * c) for ci in cidx]
        part = lambda d, k, h: pk_refs[d][0, rows[d], k * width + h * DN_HEAD:k * width + (h + 1) * DN_HEAD]
        v_new = [part(d, DN_U, h) - _dotp(part(d, DN_W, h), s[i]) for i, (d, h) in enumerate(chains)]
        qs = [_dotp(part(d, DN_QD, h), s[i]) for i, (d, h) in enumerate(chains)]
        for i, (d, h) in enumerate(chains):
            qk = pk_refs[d][0, rows[d], DN_QK * width + h * c:DN_QK * width + (h + 1) * c]
            o_refs[d][rows[d], hcols(h)] = qs[i] + _dotp(qk, v_new[i])
        s = [s[i] * last_refs[d][0, cidx[d], 0:1, hcols(h)] + _dotp(part(d, DN_KD, h), v_new[i], TN)
             for i, (d, h) in enumerate(chains)]
    for i, (d, h) in enumerate(chains):
        s_ref[d * nh + h] = s[i]


def _dn_recurrence(pk, last, *, ctx):
    _, s_tot, pw = pk.shape
    width = last.shape[-1]
    c = DN_CHUNK
    n_sub = DN_SCAN_CHUNKS
    nblk = s_tot // (c * n_sub)
    nctx = ctx // (c * n_sub)
    assert s_tot % (c * n_sub) == 0 and ctx % (c * n_sub) == 0
    nh = width // DN_HEAD
    fwd = lambda j: j
    bwd = lambda j: jnp.where(j < nctx, nctx - 1 - j, nblk - 1 - (j - nctx))
    specs = []
    for d, cm in ((0, fwd), (1, bwd)):
        specs += [pl.BlockSpec((1, c * n_sub, pw), lambda j, cm=cm, d=d: (d, cm(j), 0)),
                  pl.BlockSpec((1, n_sub, 8, width), lambda j, cm=cm, d=d: (d, cm(j), 0, 0))]
    out = jax.ShapeDtypeStruct((s_tot, width), F32)
    return pl.pallas_call(
        functools.partial(_dn_rec_kernel, nh=nh),
        name="dn_scan",
        out_shape=(out, out),
        grid=(nblk,),
        in_specs=specs,
        out_specs=(pl.BlockSpec((c * n_sub, width), lambda j: (fwd(j), 0)),
                   pl.BlockSpec((c * n_sub, width), lambda j: (bwd(j), 0))),
        scratch_shapes=[pltpu.VMEM((2 * nh, DN_HEAD, DN_HEAD), F32)],
        compiler_params=_cp("arbitrary"),
    )(pk, last, pk, last)


def _dn_finish_kernel(of_ref, ob_ref, gate_ref, gain_ref, o_ref):
    nh = of_ref.shape[1] // DN_HEAD
    for h in range(nh):
        hs = slice(h * DN_HEAD, (h + 1) * DN_HEAD)
        o = of_ref[:, hs] + ob_ref[:, hs]
        y = o * lax.rsqrt(jnp.mean(o * o, axis=-1, keepdims=True) + EPS) * gain_ref[...]
        o_ref[:, hs] = (y * _silu(gate_ref[:, hs])).astype(o_ref.dtype)


def _dn_finish(o_f, o_b, p, gain):
    s_tot, width = o_f.shape
    tm = _row_tile(s_tot, (528, 640, 256, 128))
    blk = pl.BlockSpec((tm, width), lambda i: (i, 0))
    return pl.pallas_call(
        _dn_finish_kernel,
        name="dn_finish",
        out_shape=jax.ShapeDtypeStruct((s_tot, width), BF16),
        grid=(s_tot // tm,),
        in_specs=[blk, blk, pl.BlockSpec((tm, width), lambda i: (i, 6)), pl.BlockSpec((1, DN_HEAD), lambda i: (0, 0))],
        out_specs=blk,
        compiler_params=_cp("parallel"),
    )(o_f, o_b, p, gain)


def _rope_tables(ctx, s_lat):
    rows = s_lat // GRID_W
    n_freq = HEAD_D // 4
    row = np.repeat(np.arange(rows, dtype=np.float32), GRID_W)
    col = np.tile(np.arange(GRID_W, dtype=np.float32), rows)
    inv = (ROPE_THETA ** (-np.arange(n_freq, dtype=np.float32) / n_freq)).astype(np.float32)
    ang = np.concatenate([row[:, None] * inv, col[:, None] * inv], axis=-1).astype(np.float32)
    cos, sin = np.cos(ang), np.sin(ang)
    cos = np.concatenate([np.ones((ctx, HEAD_D // 2), np.float32), cos], axis=0)
    sin = np.concatenate([np.zeros((ctx, HEAD_D // 2), np.float32), sin], axis=0)
    cos_t = np.tile(np.concatenate([cos, cos], axis=1), (1, 2))
    sin_t = np.tile(np.concatenate([-sin, sin], axis=1), (1, 2))
    return jnp.asarray(cos_t, F32), jnp.asarray(sin_t, F32)


def _pad_rows(a, rows):
    return jnp.concatenate([a, jnp.zeros((rows - a.shape[0],) + a.shape[1:], a.dtype)], axis=0)


def kernel(x, c, ctx, c_ctx, w_ada, b_ada, norm_mix_pre, norm_mix_post, norm_ffn_pre, norm_ffn_post, w_in, w_out, attn_q_norm, attn_k_norm, hy_short, hy_w1, hy_b1, hy_w2, hy_b2, hy_w3, hy_b3, hy_w4, hy_freq, hy_skip, dn_short, dn_a_log, dn_dt_bias, dn_norm, df_lambda, df_norm, ffn_up, ffn_conv, ffn_down):
    batch, s_lat, d = x.shape
    n_ctx = ctx.shape[1]
    depth = w_in.shape[0]
    gw = d // 4
    assert batch == 1 and gw == 512 and n_ctx % 256 == 0 and s_lat % 512 == 0
    s_tot = n_ctx + s_lat

    xx = jnp.concatenate([ctx[0], x[0]], axis=0)
    c_rows = _pad_rows(jnp.concatenate([c, c_ctx[None, :]], axis=0), 8)
    mods = _mod_vectors(c_rows, w_ada, b_ada)

    cos_t, sin_t = _rope_tables(n_ctx, s_lat)
    lane = np.arange(128)
    gmat = jnp.asarray((lane[:, None] // HEAD_D == lane[None, :] // HEAD_D) / HEAD_D, F32)
    ident = jnp.asarray([[0.0], [1.0], [0.0]], F32)

    a_cols = gw + 2 * 2 * HEAD_D
    o_hy, o_dn, o_df = a_cols, a_cols + 3 * gw, a_cols + 3 * gw + 4 * gw + 16
    order = [(o_hy, o_hy + 3 * gw), (o_dn, o_dn + 4 * gw), (o_df, o_df + 3 * gw), (0, a_cols),
             (o_dn + 4 * gw, o_dn + 4 * gw + 16)]
    n_proj = 12 * gw
    used = sum(b - a for a, b in order)

    for i in range(depth):
        lam_init = 0.8 - 0.6 * math.exp(-0.3 * i)
        mod = mods[i]
        sh1, sc1, gt1, sh2, sc2, gt2 = (mod[:, k * d:(k + 1) * d] for k in range(6))
        mod_mix = _pad_rows(jnp.concatenate([sh1[0:1], sc1[0:1], sh1[1:2], sc1[1:2]], axis=0), 8)
        mod_ffn = _pad_rows(jnp.concatenate([sh2[0:1], sc2[0:1], sh2[1:2], sc2[1:2]], axis=0), 8)

        w_in_p = jnp.concatenate([w_in[i][:, a:b] for a, b in order]
                                 + [jnp.zeros((d, n_proj - used), F32)], axis=1).astype(BF16)
        cw_in = jnp.concatenate([hy_short[i], dn_short[i], jnp.broadcast_to(ident, (3, n_proj - 6 * gw))], axis=1)
        p = _norm_mod_matmul_conv(xx, norm_mix_pre[i][None, :], mod_mix, w_in_p, _pad_rows(cw_in, 8),
                                  ctx=n_ctx, ffn=False, conv_cols=6 * gw)

        qg, kg, vgt, qd, kd, vdt = _attn_prep(p, jnp.tile(attn_q_norm[i], 2)[None, :],
                                              jnp.tile(attn_k_norm[i], 2)[None, :], cos_t, sin_t, gmat)
        ya = _gqa_attention(qg, kg, vgt, ctx=n_ctx)
        yd = _diff_attention(qd, kd, vdt, df_lambda[i], df_norm[i][None, :], ctx=n_ctx, lam_init=lam_init)

        filt = (_pad_rows(hy_w1[i], 40).T, hy_b1[i][:, None], hy_w2[i].T, hy_b2[i][:, None], hy_w3[i].T,
                hy_b3[i][:, None], hy_w4[i], hy_freq[i][:, None])
        skip = hy_skip[i][None, :]
        yb_lat = _hyena_latent(p, _hy_filter(filt, s_lat, 2 * s_lat), skip, ctx=n_ctx)
        if i < depth - 1:
            yb_ctx = _hy_ctx(p, _hy_filter(filt, n_ctx, 2 * n_ctx), skip, ctx=n_ctx)
        else:
            yb_ctx = jnp.zeros((n_ctx, gw), BF16)

        zeros8 = jnp.zeros((8,), F32)
        cst = _pad_rows(jnp.stack([jnp.concatenate([zeros8, dn_a_log[i].reshape(-1), jnp.zeros((112,), F32)]),
                                   jnp.concatenate([zeros8, dn_dt_bias[i].reshape(-1), jnp.zeros((112,), F32)])]), 8)
        o_f, o_b = _dn_recurrence(*_dn_intra(p, cst), ctx=n_ctx)
        yc = _dn_finish(o_f, o_b, p, dn_norm[i][None, :])

        xx = _out_proj(ya, yb_ctx, yb_lat, yc, yd, w_out[i].astype(BF16), xx, gt1, norm_mix_post[i][None, :],
                       ctx=n_ctx)

        g = _norm_mod_matmul_conv(xx, norm_ffn_pre[i][None, :], mod_ffn, ffn_up[i].astype(BF16),
                                  _pad_rows(ffn_conv[i], 8), ctx=n_ctx, ffn=True, conv_cols=ffn_down.shape[1])
        xx = _matmul_norm_residual(g, ffn_down[i].astype(BF16), xx, gt2, norm_ffn_post[i][None, :], ctx=n_ctx)

    return xx[n_ctx:][None]
```
